```python
import jax, jax.numpy as jnp
from jax import lax
import numpy as np

D_MODEL = 1024
BATCH = 8
SEQ = 2048
DEPTH = 4
DEC_BATCH = 128
DEC_SEQ = 8
PAST_LEN = 16384
PAGE_SIZE = 128

M_HEADS = 4
M_DK = 128
M_DV = 256
M_QK = M_HEADS * M_DK
M_V = M_HEADS * M_DV
CHUNK = 64
F_BIAS = 3.0
R_HEADS = 16
R_HD = 64
R_W = R_HEADS * R_HD
R_LORA_W = 64
R_LORA_A = 64
R_LORA_G = 128
D_FF = 2816
CONV_W = 3
RMS_EPS = 1e-6
GN_EPS = 64e-5

MIX_SIZES = [M_QK, M_QK, M_V, M_V, M_HEADS, M_HEADS, D_MODEL, D_MODEL]
RWKV_SIZES = [R_W, R_W, R_W, R_LORA_W, R_LORA_A, R_LORA_G]
MIX_COLS = sum(MIX_SIZES)
RWKV_COLS = sum(RWKV_SIZES)
TOTAL_COLS = MIX_COLS + RWKV_COLS
F_OFF = 2 * M_QK + 2 * M_V + M_HEADS

kernel_name = 'mlstm_rwkv7_gated_hybrid_step'


def _split_points(sizes):
    return [int(v) for v in np.cumsum(sizes)[:-1]]


def _rmsnorm(x, g):
    xf = x.astype(jnp.float32)
    y = xf * lax.rsqrt(jnp.mean(xf * xf, axis=-1, keepdims=True) + RMS_EPS)
    return (y * g.astype(jnp.float32)).astype(x.dtype)


def _mlstm_chunkwise(q, k, v, i_pre, f_pre, C0, n0, m0):
    B, T, H, _ = q.shape
    L = CHUNK if T % CHUNK == 0 else T
    NC = T // L

    def chunks4(a):
        return a.reshape(B, NC, L, H, a.shape[-1]).transpose(1, 0, 3, 2, 4)

    def chunks3(a):
        return a.reshape(B, NC, L, H).transpose(1, 0, 3, 2)

    causal = jnp.tril(jnp.ones((L, L), dtype=bool))

    def step(carry, inp):
        C, n, m = carry
        qc, kc, vc, ic, fc = inp
        b = jnp.cumsum(jax.nn.log_sigmoid(fc), axis=-1)
        dmat = jnp.where(causal, b[..., :, None] - b[..., None, :] + ic[..., None, :], -jnp.inf)
        m_inter = b + m[..., None]
        m_t = jnp.maximum(m_inter, jnp.max(dmat, axis=-1))
        s = jnp.einsum('bhtd,bhsd->bhts', qc, kc) * jnp.exp(dmat - m_t[..., None])
        inter = jnp.exp(m_inter - m_t)
        num = jnp.einsum('bhts,bhsv->bhtv', s, vc) + inter[..., None] * jnp.einsum('bhvd,bhtd->bhtv', C, qc)
        den = jnp.sum(s, axis=-1) + inter * jnp.einsum('bhd,bhtd->bht', n, qc)
        h = num / jnp.maximum(jnp.abs(den), jnp.exp(-m_t))[..., None]
        m_new = m_t[..., -1]
        w_s = jnp.exp(b[..., -1:] - b + ic - m_new[..., None])
        decay = jnp.exp(b[..., -1] + m - m_new)
        C_new = decay[..., None, None] * C + jnp.einsum('bhs,bhsv,bhsd->bhvd', w_s, vc, kc)
        n_new = decay[..., None] * n + jnp.einsum('bhs,bhsd->bhd', w_s, kc)
        return (C_new, n_new, m_new), h

    xs = (chunks4(q * (M_DK ** -0.5)), chunks4(k), chunks4(v), chunks3(i_pre), chunks3(f_pre))
    (C1, n1, m1), hs = lax.scan(step, (C0, n0, m0), xs)
    h = hs.transpose(1, 0, 3, 2, 4).reshape(B, T, H, -1)
    return h, C1, n1, m1


def _rwkv7_recurrence(r, decay, k, v, kk, kka, S0):
    def step(S, inp):
        r_t, w_t, k_t, v_t, kk_t, kka_t = inp
        sa = jnp.einsum('bhvk,bhk->bhv', S, -kk_t)
        S = S * w_t[..., None, :] + sa[..., None] * kka_t[..., None, :] + v_t[..., None] * k_t[..., None, :]
        return S, jnp.einsum('bhvk,bhk->bhv', S, r_t)

    xs = tuple(jnp.moveaxis(a, 1, 0) for a in (r, decay, k, v, kk, kka))
    S1, ys = lax.scan(step, S0, xs)
    return jnp.moveaxis(ys, 0, 1), S1


def _token_mix(h, shift_prev, C0, n0, m0, S0, w_in, b_in, mu_shift, m_norm_g,
               r_w0, r_w_up, r_a0, r_a_up, r_g_up, r_k_k, r_k_a, r_r_k, r_gn_g, r_gn_b,
               w_br_m, w_br_r, w_out):
    f32 = jnp.float32
    B, T, _ = h.shape
    P = h @ w_in + b_in
    P_mix, P_rw = P[..., :MIX_COLS], P[..., MIX_COLS:]
    prev = jnp.concatenate([shift_prev[:, None, :].astype(P.dtype), P_rw[:, :-1]], axis=1)
    P_rw_s = P_rw + (prev - P_rw) * mu_shift
    new_shift = P_rw[:, -1]

    mq, mk, mv, mo, mi, mf, gm, gr = jnp.split(P_mix, _split_points(MIX_SIZES), axis=-1)
    hm, C1, n1, m1 = _mlstm_chunkwise(
        mq.reshape(B, T, M_HEADS, M_DK).astype(f32), mk.reshape(B, T, M_HEADS, M_DK).astype(f32),
        mv.reshape(B, T, M_HEADS, M_DV).astype(f32), mi.astype(f32), mf.astype(f32),
        C0.astype(f32), n0.astype(f32), m0.astype(f32))
    hm = hm * lax.rsqrt(jnp.mean(hm * hm, axis=-1, keepdims=True) + RMS_EPS) * m_norm_g.astype(f32)
    hm = (hm.reshape(B, T, M_V) * jax.nn.sigmoid(mo.astype(f32))).astype(h.dtype)
    branch_m = hm @ w_br_m

    rr, rk, rv, wd, ad, gd = jnp.split(P_rw_s, _split_points(RWKV_SIZES), axis=-1)
    w_log = -jax.nn.softplus(-(r_w0 + jnp.tanh(wd) @ r_w_up).astype(f32)) - 0.5
    decay = jnp.exp(-jnp.exp(w_log))
    a = jax.nn.sigmoid((r_a0 + ad @ r_a_up).astype(f32))
    g = (jax.nn.sigmoid(gd) @ r_g_up).astype(f32)
    rk32 = rk.astype(f32)
    kk = (rk32 * r_k_k.astype(f32)).reshape(B, T, R_HEADS, R_HD)
    kk = kk / jnp.maximum(jnp.sqrt(jnp.sum(kk * kk, axis=-1, keepdims=True)), 1e-12)
    kmod = rk32 * (1.0 + (a - 1.0) * r_k_a.astype(f32))
    heads = lambda t: t.reshape(B, T, R_HEADS, R_HD)
    r4, d4, k4, v4, a4 = heads(rr.astype(f32)), heads(decay), heads(kmod), heads(rv.astype(f32)), heads(a)
    y, S1 = _rwkv7_recurrence(r4, d4, k4, v4, kk, kk * a4, S0.astype(f32))
    y_mean = jnp.mean(y, axis=-1, keepdims=True)
    y_var = jnp.mean(jnp.square(y - y_mean), axis=-1, keepdims=True)
    yn = ((y - y_mean) * lax.rsqrt(y_var + GN_EPS)).reshape(B, T, R_W) * r_gn_g.astype(f32) + r_gn_b.astype(f32)
    bonus = (jnp.sum(r4 * k4 * r_r_k.astype(f32), axis=-1, keepdims=True) * v4).reshape(B, T, R_W)
    yr = ((yn + bonus) * g).astype(h.dtype)
    branch_r = yr @ w_br_r

    merged = jax.nn.sigmoid(gm) * branch_m + jax.nn.sigmoid(gr) * branch_r
    return merged @ w_out, (C1, n1, m1, S1, new_shift)


def _conv_ffn(h, buf, w_up, conv_w, conv_b, w_down):
    T = h.shape[1]
    U = h @ w_up
    Ucat = jnp.concatenate([buf.astype(U.dtype), U], axis=1)
    conv = conv_b + sum(Ucat[:, j:j + T] * conv_w[j] for j in range(CONV_W))
    act = jax.nn.silu(conv[..., :D_FF]) * conv[..., D_FF:]
    return act @ w_down, Ucat[:, -(CONV_W - 1):]


def _trunk(x, st_C, st_n, st_m, st_S, st_shift, st_conv,
           norm_mix_g, w_in, b_in, mu_shift, m_norm_g, r_w0, r_w_up, r_a0, r_a_up, r_g_up,
           r_k_k, r_k_a, r_r_k, r_gn_g, r_gn_b, w_br_m, w_br_r, w_out,
           norm_ffn_g, w_up, conv_w, conv_b, w_down, norm_final_g):
    outs = [[] for _ in range(6)]
    for l in range(DEPTH):
        mix, (C1, n1, m1, S1, sh1) = _token_mix(
            _rmsnorm(x, norm_mix_g[l]), st_shift[l], st_C[l], st_n[l], st_m[l], st_S[l],
            w_in[l], b_in[l], mu_shift[l], m_norm_g[l], r_w0[l], r_w_up[l], r_a0[l], r_a_up[l],
            r_g_up[l], r_k_k[l], r_k_a[l], r_r_k[l], r_gn_g[l], r_gn_b[l], w_br_m[l], w_br_r[l], w_out[l])
        x = x + mix
        ffn, cv1 = _conv_ffn(_rmsnorm(x, norm_ffn_g[l]), st_conv[l], w_up[l], conv_w[l], conv_b[l], w_down[l])
        x = x + ffn
        for lst, val in zip(outs, (C1, n1, m1, S1, sh1, cv1)):
            lst.append(val)
    return _rmsnorm(x, norm_final_g), [jnp.stack(o) for o in outs]


def setup_inputs(seed: int = 0) -> dict:
    key = jax.random.key(seed)
    ks = iter(jax.random.split(key, 48))
    nrm = lambda shape, s: jax.random.normal(next(ks), shape, jnp.float32) * s
    uni = lambda shape: jax.random.uniform(next(ks), shape, jnp.float32)
    b_in = nrm((DEPTH, TOTAL_COLS), 0.02).at[:, F_OFF:F_OFF + M_HEADS].add(F_BIAS)
    return {
        'x_prompt': nrm((BATCH, SEQ, D_MODEL), 1.0),
        'x_sample': nrm((DEC_BATCH, DEC_SEQ, D_MODEL), 1.0),
        'state_mlstm_C': nrm((DEPTH, DEC_BATCH, M_HEADS, M_DV, M_DK), 0.1),
        'state_mlstm_n': nrm((DEPTH, DEC_BATCH, M_HEADS, M_DK), 0.1),
        'state_mlstm_m': nrm((DEPTH, DEC_BATCH, M_HEADS), 1.0),
        'state_rwkv_S': nrm((DEPTH, DEC_BATCH, R_HEADS, R_HD, R_HD), 0.1),
        'state_rwkv_shift': nrm((DEPTH, DEC_BATCH, RWKV_COLS), 1.0),
        'state_ffn_conv': nrm((DEPTH, DEC_BATCH, CONV_W - 1, 2 * D_FF), 1.0),
        'norm_mix_g': 1.0 + nrm((DEPTH, D_MODEL), 0.02),
        'w_in': nrm((DEPTH, D_MODEL, TOTAL_COLS), D_MODEL ** -0.5),
        'b_in': b_in,
        'mu_shift': uni((DEPTH, RWKV_COLS)),
        'm_norm_g': 1.0 + nrm((DEPTH, M_HEADS, M_DV), 0.02),
        'r_w0': nrm((DEPTH, R_W), 0.5),
        'r_w_up': nrm((DEPTH, R_LORA_W, R_W), 0.5 * R_LORA_W ** -0.5),
        'r_a0': nrm((DEPTH, R_W), 0.5),
        'r_a_up': nrm((DEPTH, R_LORA_A, R_W), R_LORA_A ** -0.5),
        'r_g_up': nrm((DEPTH, R_LORA_G, R_W), R_LORA_G ** -0.5),
        'r_k_k': 0.85 + nrm((DEPTH, R_W), 0.05),
        'r_k_a': 1.0 + nrm((DEPTH, R_W), 0.05),
        'r_r_k': nrm((DEPTH, R_HEADS, R_HD), 0.1),
        'r_gn_g': 1.0 + nrm((DEPTH, R_W), 0.02),
        'r_gn_b': nrm((DEPTH, R_W), 0.02),
        'w_br_m': nrm((DEPTH, M_V, D_MODEL), M_V ** -0.5),
        'w_br_r': nrm((DEPTH, R_W, D_MODEL), R_W ** -0.5),
        'w_out': nrm((DEPTH, D_MODEL, D_MODEL), 0.5 * D_MODEL ** -0.5),
        'norm_ffn_g': 1.0 + nrm((DEPTH, D_MODEL), 0.02),
        'w_up': nrm((DEPTH, D_MODEL, 2 * D_FF), D_MODEL ** -0.5),
        'conv_w': nrm((DEPTH, CONV_W, 2 * D_FF), CONV_W ** -0.5),
        'conv_b': nrm((DEPTH, 2 * D_FF), 0.02),
        'w_down': nrm((DEPTH, D_FF, D_MODEL), 0.5 * D_FF ** -0.5),
        'norm_final_g': 1.0 + nrm((D_MODEL,), 0.02),
    }


def reference(x_prompt, x_sample, state_mlstm_C, state_mlstm_n, state_mlstm_m, state_rwkv_S,
              state_rwkv_shift, state_ffn_conv, norm_mix_g, w_in, b_in, mu_shift, m_norm_g,
              r_w0, r_w_up, r_a0, r_a_up, r_g_up, r_k_k, r_k_a, r_r_k, r_gn_g, r_gn_b,
              w_br_m, w_br_r, w_out, norm_ffn_g, w_up, conv_w, conv_b, w_down, norm_final_g):
    Bp = x_prompt.shape[0]
    f32 = jnp.float32
    zC = jnp.zeros((DEPTH, Bp, M_HEADS, M_DV, M_DK), f32)
    zn = jnp.zeros((DEPTH, Bp, M_HEADS, M_DK), f32)
    zm = jnp.zeros((DEPTH, Bp, M_HEADS), f32)
    zS = jnp.zeros((DEPTH, Bp, R_HEADS, R_HD, R_HD), f32)
    zsh = jnp.zeros((DEPTH, Bp, RWKV_COLS), x_prompt.dtype)
    zcv = jnp.zeros((DEPTH, Bp, CONV_W - 1, 2 * D_FF), x_prompt.dtype)
    y_prompt, (pC, pn, pm, pS, psh, pcv) = _trunk(
        x_prompt, zC, zn, zm, zS, zsh, zcv,
        norm_mix_g, w_in, b_in, mu_shift, m_norm_g, r_w0, r_w_up, r_a0, r_a_up, r_g_up,
        r_k_k, r_k_a, r_r_k, r_gn_g, r_gn_b, w_br_m, w_br_r, w_out,
        norm_ffn_g, w_up, conv_w, conv_b, w_down, norm_final_g)
    y_sample, (sC, sn, sm, sS, ssh, scv) = _trunk(
        x_sample, state_mlstm_C, state_mlstm_n, state_mlstm_m, state_rwkv_S, state_rwkv_shift, state_ffn_conv,
        norm_mix_g, w_in, b_in, mu_shift, m_norm_g, r_w0, r_w_up, r_a0, r_a_up, r_g_up,
        r_k_k, r_k_a, r_r_k, r_gn_g, r_gn_b, w_br_m, w_br_r, w_out,
        norm_ffn_g, w_up, conv_w, conv_b, w_down, norm_final_g)
    return (y_prompt, y_sample, pC, pn, pm, pS, psh, pcv, sC, sn, sm, sS, ssh, scv)
```

```python
import functools

import jax
import jax.numpy as jnp
from jax import lax
from jax.experimental import pallas as pl
from jax.experimental.pallas import tpu as pltpu

F32 = jnp.float32
BF16 = jnp.bfloat16

M_HEADS = 4
M_DK = 128
M_DV = 256
M_CHUNK = 64
R_HEADS = 16
R_HD = 64
R_LORA = 256
RMS_EPS = 1e-6
GN_EPS = 64e-5

LANES = 128
SUBLANES = 8
VMEM_LIMIT_CAP = 56 * 1024 * 1024
HEADS_PER_LANE_GROUP = LANES // R_HEADS

HI = lax.Precision.HIGHEST


def _vmem_limit(block_bytes):
    return int(min(VMEM_LIMIT_CAP, max(16 * 1024 * 1024, 3 * block_bytes)))


def _nbytes(shape, dtype):
    n = 1
    for s in shape:
        n *= s
    return n * jnp.dtype(dtype).itemsize


def _pick_tile(n, candidates):
    for c in candidates:
        if n % c == 0:
            return c
    raise ValueError(f"no tile in {candidates} divides {n}")


def _sigmoid(x):
    return 1.0 / (1.0 + jnp.exp(-x))


def _softplus(x):
    return jnp.maximum(x, 0.0) + jnp.log(1.0 + jnp.exp(-jnp.abs(x)))


def _norm_matmul_kernel(x_ref, g_ref, w_ref, b_ref, o_ref, h_ref):
    @pl.when(pl.program_id(1) == 0)
    def _():
        x = x_ref[...]
        ms = jnp.mean(x * x, axis=-1, keepdims=True)
        h_ref[...] = (x * lax.rsqrt(ms + RMS_EPS) * g_ref[...]).astype(BF16)

    o_ref[...] = jnp.dot(h_ref[...], w_ref[...], preferred_element_type=F32) + b_ref[...]


def _norm_matmul(x, g, w, b, *, tn):
    n, d = x.shape
    c = w.shape[1]
    tm = _pick_tile(n, (1024, 512, 256, 128, 64, 32, 16, 8))
    blocks = _nbytes((tm, d), F32) + _nbytes((d, tn), BF16) + _nbytes((tm, tn), F32)
    return pl.pallas_call(
        _norm_matmul_kernel,
        grid=(n // tm, c // tn),
        in_specs=[
            pl.BlockSpec((tm, d), lambda i, j: (i, 0)),
            pl.BlockSpec((1, d), lambda i, j: (0, 0)),
            pl.BlockSpec((d, tn), lambda i, j: (0, j)),
            pl.BlockSpec((1, tn), lambda i, j: (0, j)),
        ],
        out_specs=pl.BlockSpec((tm, tn), lambda i, j: (i, j)),
        out_shape=jax.ShapeDtypeStruct((n, c), F32),
        scratch_shapes=[pltpu.VMEM((tm, d), BF16)],
        compiler_params=pltpu.CompilerParams(
            dimension_semantics=("arbitrary", "arbitrary"),
            vmem_limit_bytes=_vmem_limit(blocks + _nbytes((tm, d), BF16)),
        ),
    )(x, g, w, b)


def _mlstm_kernel(q_ref, k_ref, v_ref, o_ref, gt_ref, c0_ref, n0_ref, m0_ref, gn_ref,
                  h_ref, c1_ref, n1_ref, m1_ref, c_sc, n_sc, m_sc, *, chunk):
    L = chunk
    head = pl.program_id(1)
    ci = pl.program_id(2)

    @pl.when(ci == 0)
    def _():
        c_sc[...] = c0_ref[0, 0]
        n_sc[...] = n0_ref[0, 0]
        m_sc[...] = m0_ref[0, 0]

    q = q_ref[...] * (M_DK ** -0.5)
    k = k_ref[...]
    v = v_ref[...]
    gates = gt_ref[...]
    lane = lax.broadcasted_iota(jnp.int32, (1, LANES), 1)
    sel_i = lane == head
    sel_f = lane == head + M_HEADS

    row = lax.broadcasted_iota(jnp.int32, (L, L), 0)
    col = lax.broadcasted_iota(jnp.int32, (L, L), 1)
    causal = row >= col
    logf = -_softplus(-gates)
    b_all = jnp.dot(causal.astype(F32), logf, precision=HI, preferred_element_type=F32)
    z = jnp.where(sel_i, gates, 0.0) + jnp.where(sel_f, b_all, 0.0)
    i_col = jnp.sum(jnp.where(sel_i, z, 0.0), axis=-1, keepdims=True)
    b_col = jnp.sum(jnp.where(sel_f, z, 0.0), axis=-1, keepdims=True)
    r8 = lax.broadcasted_iota(jnp.int32, (SUBLANES, LANES), 0)
    l8 = lax.broadcasted_iota(jnp.int32, (SUBLANES, LANES), 1)
    pick = (((r8 == 0) & (l8 == head)) | ((r8 == 1) & (l8 == head + M_HEADS))).astype(F32)
    rows = lax.dot_general(pick, z, (((1,), (1,)), ((), ())), precision=HI,
                           preferred_element_type=F32)
    i_row = rows[0:1, :]
    b_row = rows[1:2, :]

    m_prev = m_sc[...]
    dmat = jnp.where(causal, b_col - b_row + i_row, -jnp.inf)
    m_inter = b_col + m_prev
    m_t = jnp.maximum(m_inter, jnp.max(dmat, axis=-1, keepdims=True))
    s = lax.dot_general(q, k, (((1,), (1,)), ((), ())), preferred_element_type=F32)
    s = s * jnp.exp(dmat - m_t)
    inter = jnp.exp(m_inter - m_t)
    c_prev = c_sc[...]
    n_prev = n_sc[...]
    qc = lax.dot_general(q, c_prev, (((1,), (1,)), ((), ())), preferred_element_type=F32)
    num = jnp.dot(s, v, preferred_element_type=F32) + inter * qc
    den = jnp.sum(s, axis=-1, keepdims=True) + inter * jnp.sum(q * n_prev, axis=-1, keepdims=True)
    hh = num / jnp.maximum(jnp.abs(den), jnp.exp(-m_t))

    m_new = m_t[L - 1:L, :]
    b_last = b_col[L - 1:L, :]
    w_s = jnp.exp(b_last - b_col + i_col - m_new)
    decay = jnp.exp(b_last + m_prev - m_new)
    c_sc[...] = decay * c_prev + lax.dot_general(
        v * w_s, k, (((0,), (0,)), ((), ())), preferred_element_type=F32)
    n_sc[...] = decay * n_prev + jnp.sum(w_s * k, axis=0, keepdims=True)
    m_sc[...] = m_new

    hn = hh * lax.rsqrt(jnp.mean(hh * hh, axis=-1, keepdims=True) + RMS_EPS) * gn_ref[...]
    h_ref[...] = hn * _sigmoid(o_ref[...])

    @pl.when(ci == pl.num_programs(2) - 1)
    def _():
        c1_ref[0, 0] = c_sc[...]
        n1_ref[0, 0] = n_sc[...]
        m1_ref[0, 0] = m_sc[...]


def _mlstm(p_mix, gates, c0, n0, m0, gn, *, row0, batch, seq):
    L = M_CHUNK if seq % M_CHUNK == 0 else seq
    nc = seq // L
    rb0 = row0 // L
    kblk = (M_HEADS * M_DK) // M_DK
    vblk = (2 * M_HEADS * M_DK) // M_DV
    oblk = vblk + M_HEADS

    def rows(b, h, c):
        return rb0 + b * nc + c

    n_tok = batch * seq
    blocks = (2 * _nbytes((L, M_DK), F32) + 3 * _nbytes((L, M_DV), F32) + _nbytes((L, LANES), F32)
              + 2 * _nbytes((M_DV, M_DK), F32))
    return pl.pallas_call(
        functools.partial(_mlstm_kernel, chunk=L),
        grid=(batch, M_HEADS, nc),
        in_specs=[
            pl.BlockSpec((L, M_DK), lambda b, h, c: (rows(b, h, c), h)),
            pl.BlockSpec((L, M_DK), lambda b, h, c: (rows(b, h, c), kblk + h)),
            pl.BlockSpec((L, M_DV), lambda b, h, c: (rows(b, h, c), vblk + h)),
            pl.BlockSpec((L, M_DV), lambda b, h, c: (rows(b, h, c), oblk + h)),
            pl.BlockSpec((L, LANES), lambda b, h, c: (rows(b, h, c), 0)),
            pl.BlockSpec((1, 1, M_DV, M_DK), lambda b, h, c: (b, h, 0, 0)),
            pl.BlockSpec((1, 1, 1, M_DK), lambda b, h, c: (b, h, 0, 0)),
            pl.BlockSpec((1, 1, 1, 1), lambda b, h, c: (b, h, 0, 0)),
            pl.BlockSpec((1, M_DV), lambda b, h, c: (0, h)),
        ],
        out_specs=[
            pl.BlockSpec((L, M_DV), lambda b, h, c: (b * nc + c, h)),
            pl.BlockSpec((1, 1, M_DV, M_DK), lambda b, h, c: (b, h, 0, 0)),
            pl.BlockSpec((1, 1, 1, M_DK), lambda b, h, c: (b, h, 0, 0)),
            pl.BlockSpec((1, 1, 1, 1), lambda b, h, c: (b, h, 0, 0)),
        ],
        out_shape=[
            jax.ShapeDtypeStruct((n_tok, M_HEADS * M_DV), F32),
            jax.ShapeDtypeStruct((batch, M_HEADS, M_DV, M_DK), F32),
            jax.ShapeDtypeStruct((batch, M_HEADS, 1, M_DK), F32),
            jax.ShapeDtypeStruct((batch, M_HEADS, 1, 1), F32),
        ],
        scratch_shapes=[
            pltpu.VMEM((M_DV, M_DK), F32),
            pltpu.VMEM((1, M_DK), F32),
            pltpu.VMEM((1, 1), F32),
        ],
        compiler_params=pltpu.CompilerParams(
            dimension_semantics=("arbitrary", "arbitrary", "arbitrary"),
            vmem_limit_bytes=_vmem_limit(blocks),
        ),
    )(p_mix, p_mix, p_mix, p_mix, gates, c0, n0.reshape(batch, M_HEADS, 1, M_DK),
      m0.reshape(batch, M_HEADS, 1, 1), gn)


def _shift_rows(x, shift, *, seq, tile_has_whole_seqs, head_rows):
    rows = x.shape[0]
    rolled = pltpu.roll(x, shift, 0)
    r = lax.broadcasted_iota(jnp.int32, (rows, 1), 0)
    if tile_has_whole_seqs:
        return jnp.where((r % seq) < shift, head_rows, rolled)
    out = rolled
    for s in range(shift):
        out = jnp.where(r == s, head_rows[s:s + 1, :], out)
    return out


def _rwkv_prep_kernel(p_ref, head_ref, mu_ref, wl_ref, w0_ref, a0_ref,
                      rr_ref, rk_ref, rv_ref, wp_ref, ap_ref, g_ref, carry_sc,
                      *, seq, whole_seqs):
    x = p_ref[...]
    tb = x.shape[0]
    rw = R_HEADS * R_HD
    if whole_seqs:
        prev = _shift_rows(x, 1, seq=seq, tile_has_whole_seqs=True, head_rows=head_ref[...])
    else:
        @pl.when(pl.program_id(1) == 0)
        def _():
            carry_sc[...] = head_ref[0]
        prev = _shift_rows(x, 1, seq=seq, tile_has_whole_seqs=False, head_rows=carry_sc[...])
        carry_sc[...] = x[tb - 1:tb, :]
    xs = x + (prev - x) * mu_ref[...]
    rr_ref[...] = xs[:, 0:rw]
    rk_ref[...] = xs[:, rw:2 * rw]
    rv_ref[...] = xs[:, 2 * rw:3 * rw]
    z = xs[:, 3 * rw:3 * rw + R_LORA]
    lane = lax.broadcasted_iota(jnp.int32, (1, R_LORA), 1)
    act = jnp.where(lane < 64, jnp.tanh(z), jnp.where(lane < 128, z, _sigmoid(z)))
    lo = jnp.dot(act.astype(BF16), wl_ref[...], preferred_element_type=F32)
    wp_ref[...] = w0_ref[...] + lo[:, 0:rw]
    ap_ref[...] = a0_ref[...] + lo[:, rw:2 * rw]
    g_ref[...] = lo[:, 2 * rw:3 * rw]


def _rwkv_prep(p_rw, shift_state, mu, w_lora, w0, a0, *, row0, batch, seq):
    c = p_rw.shape[1]
    rw = R_HEADS * R_HD
    n_tok = batch * seq
    whole = seq <= 64
    if whole:
        tb = _pick_tile(n_tok, (256, 128, 64, 32, 16, 8))
        assert tb % seq == 0
        grid = (n_tok // tb, 1)
        rb0 = row0 // tb
        assert row0 % tb == 0
        tok_map = lambda i, j: (rb0 + i, 0)
        out_map = lambda i, j: (i, 0)
        head = jnp.zeros((batch, seq, c), F32).at[:, 0].set(shift_state).reshape(n_tok, c)
        head_spec = pl.BlockSpec((tb, c), out_map)
    else:
        tb = _pick_tile(seq, (256, 128, 64, 32, 16, 8))
        nt = seq // tb
        grid = (batch, nt)
        rb0 = row0 // tb
        tok_map = lambda b, j: (rb0 + b * nt + j, 0)
        out_map = lambda b, j: (b * nt + j, 0)
        head = shift_state.reshape(batch, 1, c)
        head_spec = pl.BlockSpec((1, 1, c), lambda b, j: (b, 0, 0))
    tok_out = jax.ShapeDtypeStruct((n_tok, rw), F32)
    tok_spec = pl.BlockSpec((tb, rw), out_map)
    blocks = 2 * _nbytes((tb, c), F32) + 6 * _nbytes((tb, rw), F32) + _nbytes(w_lora.shape, BF16)
    outs = pl.pallas_call(
        functools.partial(_rwkv_prep_kernel, seq=seq, whole_seqs=whole),
        grid=grid,
        in_specs=[
            pl.BlockSpec((tb, c), tok_map),
            head_spec,
            pl.BlockSpec((1, c), lambda i, j: (0, 0)),
            pl.BlockSpec(w_lora.shape, lambda i, j: (0, 0)),
            pl.BlockSpec((1, rw), lambda i, j: (0, 0)),
            pl.BlockSpec((1, rw), lambda i, j: (0, 0)),
        ],
        out_specs=[tok_spec] * 6,
        out_shape=[tok_out] * 6,
        scratch_shapes=[pltpu.VMEM((1, c), F32)],
        compiler_params=pltpu.CompilerParams(
            dimension_semantics=("arbitrary", "arbitrary"),
            vmem_limit_bytes=_vmem_limit(blocks),
        ),
    )(p_rw, head, mu, w_lora, w0, a0)
    new_shift = p_rw[row0:row0 + n_tok].reshape(batch, seq, c)[:, seq - 1]
    return (*outs, new_shift)


def _rwkv_scan_kernel(rr_ref, rk_ref, rv_ref, wp_ref, ap_ref, kks_ref, ka_ref, rrk_ref,
                      gng_ref, gnb_ref, s0_ref, y_ref, s1_ref,
                      s_sc, w_sc, kka_sc, km_sc, nkk_sc, wr_sc, coef_sc, *, tb):
    j = pl.program_id(1)
    nk = R_HD
    k8 = nk // SUBLANES

    @pl.when(j == 0)
    def _():
        s_sc[...] = s0_ref[0]

    rr = rr_ref[...]
    rk = rk_ref[...]
    a = _sigmoid(ap_ref[...])
    dec = jnp.exp(-jnp.exp(-_softplus(-wp_ref[...]) - 0.5))
    kk = rk * kks_ref[...][None]
    nrm = jnp.sqrt(jnp.sum(kk * kk, axis=(1, 2), keepdims=True))
    kk = kk / jnp.maximum(nrm, 1e-12)
    kmod = rk * (1.0 + (a - 1.0) * ka_ref[...][None])
    kka = kk * a
    w_sc[...] = dec
    kka_sc[...] = kka
    km_sc[...] = kmod
    nkk_sc[...] = -kk
    wr_sc[...] = dec * rr
    full = (tb, SUBLANES, LANES)
    coef_sc[0] = jnp.broadcast_to(jnp.sum(kka * rr, axis=(1, 2), keepdims=True)[:, 0], full)
    coef_sc[1] = jnp.broadcast_to(jnp.sum(kmod * rr, axis=(1, 2), keepdims=True)[:, 0], full)
    coef_sc[2] = jnp.broadcast_to(
        jnp.sum(rr * kmod * rrk_ref[...][None], axis=(1, 2), keepdims=True)[:, 0], full)

    def row(ref, t, k):
        r = ref[t, k // SUBLANES, pl.ds(k % SUBLANES, 1), :]
        return jnp.broadcast_to(r, (SUBLANES, LANES))[None]

    def matvecs(t):
        acc_sa = jnp.zeros((k8, SUBLANES, LANES), F32)
        acc_y = jnp.zeros((k8, SUBLANES, LANES), F32)
        for k in range(nk):
            sk = s_sc[k]
            acc_sa = acc_sa + sk * row(nkk_sc, t, k)
            acc_y = acc_y + sk * row(wr_sc, t, k)
        return acc_sa, acc_y

    def emit(t, sa, y0):
        val = rv_ref[t]
        y = y0 + sa * coef_sc[0, t][None] + val * coef_sc[1, t][None]
        mean = jnp.sum(y, axis=(0, 1), keepdims=True) * (1.0 / nk)
        d = y - mean
        var = jnp.sum(d * d, axis=(0, 1), keepdims=True) * (1.0 / nk)
        yn = d * lax.rsqrt(var + GN_EPS) * gng_ref[...] + gnb_ref[...]
        y_ref[t] = yn + coef_sc[2, t][None] * val
        return val

    def step(t, carry):
        sa, y0 = carry
        val = emit(t, sa, y0)
        acc_sa = jnp.zeros((k8, SUBLANES, LANES), F32)
        acc_y = jnp.zeros((k8, SUBLANES, LANES), F32)
        for k in range(nk):
            sk = s_sc[k] * row(w_sc, t, k) + sa * row(kka_sc, t, k) + val * row(km_sc, t, k)
            s_sc[k] = sk
            acc_sa = acc_sa + sk * row(nkk_sc, t + 1, k)
            acc_y = acc_y + sk * row(wr_sc, t + 1, k)
        return acc_sa, acc_y

    sa, y0 = lax.fori_loop(0, tb - 1, step, matvecs(0))
    t_last = tb - 1
    val = emit(t_last, sa, y0)
    for k in range(nk):
        s_sc[k] = (s_sc[k] * row(w_sc, t_last, k) + sa * row(kka_sc, t_last, k)
                   + val * row(km_sc, t_last, k))

    @pl.when(j == pl.num_programs(1) - 1)
    def _():
        s1_ref[0] = s_sc[...]


def _to_lanes(x, *, batch, seq):
    g = batch // HEADS_PER_LANE_GROUP
    x = x.reshape(g, HEADS_PER_LANE_GROUP, seq, R_HEADS, R_HD)
    x = x.transpose(0, 2, 4, 1, 3)
    return x.reshape(g * seq, R_HD // SUBLANES, SUBLANES, LANES)


def _from_lanes(y, *, batch, seq):
    g = batch // HEADS_PER_LANE_GROUP
    y = y.reshape(g, seq, R_HD, HEADS_PER_LANE_GROUP, R_HEADS)
    return y.transpose(0, 3, 1, 4, 2).reshape(batch * seq, R_HEADS * R_HD)


def _param_lanes(p):
    p = p.reshape(R_HEADS, R_HD).T
    return jnp.tile(p, (1, HEADS_PER_LANE_GROUP)).reshape(R_HD // SUBLANES, SUBLANES, LANES)


def _rwkv_scan(rr, rk, rv, wp, ap, kks, ka, rrk, gng, gnb, s0, *, batch, seq):
    assert batch % HEADS_PER_LANE_GROUP == 0
    g = batch // HEADS_PER_LANE_GROUP
    tb = _pick_tile(seq, (16, 8))
    nt = seq // tb
    vecs = [_to_lanes(a, batch=batch, seq=seq) for a in (rr, rk, rv, wp, ap)]
    s0l = s0.reshape(g, HEADS_PER_LANE_GROUP, R_HEADS, R_HD, R_HD).transpose(0, 4, 3, 1, 2)
    s0l = s0l.reshape(g, R_HD, R_HD // SUBLANES, SUBLANES, LANES)
    k8 = R_HD // SUBLANES
    vec_spec = pl.BlockSpec((tb, k8, SUBLANES, LANES), lambda gi, j: (gi * nt + j, 0, 0, 0))
    par_spec = pl.BlockSpec((k8, SUBLANES, LANES), lambda gi, j: (0, 0, 0))
    st_spec = pl.BlockSpec((1, R_HD, k8, SUBLANES, LANES), lambda gi, j: (gi, 0, 0, 0, 0))
    vec_bytes = _nbytes((tb, k8, SUBLANES, LANES), F32)
    blocks = 11 * vec_bytes + 3 * _nbytes((R_HD, R_HD, LANES), F32)
    y, s1l = pl.pallas_call(
        functools.partial(_rwkv_scan_kernel, tb=tb),
        grid=(g, nt),
        in_specs=[vec_spec] * 5 + [par_spec] * 5 + [st_spec],
        out_specs=[vec_spec, st_spec],
        out_shape=[
            jax.ShapeDtypeStruct((g * seq, k8, SUBLANES, LANES), F32),
            jax.ShapeDtypeStruct(s0l.shape, F32),
        ],
        scratch_shapes=[pltpu.VMEM((R_HD, k8, SUBLANES, LANES), F32)]
        + [pltpu.VMEM((tb, k8, SUBLANES, LANES), F32)] * 5
        + [pltpu.VMEM((3, tb, SUBLANES, LANES), F32)],
        compiler_params=pltpu.CompilerParams(
            dimension_semantics=("arbitrary", "arbitrary"),
            vmem_limit_bytes=_vmem_limit(blocks),
        ),
    )(*vecs, _param_lanes(kks), _param_lanes(ka), _param_lanes(rrk), _param_lanes(gng),
      _param_lanes(gnb), s0l)
    s1 = s1l.reshape(g, R_HD, R_HD, HEADS_PER_LANE_GROUP, R_HEADS).transpose(0, 3, 4, 2, 1)
    return _from_lanes(y, batch=batch, seq=seq), s1.reshape(batch, R_HEADS, R_HD, R_HD)


def _mix_out_kernel(hm_ref, yr_ref, g_ref, gm_ref, gr_ref, x_ref, wm_ref, wr_ref, wo_ref, o_ref):
    bm = jnp.dot(hm_ref[...].astype(BF16), wm_ref[...], preferred_element_type=F32)
    yr = (yr_ref[...] * g_ref[...]).astype(BF16)
    br = jnp.dot(yr, wr_ref[...], preferred_element_type=F32)
    merged = _sigmoid(gm_ref[...]) * bm + _sigmoid(gr_ref[...]) * br
    o_ref[...] = x_ref[...] + jnp.dot(merged.astype(BF16), wo_ref[...], preferred_element_type=F32)


def _mix_out(hm, yr, g, p_mix, x, wm, wr, wo):
    n, d = x.shape
    tm = _pick_tile(n, (256, 128, 64, 32, 16, 8))
    gm_blk = (2 * M_HEADS * (M_DK + M_DV)) // d
    row = lambda i: (i, 0)
    fixed = lambda i: (0, 0)
    blocks = 7 * _nbytes((tm, d), F32) + 3 * _nbytes((d, d), BF16)
    return pl.pallas_call(
        _mix_out_kernel,
        grid=(n // tm,),
        in_specs=[
            pl.BlockSpec((tm, wm.shape[0]), row),
            pl.BlockSpec((tm, wr.shape[0]), row),
            pl.BlockSpec((tm, wr.shape[0]), row),
            pl.BlockSpec((tm, d), lambda i: (i, gm_blk)),
            pl.BlockSpec((tm, d), lambda i: (i, gm_blk + 1)),
            pl.BlockSpec((tm, d), row),
            pl.BlockSpec(wm.shape, fixed),
            pl.BlockSpec(wr.shape, fixed),
            pl.BlockSpec(wo.shape, fixed),
        ],
        out_specs=pl.BlockSpec((tm, d), row),
        out_shape=jax.ShapeDtypeStruct((n, d), F32),
        compiler_params=pltpu.CompilerParams(
            dimension_semantics=("arbitrary",),
            vmem_limit_bytes=_vmem_limit(blocks),
        ),
    )(hm, yr, g, p_mix, p_mix, x, wm, wr, wo)


def _conv_down_kernel(u_ref, h1_ref, h2_ref, x_ref, cw_ref, cb_ref, wd_ref, o_ref,
                      carry_sc, *, seq, whole_seqs):
    u = u_ref[...]
    tb = u.shape[0]
    dff = u.shape[1] // 2
    if whole_seqs:
        p1 = _shift_rows(u, 1, seq=seq, tile_has_whole_seqs=True, head_rows=h1_ref[...])
        p2 = _shift_rows(u, 2, seq=seq, tile_has_whole_seqs=True, head_rows=h2_ref[...])
    else:
        @pl.when(pl.program_id(1) == 0)
        def _():
            carry_sc[...] = h1_ref[0]
        carry = carry_sc[...]
        p1 = _shift_rows(u, 1, seq=seq, tile_has_whole_seqs=False, head_rows=carry[1:2, :])
        p2 = _shift_rows(u, 2, seq=seq, tile_has_whole_seqs=False, head_rows=carry)
        carry_sc[...] = u[tb - 2:tb, :]
    cw = cw_ref[...]
    conv = cb_ref[...] + p2 * cw[0:1, :] + p1 * cw[1:2, :] + u * cw[2:3, :]
    gate = conv[:, :dff]
    act = gate * _sigmoid(gate) * conv[:, dff:]
    o_ref[...] = x_ref[...] + jnp.dot(act.astype(BF16), wd_ref[...], preferred_element_type=F32)


def _conv_down(u, conv_state, x, cw, cb, wd, *, row0, batch, seq):
    c = u.shape[1]
    d = x.shape[1]
    n_tok = batch * seq
    taps = cw.shape[0]
    assert taps == 3 and seq >= taps - 1
    whole = seq <= 64
    if whole:
        tb = _pick_tile(n_tok, (128, 64, 32, 16, 8))
        assert tb % seq == 0 and row0 % tb == 0
        grid = (n_tok // tb, 1)
        rb0 = row0 // tb
        tok_map = lambda i, j: (rb0 + i, 0)
        out_map = lambda i, j: (i, 0)
        zeros = jnp.zeros((batch, seq, c), F32)
        h1 = zeros.at[:, 0].set(conv_state[:, 1]).reshape(n_tok, c)
        h2 = zeros.at[:, 0].set(conv_state[:, 0]).at[:, 1].set(conv_state[:, 1]).reshape(n_tok, c)
        h_spec = pl.BlockSpec((tb, c), out_map)
    else:
        tb = _pick_tile(seq, (128, 64, 32, 16, 8))
        nt = seq // tb
        grid = (batch, nt)
        rb0 = row0 // tb
        tok_map = lambda b, j: (rb0 + b * nt + j, 0)
        out_map = lambda b, j: (b * nt + j, 0)
        h1 = conv_state
        h2 = conv_state
        h_spec = pl.BlockSpec((1, taps - 1, c), lambda b, j: (b, 0, 0))
    blocks = 5 * _nbytes((tb, c), F32) + 2 * _nbytes((tb, d), F32) + _nbytes(wd.shape, BF16)
    y = pl.pallas_call(
        functools.partial(_conv_down_kernel, seq=seq, whole_seqs=whole),
        grid=grid,
        in_specs=[
            pl.BlockSpec((tb, c), tok_map),
            h_spec,
            h_spec,
            pl.BlockSpec((tb, d), tok_map),
            pl.BlockSpec(cw.shape, lambda i, j: (0, 0)),
            pl.BlockSpec((1, c), lambda i, j: (0, 0)),
            pl.BlockSpec(wd.shape, lambda i, j: (0, 0)),
        ],
        out_specs=pl.BlockSpec((tb, d), out_map),
        out_shape=jax.ShapeDtypeStruct((n_tok, d), F32),
        scratch_shapes=[pltpu.VMEM((taps - 1, c), F32)],
        compiler_params=pltpu.CompilerParams(
            dimension_semantics=("arbitrary", "arbitrary"),
            vmem_limit_bytes=_vmem_limit(blocks),
        ),
    )(u, h1, h2, x, cw, cb, wd)
    st = u[row0:row0 + n_tok].reshape(batch, seq, c)[:, seq - (taps - 1):]
    return y, st


def _final_norm_kernel(x_ref, g_ref, o_ref):
    x = x_ref[...]
    ms = jnp.mean(x * x, axis=-1, keepdims=True)
    o_ref[...] = x * lax.rsqrt(ms + RMS_EPS) * g_ref[...]


def _final_norm(x, g):
    n, d = x.shape
    tm = _pick_tile(n, (1024, 512, 256, 128, 64, 32, 16, 8))
    return pl.pallas_call(
        _final_norm_kernel,
        grid=(n // tm,),
        in_specs=[pl.BlockSpec((tm, d), lambda i: (i, 0)), pl.BlockSpec((1, d), lambda i: (0, 0))],
        out_specs=pl.BlockSpec((tm, d), lambda i: (i, 0)),
        out_shape=jax.ShapeDtypeStruct((n, d), F32),
        compiler_params=pltpu.CompilerParams(
            dimension_semantics=("arbitrary",),
            vmem_limit_bytes=_vmem_limit(2 * _nbytes((tm, d), F32)),
        ),
    )(x, g)


def kernel(x_prompt, x_sample, state_mlstm_C, state_mlstm_n, state_mlstm_m, state_rwkv_S, state_rwkv_shift, state_ffn_conv, norm_mix_g, w_in, b_in, mu_shift, m_norm_g, r_w0, r_w_up, r_a0, r_a_up, r_g_up, r_k_k, r_k_a, r_r_k, r_gn_g, r_gn_b, w_br_m, w_br_r, w_out, norm_ffn_g, w_up, conv_w, conv_b, w_down, norm_final_g):
    bp, tp, d = x_prompt.shape
    bs, ts, _ = x_sample.shape
    depth = w_in.shape[0]
    n_p = bp * tp
    n_s = bs * ts
    rw = R_HEADS * R_HD
    qkvo = 2 * M_HEADS * (M_DK + M_DV)
    n_gate = 2 * M_HEADS
    mix_cols = qkvo + n_gate + 2 * d
    rw_cols = 3 * rw + R_LORA
    dff2 = w_up.shape[2]
    assert w_in.shape[2] == mix_cols + rw_cols

    x = jnp.concatenate([x_prompt.reshape(n_p, d), x_sample.reshape(n_s, d)], axis=0)

    zc = jnp.zeros((bp, M_HEADS, M_DV, M_DK), F32)
    zn = jnp.zeros((bp, M_HEADS, M_DK), F32)
    zm = jnp.zeros((bp, M_HEADS), F32)
    zs = jnp.zeros((bp, R_HEADS, R_HD, R_HD), F32)
    zsh = jnp.zeros((bp, rw_cols), F32)
    zcv = jnp.zeros((bp, conv_w.shape[1] - 1, dff2), F32)

    outs_p = [[] for _ in range(6)]
    outs_s = [[] for _ in range(6)]
    for l in range(depth):
        wl = w_in[l]
        bl = b_in[l]
        w_mix = jnp.concatenate([wl[:, :qkvo], wl[:, qkvo + n_gate:mix_cols]], axis=1).astype(BF16)
        b_mix = jnp.concatenate([bl[:qkvo], bl[qkvo + n_gate:mix_cols]])[None]
        w_gate = jnp.pad(wl[:, qkvo:qkvo + n_gate], ((0, 0), (0, LANES - n_gate))).astype(BF16)
        b_gate = jnp.pad(bl[qkvo:qkvo + n_gate], (0, LANES - n_gate))[None]
        w_rw = wl[:, mix_cols:].astype(BF16)
        b_rw = bl[mix_cols:][None]
        w_lora = jnp.zeros((R_LORA, 3 * rw), F32)
        w_lora = w_lora.at[0:64, 0:rw].set(r_w_up[l]).at[64:128, rw:2 * rw].set(r_a_up[l])
        w_lora = w_lora.at[128:256, 2 * rw:].set(r_g_up[l]).astype(BF16)
        g_mix = norm_mix_g[l][None]

        p_mix = _norm_matmul(x, g_mix, w_mix, b_mix, tn=1024)
        gates = _norm_matmul(x, g_mix, w_gate, b_gate, tn=LANES)
        p_rw = _norm_matmul(x, g_mix, w_rw, b_rw, tn=rw_cols // 2)

        gn = m_norm_g[l].reshape(1, M_HEADS * M_DV)
        hm_p, c_p, nn_p, m_p = _mlstm(p_mix, gates, zc, zn, zm, gn, row0=0, batch=bp, seq=tp)
        hm_s, c_s, nn_s, m_s = _mlstm(p_mix, gates, state_mlstm_C[l], state_mlstm_n[l],
                                      state_mlstm_m[l], gn, row0=n_p, batch=bs, seq=ts)

        prep_args = (mu_shift[l][None], w_lora, r_w0[l][None], r_a0[l][None])
        rr_p, rk_p, rv_p, wp_p, ap_p, g_p, sh_p = _rwkv_prep(
            p_rw, zsh, *prep_args, row0=0, batch=bp, seq=tp)
        rr_s, rk_s, rv_s, wp_s, ap_s, g_s, sh_s = _rwkv_prep(
            p_rw, state_rwkv_shift[l], *prep_args, row0=n_p, batch=bs, seq=ts)
        scan_args = (r_k_k[l], r_k_a[l], r_r_k[l], r_gn_g[l], r_gn_b[l])
        y_p, s_p = _rwkv_scan(rr_p, rk_p, rv_p, wp_p, ap_p, *scan_args, zs, batch=bp, seq=tp)
        y_s, s_s = _rwkv_scan(rr_s, rk_s, rv_s, wp_s, ap_s, *scan_args, state_rwkv_S[l],
                              batch=bs, seq=ts)

        hm = jnp.concatenate([hm_p, hm_s], axis=0)
        yr = jnp.concatenate([y_p, y_s], axis=0)
        gg = jnp.concatenate([g_p, g_s], axis=0)
        x = _mix_out(hm, yr, gg, p_mix, x, w_br_m[l].astype(BF16), w_br_r[l].astype(BF16),
                     w_out[l].astype(BF16))

        u = _norm_matmul(x, norm_ffn_g[l][None], w_up[l].astype(BF16),
                         jnp.zeros((1, dff2), F32), tn=dff2 // 4)
        wd = w_down[l].astype(BF16)
        x_p, cv_p = _conv_down(u, zcv, x, conv_w[l], conv_b[l][None], wd, row0=0, batch=bp, seq=tp)
        x_s, cv_s = _conv_down(u, state_ffn_conv[l], x, conv_w[l], conv_b[l][None], wd,
                               row0=n_p, batch=bs, seq=ts)
        x = jnp.concatenate([x_p, x_s], axis=0)

        for lst, val in zip(outs_p, (c_p, nn_p.reshape(bp, M_HEADS, M_DK), m_p.reshape(bp, M_HEADS),
                                     s_p, sh_p, cv_p)):
            lst.append(val)
        for lst, val in zip(outs_s, (c_s, nn_s.reshape(bs, M_HEADS, M_DK), m_s.reshape(bs, M_HEADS),
                                     s_s, sh_s, cv_s)):
            lst.append(val)

    y = _final_norm(x, norm_final_g[None])
    y_prompt = y[:n_p].reshape(bp, tp, d)
    y_sample = y[n_p:].reshape(bs, ts, d)
    return (y_prompt, y_sample, *[jnp.stack(o) for o in outs_p], *[jnp.stack(o) for o in outs_s])
```

```python
import functools

import jax
import jax.numpy as jnp
from jax import lax
from jax.experimental import pallas as pl
from jax.experimental.pallas import tpu as pltpu

F32 = jnp.float32
BF16 = jnp.bfloat16

M_HEADS = 4
M_DK = 128
M_DV = 256
M_CHUNK = 64
R_HEADS = 16
R_HD = 64
R_LORA = 256
RMS_EPS = 1e-6
GN_EPS = 64e-5

LANES = 128
SUBLANES = 8
VMEM_LIMIT_CAP = 56 * 1024 * 1024
HEADS_PER_LANE_GROUP = LANES // R_HEADS

HI = lax.Precision.HIGHEST
ANY_SPEC = pl.BlockSpec(memory_space=pl.ANY)


def _vmem_limit(block_bytes):
    return int(min(VMEM_LIMIT_CAP, max(16 * 1024 * 1024, 3 * block_bytes)))


def _nbytes(shape, dtype):
    n = 1
    for s in shape:
        n *= s
    return n * jnp.dtype(dtype).itemsize


def _pick_tile(n, candidates):
    for c in candidates:
        if n % c == 0:
            return c
    raise ValueError(f"no tile in {candidates} divides {n}")


def _sigmoid(x):
    return 1.0 / (1.0 + jnp.exp(-x))


def _softplus(x):
    return jnp.maximum(x, 0.0) + jnp.log(1.0 + jnp.exp(-jnp.abs(x)))


def _params(semantics, block_bytes):
    return pltpu.CompilerParams(dimension_semantics=semantics,
                                vmem_limit_bytes=_vmem_limit(block_bytes))


def _norm_matmul_kernel(x_ref, g_ref, w_ref, b_ref, o_ref, h_ref):
    @pl.when(pl.program_id(1) == 0)
    def _():
        x = x_ref[...]
        ms = jnp.mean(x * x, axis=-1, keepdims=True)
        h_ref[...] = (x * lax.rsqrt(ms + RMS_EPS) * g_ref[...]).astype(BF16)

    o_ref[...] = jnp.dot(h_ref[...], w_ref[...], preferred_element_type=F32) + b_ref[...]


def _norm_matmul(x, g, w, b, *, tn, name):
    n, d = x.shape
    c = w.shape[1]
    tm = _pick_tile(n, (1024, 512, 256, 128, 64, 32, 16, 8))
    blocks = (_nbytes((tm, d), F32) + _nbytes((d, tn), BF16) + _nbytes((tm, tn), F32)
              + _nbytes((tm, d), BF16))
    return pl.pallas_call(
        _norm_matmul_kernel,
        name=name,
        grid=(n // tm, c // tn),
        in_specs=[
            pl.BlockSpec((tm, d), lambda i, j: (i, 0)),
            pl.BlockSpec((1, d), lambda i, j: (0, 0)),
            pl.BlockSpec((d, tn), lambda i, j: (0, j)),
            pl.BlockSpec((1, tn), lambda i, j: (0, j)),
        ],
        out_specs=pl.BlockSpec((tm, tn), lambda i, j: (i, j)),
        out_shape=jax.ShapeDtypeStruct((n, c), F32),
        scratch_shapes=[pltpu.VMEM((tm, d), BF16)],
        compiler_params=_params(("arbitrary", "arbitrary"), blocks),
    )(x, g, w, b)


def _mlstm_kernel(*refs, chunk, nseq, aliased):
    (q_ref, k_ref, v_ref, o_ref, gt_ref, c0_ref, n0_ref, m0_ref, gn_ref) = refs[:9]
    refs = refs[10:] if aliased else refs[9:]
    h_ref, c1_ref, n1_ref, m1_ref, c_sc, n_sc, m_sc = refs
    L = chunk
    ci = pl.program_id(1)

    @pl.when(ci == 0)
    def _():
        c_sc[...] = c0_ref[...]
        n_sc[...] = n0_ref[...]
        m_sc[...] = m0_ref[...]

    lane = lax.broadcasted_iota(jnp.int32, (1, LANES), 1)
    row = lax.broadcasted_iota(jnp.int32, (L, L), 0)
    col = lax.broadcasted_iota(jnp.int32, (L, L), 1)
    causal = row >= col
    tri = causal.astype(F32)
    eye8 = (lax.broadcasted_iota(jnp.int32, (SUBLANES, LANES), 0)
            == lax.broadcasted_iota(jnp.int32, (SUBLANES, LANES), 1)).astype(F32)

    for s in range(nseq):
        r0 = s * L
        gates = gt_ref[r0:r0 + L, :]
        b_all = jnp.dot(tri, -_softplus(-gates), precision=HI, preferred_element_type=F32)
        z = jnp.where(lane < M_HEADS, gates, b_all)
        z_rows = lax.dot_general(eye8, z, (((1,), (1,)), ((), ())), precision=HI,
                                 preferred_element_type=F32)
        for h in range(M_HEADS):
            q = q_ref[r0:r0 + L, h * M_DK:(h + 1) * M_DK] * (M_DK ** -0.5)
            k = k_ref[r0:r0 + L, h * M_DK:(h + 1) * M_DK]
            v = v_ref[r0:r0 + L, h * M_DV:(h + 1) * M_DV]
            i_col = z[:, h:h + 1]
            b_col = z[:, M_HEADS + h:M_HEADS + h + 1]
            i_row = z_rows[h:h + 1, :]
            b_row = z_rows[M_HEADS + h:M_HEADS + h + 1, :]
            m_prev = m_sc[s, h]
            c_prev = c_sc[s, h]
            n_prev = n_sc[s, h]

            dmat = jnp.where(causal, b_col - b_row + i_row, -jnp.inf)
            m_inter = b_col + m_prev
            m_t = jnp.maximum(m_inter, jnp.max(dmat, axis=-1, keepdims=True))
            sc = lax.dot_general(q, k, (((1,), (1,)), ((), ())), preferred_element_type=F32)
            sc = sc * jnp.exp(dmat - m_t)
            inter = jnp.exp(m_inter - m_t)
            qc = lax.dot_general(q, c_prev, (((1,), (1,)), ((), ())), preferred_element_type=F32)
            num = jnp.dot(sc, v, preferred_element_type=F32) + inter * qc
            den = (jnp.sum(sc, axis=-1, keepdims=True)
                   + inter * jnp.sum(q * n_prev, axis=-1, keepdims=True))
            hh = num / jnp.maximum(jnp.abs(den), jnp.exp(-m_t))

            m_new = m_t[L - 1:L, :]
            b_last = b_col[L - 1:L, :]
            w_s = jnp.exp(b_last - b_col + i_col - m_new)
            decay = jnp.exp(b_last + m_prev - m_new)
            c_sc[s, h] = decay * c_prev + lax.dot_general(
                v * w_s, k, (((0,), (0,)), ((), ())), preferred_element_type=F32)
            n_sc[s, h] = decay * n_prev + jnp.sum(w_s * k, axis=0, keepdims=True)
            m_sc[s, h] = m_new

            hn = (hh * lax.rsqrt(jnp.mean(hh * hh, axis=-1, keepdims=True) + RMS_EPS)
                  * gn_ref[:, h * M_DV:(h + 1) * M_DV])
            h_ref[r0:r0 + L, h * M_DV:(h + 1) * M_DV] = hn * _sigmoid(
                o_ref[r0:r0 + L, h * M_DV:(h + 1) * M_DV])

    @pl.when(ci == pl.num_programs(1) - 1)
    def _():
        c1_ref[...] = c_sc[...]
        n1_ref[...] = n_sc[...]
        m1_ref[...] = m_sc[...]


def _mlstm(p_mix, gates, c0, n0, m0, gn, hm_buf, *, row0, batch, seq, name):
    n_all = p_mix.shape[0]
    L = M_CHUNK if seq % M_CHUNK == 0 else seq
    nc = seq // L
    nseq = 1 if nc > 1 else _pick_tile(batch, (4, 2, 1))
    rows = nseq * L
    rb0 = row0 // rows
    assert row0 % rows == 0
    qk_w = M_HEADS * M_DK
    v_w = M_HEADS * M_DV
    assert qk_w * 2 == v_w

    def tok(bi, c):
        return rb0 + bi * nc + c

    aliased = hm_buf is not None
    state_map = lambda bi, c: (bi, 0, 0, 0)
    in_specs = [
        pl.BlockSpec((rows, qk_w), lambda bi, c: (tok(bi, c), 0)),
        pl.BlockSpec((rows, qk_w), lambda bi, c: (tok(bi, c), 1)),
        pl.BlockSpec((rows, v_w), lambda bi, c: (tok(bi, c), 1)),
        pl.BlockSpec((rows, v_w), lambda bi, c: (tok(bi, c), 2)),
        pl.BlockSpec((rows, LANES), lambda bi, c: (tok(bi, c), 0)),
        pl.BlockSpec((nseq, M_HEADS, M_DV, M_DK), state_map),
        pl.BlockSpec((nseq, M_HEADS, 1, M_DK), state_map),
        pl.BlockSpec((nseq, M_HEADS, 1, 1), state_map),
        pl.BlockSpec((1, v_w), lambda bi, c: (0, 0)),
    ]
    args = [p_mix, p_mix, p_mix, p_mix, gates, c0, n0.reshape(batch, M_HEADS, 1, M_DK),
            m0.reshape(batch, M_HEADS, 1, 1), gn]
    if aliased:
        in_specs.append(ANY_SPEC)
        args.append(hm_buf)
    blocks = (2 * _nbytes((rows, qk_w), F32) + 3 * _nbytes((rows, v_w), F32)
              + 3 * _nbytes((nseq, M_HEADS, M_DV, M_DK), F32))
    return pl.pallas_call(
        functools.partial(_mlstm_kernel, chunk=L, nseq=nseq, aliased=aliased),
        name=name,
        grid=(batch // nseq, nc),
        in_specs=in_specs,
        out_specs=[
            pl.BlockSpec((rows, v_w), lambda bi, c: (tok(bi, c), 0)),
            pl.BlockSpec((nseq, M_HEADS, M_DV, M_DK), state_map),
            pl.BlockSpec((nseq, M_HEADS, 1, M_DK), state_map),
            pl.BlockSpec((nseq, M_HEADS, 1, 1), state_map),
        ],
        out_shape=[
            jax.ShapeDtypeStruct((n_all, v_w), F32),
            jax.ShapeDtypeStruct((batch, M_HEADS, M_DV, M_DK), F32),
            jax.ShapeDtypeStruct((batch, M_HEADS, 1, M_DK), F32),
            jax.ShapeDtypeStruct((batch, M_HEADS, 1, 1), F32),
        ],
        scratch_shapes=[
            pltpu.VMEM((nseq, M_HEADS, M_DV, M_DK), F32),
            pltpu.VMEM((nseq, M_HEADS, 1, M_DK), F32),
            pltpu.VMEM((nseq, M_HEADS, 1, 1), F32),
        ],
        input_output_aliases={9: 0} if aliased else {},
        compiler_params=_params(("arbitrary", "arbitrary"), blocks),
    )(*args)


def _shift_rows(x, shift, *, seq, tile_has_whole_seqs, head_rows):
    rows = x.shape[0]
    rolled = pltpu.roll(x, shift, 0)
    r = lax.broadcasted_iota(jnp.int32, (rows, 1), 0)
    if tile_has_whole_seqs:
        return jnp.where((r % seq) < shift, head_rows, rolled)
    out = rolled
    for s in range(shift):
        out = jnp.where(r == s, head_rows[s:s + 1, :], out)
    return out


def _last_rows(a, *, row0, batch, seq, count):
    picks = [a[row0 + seq - count + i:row0 + batch * seq:seq] for i in range(count)]
    return jnp.stack(picks, axis=1)


def _rwkv_prep_kernel(p_ref, head_ref, mu_ref, wl_ref, w0_ref, a0_ref,
                      rr_ref, rk_ref, rv_ref, wp_ref, ap_ref, g_ref, carry_sc,
                      *, seq, whole_seqs):
    x = p_ref[...]
    tb = x.shape[0]
    rw = R_HEADS * R_HD
    if whole_seqs:
        prev = _shift_rows(x, 1, seq=seq, tile_has_whole_seqs=True, head_rows=head_ref[...])
    else:
        @pl.when(pl.program_id(1) == 0)
        def _():
            carry_sc[...] = head_ref[0]
        prev = _shift_rows(x, 1, seq=seq, tile_has_whole_seqs=False, head_rows=carry_sc[...])
        carry_sc[...] = x[tb - 1:tb, :]
    xs = x + (prev - x) * mu_ref[...]
    rr_ref[...] = xs[:, 0:rw]
    rk_ref[...] = xs[:, rw:2 * rw]
    rv_ref[...] = xs[:, 2 * rw:3 * rw]
    z = xs[:, 3 * rw:3 * rw + R_LORA]
    lane = lax.broadcasted_iota(jnp.int32, (1, R_LORA), 1)
    act = jnp.where(lane < 64, jnp.tanh(z), jnp.where(lane < 128, z, _sigmoid(z)))
    lo = jnp.dot(act.astype(BF16), wl_ref[...], preferred_element_type=F32)
    wp_ref[...] = jnp.exp(-jnp.exp(-_softplus(-(w0_ref[...] + lo[:, 0:rw])) - 0.5))
    ap_ref[...] = _sigmoid(a0_ref[...] + lo[:, rw:2 * rw])
    g_ref[...] = lo[:, 2 * rw:3 * rw]


def _rwkv_prep(p_rw, shift_state, mu, w_lora, w0, a0, *, row0, batch, seq, name):
    c = p_rw.shape[1]
    rw = R_HEADS * R_HD
    n_tok = batch * seq
    whole = seq <= 64
    if whole:
        tb = _pick_tile(n_tok, (256, 128, 64, 32, 16, 8))
        assert tb % seq == 0 and row0 % tb == 0
        grid = (n_tok // tb, 1)
        rb0 = row0 // tb
        tok_map = lambda i, j: (rb0 + i, 0)
        out_map = lambda i, j: (i, 0)
        head = jnp.zeros((batch, seq, c), F32).at[:, 0].set(shift_state).reshape(n_tok, c)
        head_spec = pl.BlockSpec((tb, c), out_map)
    else:
        tb = _pick_tile(seq, (256, 128, 64, 32, 16, 8))
        nt = seq // tb
        grid = (batch, nt)
        rb0 = row0 // tb
        tok_map = lambda b, j: (rb0 + b * nt + j, 0)
        out_map = lambda b, j: (b * nt + j, 0)
        head = shift_state.reshape(batch, 1, c)
        head_spec = pl.BlockSpec((1, 1, c), lambda b, j: (b, 0, 0))
    tok_out = jax.ShapeDtypeStruct((n_tok, rw), F32)
    tok_spec = pl.BlockSpec((tb, rw), out_map)
    blocks = 2 * _nbytes((tb, c), F32) + 6 * _nbytes((tb, rw), F32) + _nbytes(w_lora.shape, BF16)
    outs = pl.pallas_call(
        functools.partial(_rwkv_prep_kernel, seq=seq, whole_seqs=whole),
        name=name,
        grid=grid,
        in_specs=[
            pl.BlockSpec((tb, c), tok_map),
            head_spec,
            pl.BlockSpec((1, c), lambda i, j: (0, 0)),
            pl.BlockSpec(w_lora.shape, lambda i, j: (0, 0)),
            pl.BlockSpec((1, rw), lambda i, j: (0, 0)),
            pl.BlockSpec((1, rw), lambda i, j: (0, 0)),
        ],
        out_specs=[tok_spec] * 6,
        out_shape=[tok_out] * 6,
        scratch_shapes=[pltpu.VMEM((1, c), F32)],
        compiler_params=_params(("arbitrary", "arbitrary"), blocks),
    )(p_rw, head, mu, w_lora, w0, a0)
    new_shift = _last_rows(p_rw, row0=row0, batch=batch, seq=seq, count=1)[:, 0]
    return (*outs, new_shift)


def _rwkv_scan_kernel(rr_ref, rk_ref, rv_ref, wp_ref, ap_ref, kks_ref, ka_ref, rrk_ref,
                      gng_ref, gnb_ref, s0_ref, y_ref, s1_ref,
                      s_sc, w_sc, kka_sc, km_sc, nkk_sc, wr_sc, coef_sc, *, tb):
    j = pl.program_id(1)
    nk = R_HD

    @pl.when(j == 0)
    def _():
        s_sc[...] = s0_ref[0]

    rr = rr_ref[...]
    rk = rk_ref[...]
    a = ap_ref[...]
    dec = wp_ref[...]
    kk = rk * kks_ref[...][None]
    nrm = jnp.sqrt(jnp.sum(kk * kk, axis=1, keepdims=True))
    kk = kk / jnp.maximum(nrm, 1e-12)
    kmod = rk * (1.0 + (a - 1.0) * ka_ref[...][None])
    kka = kk * a
    w_sc[...] = dec
    kka_sc[...] = kka
    km_sc[...] = kmod
    nkk_sc[...] = -kk
    wr_sc[...] = dec * rr
    full = (tb, SUBLANES, LANES)
    coef_sc[0] = jnp.broadcast_to(jnp.sum(kka * rr, axis=1, keepdims=True), full)
    coef_sc[1] = jnp.broadcast_to(jnp.sum(kmod * rr, axis=1, keepdims=True), full)
    coef_sc[2] = jnp.broadcast_to(
        jnp.sum(rr * kmod * rrk_ref[...][None], axis=1, keepdims=True), full)

    def row(ref, t, k):
        return jnp.broadcast_to(ref[t, pl.ds(k, 1), :], (nk, LANES))

    def coef(i, t):
        return jnp.broadcast_to(coef_sc[i, t, 0:1, :], (nk, LANES))

    def matvecs(t):
        acc_sa = jnp.zeros((nk, LANES), F32)
        acc_y = jnp.zeros((nk, LANES), F32)
        for k in range(nk):
            sk = s_sc[k]
            acc_sa = acc_sa + sk * row(nkk_sc, t, k)
            acc_y = acc_y + sk * row(wr_sc, t, k)
        return acc_sa, acc_y

    def emit(t, sa, y0):
        val = rv_ref[t]
        y = y0 + sa * coef(0, t) + val * coef(1, t)
        mean = jnp.sum(y, axis=0, keepdims=True) * (1.0 / nk)
        d = y - mean
        var = jnp.sum(d * d, axis=0, keepdims=True) * (1.0 / nk)
        yn = d * lax.rsqrt(var + GN_EPS) * gng_ref[...] + gnb_ref[...]
        y_ref[t] = yn + coef(2, t) * val
        return val

    def step(t, carry):
        sa, y0 = carry
        val = emit(t, sa, y0)
        acc_sa = jnp.zeros((nk, LANES), F32)
        acc_y = jnp.zeros((nk, LANES), F32)
        for k in range(nk):
            sk = s_sc[k] * row(w_sc, t, k) + sa * row(kka_sc, t, k) + val * row(km_sc, t, k)
            s_sc[k] = sk
            acc_sa = acc_sa + sk * row(nkk_sc, t + 1, k)
            acc_y = acc_y + sk * row(wr_sc, t + 1, k)
        return acc_sa, acc_y

    sa, y0 = lax.fori_loop(0, tb - 1, step, matvecs(0))
    t_last = tb - 1
    val = emit(t_last, sa, y0)
    for k in range(nk):
        s_sc[k] = (s_sc[k] * row(w_sc, t_last, k) + sa * row(kka_sc, t_last, k)
                   + val * row(km_sc, t_last, k))

    @pl.when(j == pl.num_programs(1) - 1)
    def _():
        s1_ref[0] = s_sc[...]


def _to_lanes(x, *, batch, seq):
    g = batch // HEADS_PER_LANE_GROUP
    x = x.reshape(g, HEADS_PER_LANE_GROUP, seq, R_HEADS, R_HD)
    x = x.transpose(0, 2, 4, 1, 3)
    return x.reshape(g * seq, R_HD, LANES)


def _from_lanes(y, *, batch, seq):
    g = batch // HEADS_PER_LANE_GROUP
    y = y.reshape(g, seq, R_HD, HEADS_PER_LANE_GROUP, R_HEADS)
    return y.transpose(0, 3, 1, 4, 2).reshape(batch * seq, R_HEADS * R_HD)


def _param_lanes(p):
    p = p.reshape(R_HEADS, R_HD).T
    return jnp.tile(p, (1, HEADS_PER_LANE_GROUP))


def _rwkv_scan(rr, rk, rv, wp, ap, kks, ka, rrk, gng, gnb, s0, *, batch, seq, name):
    assert batch % HEADS_PER_LANE_GROUP == 0
    g = batch // HEADS_PER_LANE_GROUP
    tb = _pick_tile(seq, (32, 16, 8))
    nt = seq // tb
    vecs = [_to_lanes(a, batch=batch, seq=seq) for a in (rr, rk, rv, wp, ap)]
    s0l = s0.reshape(g, HEADS_PER_LANE_GROUP, R_HEADS, R_HD, R_HD).transpose(0, 4, 3, 1, 2)
    s0l = s0l.reshape(g, R_HD, R_HD, LANES)
    vec_spec = pl.BlockSpec((tb, R_HD, LANES), lambda gi, j: (gi * nt + j, 0, 0))
    par_spec = pl.BlockSpec((R_HD, LANES), lambda gi, j: (0, 0))
    st_spec = pl.BlockSpec((1, R_HD, R_HD, LANES), lambda gi, j: (gi, 0, 0, 0))
    vec_bytes = _nbytes((tb, R_HD, LANES), F32)
    blocks = 11 * vec_bytes + 3 * _nbytes((R_HD, R_HD, LANES), F32)
    y, s1l = pl.pallas_call(
        functools.partial(_rwkv_scan_kernel, tb=tb),
        name=name,
        grid=(g, nt),
        in_specs=[vec_spec] * 5 + [par_spec] * 5 + [st_spec],
        out_specs=[vec_spec, st_spec],
        out_shape=[
            jax.ShapeDtypeStruct((g * seq, R_HD, LANES), F32),
            jax.ShapeDtypeStruct(s0l.shape, F32),
        ],
        scratch_shapes=[pltpu.VMEM((R_HD, R_HD, LANES), F32)]
        + [pltpu.VMEM((tb, R_HD, LANES), F32)] * 5
        + [pltpu.VMEM((3, tb, SUBLANES, LANES), F32)],
        compiler_params=_params(("arbitrary", "arbitrary"), blocks),
    )(*vecs, _param_lanes(kks), _param_lanes(ka), _param_lanes(rrk), _param_lanes(gng),
      _param_lanes(gnb), s0l)
    s1 = s1l.reshape(g, R_HD, R_HD, HEADS_PER_LANE_GROUP, R_HEADS).transpose(0, 3, 4, 2, 1)
    return _from_lanes(y, batch=batch, seq=seq), s1.reshape(batch, R_HEADS, R_HD, R_HD)


def _mix_out_kernel(hm_ref, yr_ref, g_ref, gm_ref, gr_ref, x_ref, wm_ref, wr_ref, wo_ref, *rest):
    o_ref = rest[-1]
    bm = jnp.dot(hm_ref[...].astype(BF16), wm_ref[...], preferred_element_type=F32)
    yr = (yr_ref[...] * g_ref[...]).astype(BF16)
    br = jnp.dot(yr, wr_ref[...], preferred_element_type=F32)
    merged = _sigmoid(gm_ref[...]) * bm + _sigmoid(gr_ref[...]) * br
    o_ref[...] = x_ref[...] + jnp.dot(merged.astype(BF16), wo_ref[...], preferred_element_type=F32)


def _mix_out(hm, yr, g, p_mix, x, wm, wr, wo, out_buf, *, row0, name):
    n_all, d = x.shape
    n = yr.shape[0]
    tm = _pick_tile(n, (256, 128, 64, 32, 16, 8))
    assert row0 % tm == 0
    rb0 = row0 // tm
    gm_blk = (2 * M_HEADS * (M_DK + M_DV)) // d
    shared = lambda i: (rb0 + i, 0)
    own = lambda i: (i, 0)
    fixed = lambda i: (0, 0)
    aliased = out_buf is not None
    in_specs = [
        pl.BlockSpec((tm, wm.shape[0]), shared),
        pl.BlockSpec((tm, wr.shape[0]), own),
        pl.BlockSpec((tm, wr.shape[0]), own),
        pl.BlockSpec((tm, d), lambda i: (rb0 + i, gm_blk)),
        pl.BlockSpec((tm, d), lambda i: (rb0 + i, gm_blk + 1)),
        pl.BlockSpec((tm, d), shared),
        pl.BlockSpec(wm.shape, fixed),
        pl.BlockSpec(wr.shape, fixed),
        pl.BlockSpec(wo.shape, fixed),
    ]
    args = [hm, yr, g, p_mix, p_mix, x, wm, wr, wo]
    if aliased:
        in_specs.append(ANY_SPEC)
        args.append(out_buf)
    blocks = 7 * _nbytes((tm, d), F32) + 3 * _nbytes((d, d), BF16)
    return pl.pallas_call(
        _mix_out_kernel,
        name=name,
        grid=(n // tm,),
        in_specs=in_specs,
        out_specs=pl.BlockSpec((tm, d), shared),
        out_shape=jax.ShapeDtypeStruct((n_all, d), F32),
        input_output_aliases={9: 0} if aliased else {},
        compiler_params=_params(("arbitrary",), blocks),
    )(*args)


def _conv_down_kernel(u_ref, h1_ref, h2_ref, x_ref, cw_ref, cb_ref, wd_ref, *rest,
                      seq, whole_seqs):
    o_ref, carry_sc = rest[-2:]
    u = u_ref[...]
    tb = u.shape[0]
    dff = u.shape[1] // 2
    if whole_seqs:
        p1 = _shift_rows(u, 1, seq=seq, tile_has_whole_seqs=True, head_rows=h1_ref[...])
        p2 = _shift_rows(u, 2, seq=seq, tile_has_whole_seqs=True, head_rows=h2_ref[...])
    else:
        @pl.when(pl.program_id(1) == 0)
        def _():
            carry_sc[...] = h1_ref[0]
        carry = carry_sc[...]
        p1 = _shift_rows(u, 1, seq=seq, tile_has_whole_seqs=False, head_rows=carry[1:2, :])
        p2 = _shift_rows(u, 2, seq=seq, tile_has_whole_seqs=False, head_rows=carry)
        carry_sc[...] = u[tb - 2:tb, :]
    cw = cw_ref[...]
    conv = cb_ref[...] + p2 * cw[0:1, :] + p1 * cw[1:2, :] + u * cw[2:3, :]
    gate = conv[:, :dff]
    act = gate * _sigmoid(gate) * conv[:, dff:]
    o_ref[...] = x_ref[...] + jnp.dot(act.astype(BF16), wd_ref[...], preferred_element_type=F32)


def _conv_down(u, conv_state, x, cw, cb, wd, out_buf, *, row0, batch, seq, name):
    c = u.shape[1]
    n_all, d = x.shape
    n_tok = batch * seq
    taps = cw.shape[0]
    assert taps == 3 and seq >= taps - 1
    whole = seq <= 64
    if whole:
        tb = _pick_tile(n_tok, (128, 64, 32, 16, 8))
        assert tb % seq == 0 and row0 % tb == 0
        grid = (n_tok // tb, 1)
        rb0 = row0 // tb
        tok_map = lambda i, j: (rb0 + i, 0)
        own_map = lambda i, j: (i, 0)
        zeros = jnp.zeros((batch, seq, c), F32)
        h1 = zeros.at[:, 0].set(conv_state[:, 1]).reshape(n_tok, c)
        h2 = zeros.at[:, 0].set(conv_state[:, 0]).at[:, 1].set(conv_state[:, 1]).reshape(n_tok, c)
        h_spec = pl.BlockSpec((tb, c), own_map)
    else:
        tb = _pick_tile(seq, (128, 64, 32, 16, 8))
        nt = seq // tb
        grid = (batch, nt)
        rb0 = row0 // tb
        tok_map = lambda b, j: (rb0 + b * nt + j, 0)
        h1 = conv_state
        h2 = conv_state
        h_spec = pl.BlockSpec((1, taps - 1, c), lambda b, j: (b, 0, 0))
    aliased = out_buf is not None
    in_specs = [
        pl.BlockSpec((tb, c), tok_map),
        h_spec,
        h_spec,
        pl.BlockSpec((tb, d), tok_map),
        pl.BlockSpec(cw.shape, lambda i, j: (0, 0)),
        pl.BlockSpec((1, c), lambda i, j: (0, 0)),
        pl.BlockSpec(wd.shape, lambda i, j: (0, 0)),
    ]
    args = [u, h1, h2, x, cw, cb, wd]
    if aliased:
        in_specs.append(ANY_SPEC)
        args.append(out_buf)
    blocks = 5 * _nbytes((tb, c), F32) + 2 * _nbytes((tb, d), F32) + _nbytes(wd.shape, BF16)
    y = pl.pallas_call(
        functools.partial(_conv_down_kernel, seq=seq, whole_seqs=whole),
        name=name,
        grid=grid,
        in_specs=in_specs,
        out_specs=pl.BlockSpec((tb, d), tok_map),
        out_shape=jax.ShapeDtypeStruct((n_all, d), F32),
        scratch_shapes=[pltpu.VMEM((taps - 1, c), F32)],
        input_output_aliases={7: 0} if aliased else {},
        compiler_params=_params(("arbitrary", "arbitrary"), blocks),
    )(*args)
    return y, _last_rows(u, row0=row0, batch=batch, seq=seq, count=taps - 1)


def _final_norm_kernel(x_ref, g_ref, o_ref):
    x = x_ref[...]
    ms = jnp.mean(x * x, axis=-1, keepdims=True)
    o_ref[...] = x * lax.rsqrt(ms + RMS_EPS) * g_ref[...]


def _final_norm(x, g, *, row0, n, name):
    d = x.shape[1]
    tm = _pick_tile(n, (1024, 512, 256, 128, 64, 32, 16, 8))
    assert row0 % tm == 0
    rb0 = row0 // tm
    return pl.pallas_call(
        _final_norm_kernel,
        name=name,
        grid=(n // tm,),
        in_specs=[pl.BlockSpec((tm, d), lambda i: (rb0 + i, 0)),
                  pl.BlockSpec((1, d), lambda i: (0, 0))],
        out_specs=pl.BlockSpec((tm, d), lambda i: (i, 0)),
        out_shape=jax.ShapeDtypeStruct((n, d), F32),
        compiler_params=_params(("arbitrary",), 2 * _nbytes((tm, d), F32)),
    )(x, g)


def kernel(x_prompt, x_sample, state_mlstm_C, state_mlstm_n, state_mlstm_m, state_rwkv_S, state_rwkv_shift, state_ffn_conv, norm_mix_g, w_in, b_in, mu_shift, m_norm_g, r_w0, r_w_up, r_a0, r_a_up, r_g_up, r_k_k, r_k_a, r_r_k, r_gn_g, r_gn_b, w_br_m, w_br_r, w_out, norm_ffn_g, w_up, conv_w, conv_b, w_down, norm_final_g):
    bp, tp, d = x_prompt.shape
    bs, ts, _ = x_sample.shape
    depth = w_in.shape[0]
    n_p = bp * tp
    n_s = bs * ts
    rw = R_HEADS * R_HD
    qkvo = 2 * M_HEADS * (M_DK + M_DV)
    n_gate = 2 * M_HEADS
    mix_cols = qkvo + n_gate + 2 * d
    rw_cols = 3 * rw + R_LORA
    dff2 = w_up.shape[2]
    assert w_in.shape[2] == mix_cols + rw_cols

    x = jnp.concatenate([x_prompt.reshape(n_p, d), x_sample.reshape(n_s, d)], axis=0)

    zc = jnp.zeros((bp, M_HEADS, M_DV, M_DK), F32)
    zn = jnp.zeros((bp, M_HEADS, M_DK), F32)
    zm = jnp.zeros((bp, M_HEADS), F32)
    zs = jnp.zeros((bp, R_HEADS, R_HD, R_HD), F32)
    zsh = jnp.zeros((bp, rw_cols), F32)
    zcv = jnp.zeros((bp, conv_w.shape[1] - 1, dff2), F32)

    outs_p = [[] for _ in range(6)]
    outs_s = [[] for _ in range(6)]
    for l in range(depth):
        wl = w_in[l]
        bl = b_in[l]
        w_mix = jnp.concatenate([wl[:, :qkvo], wl[:, qkvo + n_gate:mix_cols]], axis=1).astype(BF16)
        b_mix = jnp.concatenate([bl[:qkvo], bl[qkvo + n_gate:mix_cols]])[None]
        w_gate = jnp.pad(wl[:, qkvo:qkvo + n_gate], ((0, 0), (0, LANES - n_gate))).astype(BF16)
        b_gate = jnp.pad(bl[qkvo:qkvo + n_gate], (0, LANES - n_gate))[None]
        w_rw = wl[:, mix_cols:].astype(BF16)
        b_rw = bl[mix_cols:][None]
        w_lora = jnp.zeros((R_LORA, 3 * rw), F32)
        w_lora = w_lora.at[0:64, 0:rw].set(r_w_up[l]).at[64:128, rw:2 * rw].set(r_a_up[l])
        w_lora = w_lora.at[128:256, 2 * rw:].set(r_g_up[l]).astype(BF16)
        g_mix = norm_mix_g[l][None]

        p_mix = _norm_matmul(x, g_mix, w_mix, b_mix, tn=1024, name="in_proj_mix")
        gates = _norm_matmul(x, g_mix, w_gate, b_gate, tn=LANES, name="in_proj_gates")
        p_rw = _norm_matmul(x, g_mix, w_rw, b_rw, tn=rw_cols // 2, name="in_proj_rwkv")

        gn = m_norm_g[l].reshape(1, M_HEADS * M_DV)
        hm, c_p, nn_p, m_p = _mlstm(p_mix, gates, zc, zn, zm, gn, None,
                                    row0=0, batch=bp, seq=tp, name="mlstm_prompt")
        hm, c_s, nn_s, m_s = _mlstm(p_mix, gates, state_mlstm_C[l], state_mlstm_n[l],
                                    state_mlstm_m[l], gn, hm,
                                    row0=n_p, batch=bs, seq=ts, name="mlstm_sample")

        prep_args = (mu_shift[l][None], w_lora, r_w0[l][None], r_a0[l][None])
        rr_p, rk_p, rv_p, wp_p, ap_p, g_p, sh_p = _rwkv_prep(
            p_rw, zsh, *prep_args, row0=0, batch=bp, seq=tp, name="rwkv_prep_prompt")
        rr_s, rk_s, rv_s, wp_s, ap_s, g_s, sh_s = _rwkv_prep(
            p_rw, state_rwkv_shift[l], *prep_args, row0=n_p, batch=bs, seq=ts,
            name="rwkv_prep_sample")
        scan_args = (r_k_k[l], r_k_a[l], r_r_k[l], r_gn_g[l], r_gn_b[l])
        y_p, s_p = _rwkv_scan(rr_p, rk_p, rv_p, wp_p, ap_p, *scan_args, zs,
                              batch=bp, seq=tp, name="rwkv_scan_prompt")
        y_s, s_s = _rwkv_scan(rr_s, rk_s, rv_s, wp_s, ap_s, *scan_args, state_rwkv_S[l],
                              batch=bs, seq=ts, name="rwkv_scan_sample")

        w_branches = (w_br_m[l].astype(BF16), w_br_r[l].astype(BF16), w_out[l].astype(BF16))
        x_mid = _mix_out(hm, y_p, g_p, p_mix, x, *w_branches, None, row0=0, name="mix_out_prompt")
        x_mid = _mix_out(hm, y_s, g_s, p_mix, x, *w_branches, x_mid, row0=n_p,
                         name="mix_out_sample")

        u = _norm_matmul(x_mid, norm_ffn_g[l][None], w_up[l].astype(BF16),
                         jnp.zeros((1, dff2), F32), tn=dff2 // 4, name="ffn_up")
        wd = w_down[l].astype(BF16)
        conv_args = (conv_w[l], conv_b[l][None], wd)
        x, cv_p = _conv_down(u, zcv, x_mid, *conv_args, None, row0=0, batch=bp, seq=tp,
                             name="conv_down_prompt")
        x, cv_s = _conv_down(u, state_ffn_conv[l], x_mid, *conv_args, x, row0=n_p, batch=bs,
                             seq=ts, name="conv_down_sample")

        for lst, val in zip(outs_p, (c_p, nn_p.reshape(bp, M_HEADS, M_DK), m_p.reshape(bp, M_HEADS),
                                     s_p, sh_p, cv_p)):
            lst.append(val)
        for lst, val in zip(outs_s, (c_s, nn_s.reshape(bs, M_HEADS, M_DK), m_s.reshape(bs, M_HEADS),
                                     s_s, sh_s, cv_s)):
            lst.append(val)

    g_fin = norm_final_g[None]
    y_prompt = _final_norm(x, g_fin, row0=0, n=n_p, name="final_norm_prompt").reshape(bp, tp, d)
    y_sample = _final_norm(x, g_fin, row0=n_p, n=n_s, name="final_norm_sample").reshape(bs, ts, d)
    return (y_prompt, y_sample, *[jnp.stack(o) for o in outs_p], *[jnp.stack(o) for o in outs_s])
```

```python
import functools

import jax
import jax.numpy as jnp
from jax import lax
from jax.experimental import pallas as pl
from jax.experimental.pallas import tpu as pltpu

F32 = jnp.float32
BF16 = jnp.bfloat16

M_HEADS = 4
M_DK = 128
M_DV = 256
M_CHUNK = 64
R_HEADS = 16
R_HD = 64
R_LORA = 256
RMS_EPS = 1e-6
GN_EPS = 64e-5

LANES = 128
SUBLANES = 8
VMEM_LIMIT_CAP = 56 * 1024 * 1024
HEADS_PER_LANE_GROUP = LANES // R_HEADS

HI = lax.Precision.HIGHEST
ANY_SPEC = pl.BlockSpec(memory_space=pl.ANY)


def _vmem_limit(block_bytes):
    return int(min(VMEM_LIMIT_CAP, max(16 * 1024 * 1024, 3 * block_bytes)))


def _nbytes(shape, dtype):
    n = 1
    for s in shape:
        n *= s
    return n * jnp.dtype(dtype).itemsize


def _pick_tile(n, candidates):
    for c in candidates:
        if n % c == 0:
            return c
    raise ValueError(f"no tile in {candidates} divides {n}")


def _sigmoid(x):
    return 1.0 / (1.0 + jnp.exp(-x))


def _softplus(x):
    return jnp.maximum(x, 0.0) + jnp.log(1.0 + jnp.exp(-jnp.abs(x)))


def _params(semantics, block_bytes):
    return pltpu.CompilerParams(dimension_semantics=semantics,
                                vmem_limit_bytes=_vmem_limit(block_bytes))


def _norm_matmul_kernel(x_ref, g_ref, w_ref, b_ref, o_ref, h_ref):
    @pl.when(pl.program_id(1) == 0)
    def _():
        x = x_ref[...]
        ms = jnp.mean(x * x, axis=-1, keepdims=True)
        h_ref[...] = (x * lax.rsqrt(ms + RMS_EPS) * g_ref[...]).astype(BF16)

    o_ref[...] = jnp.dot(h_ref[...], w_ref[...], preferred_element_type=F32) + b_ref[...]


def _norm_matmul(x, g, w, b, *, tn, name):
    n, d = x.shape
    c = w.shape[1]
    tm = _pick_tile(n, (1024, 512, 256, 128, 64, 32, 16, 8))
    blocks = (_nbytes((tm, d), F32) + _nbytes((d, tn), BF16) + _nbytes((tm, tn), F32)
              + _nbytes((tm, d), BF16))
    return pl.pallas_call(
        _norm_matmul_kernel,
        name=name,
        grid=(n // tm, c // tn),
        in_specs=[
            pl.BlockSpec((tm, d), lambda i, j: (i, 0)),
            pl.BlockSpec((1, d), lambda i, j: (0, 0)),
            pl.BlockSpec((d, tn), lambda i, j: (0, j)),
            pl.BlockSpec((1, tn), lambda i, j: (0, j)),
        ],
        out_specs=pl.BlockSpec((tm, tn), lambda i, j: (i, j)),
        out_shape=jax.ShapeDtypeStruct((n, c), F32),
        scratch_shapes=[pltpu.VMEM((tm, d), BF16)],
        compiler_params=_params(("arbitrary", "arbitrary"), blocks),
    )(x, g, w, b)


def _mlstm_kernel(*refs, chunk, nseq, aliased):
    (q_ref, k_ref, v_ref, o_ref, gt_ref, c0_ref, n0_ref, m0_ref, gn_ref) = refs[:9]
    refs = refs[10:] if aliased else refs[9:]
    h_ref, c1_ref, n1_ref, m1_ref, c_sc, n_sc, m_sc = refs
    L = chunk
    ci = pl.program_id(1)

    @pl.when(ci == 0)
    def _():
        c_sc[...] = c0_ref[...]
        n_sc[...] = n0_ref[...]
        m_sc[...] = m0_ref[...]

    lane = lax.broadcasted_iota(jnp.int32, (1, LANES), 1)
    row = lax.broadcasted_iota(jnp.int32, (L, L), 0)
    col = lax.broadcasted_iota(jnp.int32, (L, L), 1)
    causal = row >= col
    tri = causal.astype(F32)
    eye8 = (lax.broadcasted_iota(jnp.int32, (SUBLANES, LANES), 0)
            == lax.broadcasted_iota(jnp.int32, (SUBLANES, LANES), 1)).astype(F32)

    for s in range(nseq):
        r0 = s * L
        gates = gt_ref[r0:r0 + L, :]
        b_all = jnp.dot(tri, -_softplus(-gates), precision=HI, preferred_element_type=F32)
        z = jnp.where(lane < M_HEADS, gates, b_all)
        z_rows = lax.dot_general(eye8, z, (((1,), (1,)), ((), ())), precision=HI,
                                 preferred_element_type=F32)
        for h in range(M_HEADS):
            q = q_ref[r0:r0 + L, h * M_DK:(h + 1) * M_DK] * (M_DK ** -0.5)
            k = k_ref[r0:r0 + L, h * M_DK:(h + 1) * M_DK]
            v = v_ref[r0:r0 + L, h * M_DV:(h + 1) * M_DV]
            i_col = z[:, h:h + 1]
            b_col = z[:, M_HEADS + h:M_HEADS + h + 1]
            i_row = z_rows[h:h + 1, :]
            b_row = z_rows[M_HEADS + h:M_HEADS + h + 1, :]
            m_prev = m_sc[s, h]
            c_prev = c_sc[s, h]
            n_prev = n_sc[s, h]

            dmat = jnp.where(causal, b_col - b_row + i_row, -jnp.inf)
            m_inter = b_col + m_prev
            m_t = jnp.maximum(m_inter, jnp.max(dmat, axis=-1, keepdims=True))
            sc = lax.dot_general(q, k, (((1,), (1,)), ((), ())), preferred_element_type=F32)
            sc = sc * jnp.exp(dmat - m_t)
            inter = jnp.exp(m_inter - m_t)
            qc = lax.dot_general(q, c_prev, (((1,), (1,)), ((), ())), preferred_element_type=F32)
            num = jnp.dot(sc, v, preferred_element_type=F32) + inter * qc
            den = (jnp.sum(sc, axis=-1, keepdims=True)
                   + inter * jnp.sum(q * n_prev, axis=-1, keepdims=True))
            hh = num / jnp.maximum(jnp.abs(den), jnp.exp(-m_t))

            m_new = m_t[L - 1:L, :]
            b_last = b_col[L - 1:L, :]
            w_s = jnp.exp(b_last - b_col + i_col - m_new)
            decay = jnp.exp(b_last + m_prev - m_new)
            c_sc[s, h] = decay * c_prev + lax.dot_general(
                v * w_s, k, (((0,), (0,)), ((), ())), preferred_element_type=F32)
            n_sc[s, h] = decay * n_prev + jnp.sum(w_s * k, axis=0, keepdims=True)
            m_sc[s, h] = m_new

            hn = (hh * lax.rsqrt(jnp.mean(hh * hh, axis=-1, keepdims=True) + RMS_EPS)
                  * gn_ref[:, h * M_DV:(h + 1) * M_DV])
            h_ref[r0:r0 + L, h * M_DV:(h + 1) * M_DV] = hn * _sigmoid(
                o_ref[r0:r0 + L, h * M_DV:(h + 1) * M_DV])

    @pl.when(ci == pl.num_programs(1) - 1)
    def _():
        c1_ref[...] = c_sc[...]
        n1_ref[...] = n_sc[...]
        m1_ref[...] = m_sc[...]


def _mlstm(p_mix, gates, c0, n0, m0, gn, hm_buf, *, row0, batch, seq, name):
    n_all = p_mix.shape[0]
    L = M_CHUNK if seq % M_CHUNK == 0 else seq
    nc = seq // L
    nseq = 1 if nc > 1 else _pick_tile(batch, (4, 2, 1))
    rows = nseq * L
    rb0 = row0 // rows
    assert row0 % rows == 0
    qk_w = M_HEADS * M_DK
    v_w = M_HEADS * M_DV
    assert qk_w * 2 == v_w

    def tok(bi, c):
        return rb0 + bi * nc + c

    aliased = hm_buf is not None
    state_map = lambda bi, c: (bi, 0, 0, 0)
    in_specs = [
        pl.BlockSpec((rows, qk_w), lambda bi, c: (tok(bi, c), 0)),
        pl.BlockSpec((rows, qk_w), lambda bi, c: (tok(bi, c), 1)),
        pl.BlockSpec((rows, v_w), lambda bi, c: (tok(bi, c), 1)),
        pl.BlockSpec((rows, v_w), lambda bi, c: (tok(bi, c), 2)),
        pl.BlockSpec((rows, LANES), lambda bi, c: (tok(bi, c), 0)),
        pl.BlockSpec((nseq, M_HEADS, M_DV, M_DK), state_map),
        pl.BlockSpec((nseq, M_HEADS, 1, M_DK), state_map),
        pl.BlockSpec((nseq, M_HEADS, 1, 1), state_map),
        pl.BlockSpec((1, v_w), lambda bi, c: (0, 0)),
    ]
    args = [p_mix, p_mix, p_mix, p_mix, gates, c0, n0.reshape(batch, M_HEADS, 1, M_DK),
            m0.reshape(batch, M_HEADS, 1, 1), gn]
    if aliased:
        in_specs.append(ANY_SPEC)
        args.append(hm_buf)
    blocks = (2 * _nbytes((rows, qk_w), F32) + 3 * _nbytes((rows, v_w), F32)
              + 3 * _nbytes((nseq, M_HEADS, M_DV, M_DK), F32))
    return pl.pallas_call(
        functools.partial(_mlstm_kernel, chunk=L, nseq=nseq, aliased=aliased),
        name=name,
        grid=(batch // nseq, nc),
        in_specs=in_specs,
        out_specs=[
            pl.BlockSpec((rows, v_w), lambda bi, c: (tok(bi, c), 0)),
            pl.BlockSpec((nseq, M_HEADS, M_DV, M_DK), state_map),
            pl.BlockSpec((nseq, M_HEADS, 1, M_DK), state_map),
            pl.BlockSpec((nseq, M_HEADS, 1, 1), state_map),
        ],
        out_shape=[
            jax.ShapeDtypeStruct((n_all, v_w), F32),
            jax.ShapeDtypeStruct((batch, M_HEADS, M_DV, M_DK), F32),
            jax.ShapeDtypeStruct((batch, M_HEADS, 1, M_DK), F32),
            jax.ShapeDtypeStruct((batch, M_HEADS, 1, 1), F32),
        ],
        scratch_shapes=[
            pltpu.VMEM((nseq, M_HEADS, M_DV, M_DK), F32),
            pltpu.VMEM((nseq, M_HEADS, 1, M_DK), F32),
            pltpu.VMEM((nseq, M_HEADS, 1, 1), F32),
        ],
        input_output_aliases={9: 0} if aliased else {},
        compiler_params=_params(("arbitrary", "arbitrary"), blocks),
    )(*args)


def _shift_rows(x, shift, *, seq, tile_has_whole_seqs, head_rows):
    rows = x.shape[0]
    rolled = pltpu.roll(x, shift, 0)
    r = lax.broadcasted_iota(jnp.int32, (rows, 1), 0)
    if tile_has_whole_seqs:
        return jnp.where((r % seq) < shift, head_rows, rolled)
    out = rolled
    for s in range(shift):
        out = jnp.where(r == s, head_rows[s:s + 1, :], out)
    return out


def _seq_rows(x, offset, *, seq):
    rows = x.shape[0]
    i = lax.broadcasted_iota(jnp.int32, (rows // seq, rows), 0)
    r = lax.broadcasted_iota(jnp.int32, (rows // seq, rows), 1)
    pick = (r == i * seq + offset).astype(F32)
    return jnp.dot(pick, x, precision=HI, preferred_element_type=F32)


def _hk_to_kh(a, axis):
    shape = a.shape
    a = a.reshape(shape[:axis] + (R_HEADS, R_HD) + shape[axis + 1:])
    return jnp.swapaxes(a, axis, axis + 1).reshape(shape)


def _kh_to_hk(a, axis):
    shape = a.shape
    a = a.reshape(shape[:axis] + (R_HD, R_HEADS) + shape[axis + 1:])
    return jnp.swapaxes(a, axis, axis + 1).reshape(shape)


def _sections(a, axis, fn):
    rw = R_HEADS * R_HD
    idx = lambda lo, hi: tuple(slice(lo, hi) if d == axis else slice(None) for d in range(a.ndim))
    parts = [fn(a[idx(i * rw, (i + 1) * rw)], axis) for i in range(3)]
    if a.shape[axis] > 3 * rw:
        parts.append(a[idx(3 * rw, a.shape[axis])])
    return jnp.concatenate(parts, axis=axis)


def _rwkv_prep_kernel(p_ref, head_ref, mu_ref, wl_ref, w0_ref, a0_ref,
                      rr_ref, rk_ref, rv_ref, wp_ref, ap_ref, g_ref, sh_ref, carry_sc,
                      *, seq, whole_seqs):
    x = p_ref[...]
    tb = x.shape[0]
    rw = R_HEADS * R_HD
    if whole_seqs:
        prev = _shift_rows(x, 1, seq=seq, tile_has_whole_seqs=True, head_rows=head_ref[...])
        sh_ref[...] = _seq_rows(x, seq - 1, seq=seq)
    else:
        @pl.when(pl.program_id(1) == 0)
        def _():
            carry_sc[...] = head_ref[0]
        prev = _shift_rows(x, 1, seq=seq, tile_has_whole_seqs=False, head_rows=carry_sc[...])
        carry_sc[...] = x[tb - 1:tb, :]
        sh_ref[0] = x[tb - 1:tb, :]
    xs = x + (prev - x) * mu_ref[...]
    rr_ref[...] = xs[:, 0:rw]
    rk_ref[...] = xs[:, rw:2 * rw]
    rv_ref[...] = xs[:, 2 * rw:3 * rw]
    z = xs[:, 3 * rw:3 * rw + R_LORA]
    lane = lax.broadcasted_iota(jnp.int32, (1, R_LORA), 1)
    act = jnp.where(lane < 64, jnp.tanh(z), jnp.where(lane < 128, z, _sigmoid(z)))
    lo = jnp.dot(act.astype(BF16), wl_ref[...], preferred_element_type=F32)
    wp_ref[...] = jnp.exp(-jnp.exp(-_softplus(-(w0_ref[...] + lo[:, 0:rw])) - 0.5))
    ap_ref[...] = _sigmoid(a0_ref[...] + lo[:, rw:2 * rw])
    g_ref[...] = lo[:, 2 * rw:3 * rw]


def _rwkv_prep(p_rw, shift_state, mu, w_lora, w0, a0, *, row0, batch, seq, name):
    c = p_rw.shape[1]
    rw = R_HEADS * R_HD
    n_tok = batch * seq
    whole = seq <= 64
    if whole:
        tb = _pick_tile(n_tok, (256, 128, 64, 32, 16, 8))
        assert tb % seq == 0 and row0 % tb == 0
        grid = (n_tok // tb, 1)
        rb0 = row0 // tb
        tok_map = lambda i, j: (rb0 + i, 0)
        out_map = lambda i, j: (i, 0)
        head = jnp.zeros((batch, seq, c), F32).at[:, 0].set(shift_state).reshape(n_tok, c)
        head_spec = pl.BlockSpec((tb, c), out_map)
        sh_shape = jax.ShapeDtypeStruct((batch, c), F32)
        sh_spec = pl.BlockSpec((tb // seq, c), out_map)
    else:
        tb = _pick_tile(seq, (256, 128, 64, 32, 16, 8))
        nt = seq // tb
        grid = (batch, nt)
        rb0 = row0 // tb
        tok_map = lambda b, j: (rb0 + b * nt + j, 0)
        out_map = lambda b, j: (b * nt + j, 0)
        head = shift_state.reshape(batch, 1, c)
        head_spec = pl.BlockSpec((1, 1, c), lambda b, j: (b, 0, 0))
        sh_shape = jax.ShapeDtypeStruct((batch, 1, c), F32)
        sh_spec = head_spec
    tok_out = jax.ShapeDtypeStruct((n_tok, rw), F32)
    tok_spec = pl.BlockSpec((tb, rw), out_map)
    blocks = 2 * _nbytes((tb, c), F32) + 6 * _nbytes((tb, rw), F32) + _nbytes(w_lora.shape, BF16)
    outs = pl.pallas_call(
        functools.partial(_rwkv_prep_kernel, seq=seq, whole_seqs=whole),
        name=name,
        grid=grid,
        in_specs=[
            pl.BlockSpec((tb, c), tok_map),
            head_spec,
            pl.BlockSpec((1, c), lambda i, j: (0, 0)),
            pl.BlockSpec(w_lora.shape, lambda i, j: (0, 0)),
            pl.BlockSpec((1, rw), lambda i, j: (0, 0)),
            pl.BlockSpec((1, rw), lambda i, j: (0, 0)),
        ],
        out_specs=[tok_spec] * 6 + [sh_spec],
        out_shape=[tok_out] * 6 + [sh_shape],
        scratch_shapes=[pltpu.VMEM((1, c), F32)],
        compiler_params=_params(("arbitrary", "arbitrary"), blocks),
    )(p_rw, head, mu, w_lora, w0, a0)
    return (*outs[:6], outs[6].reshape(batch, c))


def _rwkv_scan_kernel(rr_ref, rk_ref, dec_ref, a_ref, rv_ref, kks_ref, ka_ref, rrk_ref,
                      gng_ref, gnb_ref, s0_ref, y_ref, s1_ref,
                      s_sc, w_sc, kka_sc, km_sc, nkk_sc, wr_sc, coef_sc, *, tb):
    j = pl.program_id(1)
    nk = R_HD

    @pl.when(j == 0)
    def _():
        s_sc[...] = s0_ref[0]

    rr = rr_ref[...]
    rk = rk_ref[...]
    a = a_ref[...]
    dec = dec_ref[...]
    kk = rk * kks_ref[...]
    nrm = jnp.sqrt(jnp.sum(kk * kk, axis=0, keepdims=True))
    kk = kk / jnp.maximum(nrm, 1e-12)
    kmod = rk * (1.0 + (a - 1.0) * ka_ref[...])
    kka = kk * a
    w_sc[...] = dec
    kka_sc[...] = kka
    km_sc[...] = kmod
    nkk_sc[...] = -kk
    wr_sc[...] = dec * rr
    coef_sc[0] = jnp.sum(kka * rr, axis=0)
    coef_sc[1] = jnp.sum(kmod * rr, axis=0)
    coef_sc[2] = jnp.sum(rr * kmod * rrk_ref[...], axis=0)

    def row(ref, t, k):
        return jnp.broadcast_to(ref[k, pl.ds(t, 1), :], (nk, LANES))

    def coef(i, t):
        return jnp.broadcast_to(coef_sc[i, pl.ds(t, 1), :], (nk, LANES))

    def matvecs(t):
        acc_sa = jnp.zeros((nk, LANES), F32)
        acc_y = jnp.zeros((nk, LANES), F32)
        for k in range(nk):
            sk = s_sc[k]
            acc_sa = acc_sa + sk * row(nkk_sc, t, k)
            acc_y = acc_y + sk * row(wr_sc, t, k)
        return acc_sa, acc_y

    def emit(t, sa, y0):
        val = rv_ref[t]
        y = y0 + sa * coef(0, t) + val * coef(1, t)
        mean = jnp.sum(y, axis=0, keepdims=True) * (1.0 / nk)
        d = y - mean
        var = jnp.sum(d * d, axis=0, keepdims=True) * (1.0 / nk)
        yn = d * lax.rsqrt(var + GN_EPS) * gng_ref[...] + gnb_ref[...]
        y_ref[t] = yn + coef(2, t) * val
        return val

    def step(t, carry):
        sa, y0 = carry
        val = emit(t, sa, y0)
        acc_sa = jnp.zeros((nk, LANES), F32)
        acc_y = jnp.zeros((nk, LANES), F32)
        for k in range(nk):
            sk = s_sc[k] * row(w_sc, t, k) + sa * row(kka_sc, t, k) + val * row(km_sc, t, k)
            s_sc[k] = sk
            acc_sa = acc_sa + sk * row(nkk_sc, t + 1, k)
            acc_y = acc_y + sk * row(wr_sc, t + 1, k)
        return acc_sa, acc_y

    sa, y0 = lax.fori_loop(0, tb - 1, step, matvecs(0))
    t_last = tb - 1
    val = emit(t_last, sa, y0)
    for k in range(nk):
        s_sc[k] = (s_sc[k] * row(w_sc, t_last, k) + sa * row(kka_sc, t_last, k)
                   + val * row(km_sc, t_last, k))

    @pl.when(j == pl.num_programs(1) - 1)
    def _():
        s1_ref[0] = s_sc[...]


LAYOUT_TB = 128


def _lane_slabs(in_ref, p):
    nb = HEADS_PER_LANE_GROUP
    zts = [in_ref[b, :, p * LANES:(p + 1) * LANES].T for b in range(nb)]
    for i in range(SUBLANES):
        d = jnp.concatenate([zt[i * R_HEADS:(i + 1) * R_HEADS, :] for zt in zts], axis=0)
        yield SUBLANES * p + i, d.T


def _lanes_in_kernel(src_ref, dst_ref, *, rows):
    for p in range(R_HD // SUBLANES):
        for i, slab in _lane_slabs(src_ref, p):
            if rows:
                dst_ref[i] = slab
            else:
                dst_ref[:, pl.ds(i, 1), :] = slab[:, None, :]


def _lanes_in(x, *, batch, seq, rows):
    nb = HEADS_PER_LANE_GROUP
    g = batch // nb
    w = R_HEADS * R_HD
    if seq % LAYOUT_TB != 0:
        x = x.reshape(g, nb, seq, R_HD, R_HEADS)
        if rows:
            return x.transpose(3, 0, 2, 1, 4).reshape(R_HD, g * seq, LANES)
        return x.transpose(0, 2, 3, 1, 4).reshape(g * seq, R_HD, LANES)
    nt = seq // LAYOUT_TB
    if rows:
        out_spec = pl.BlockSpec((R_HD, LAYOUT_TB, LANES), lambda gi, j: (0, gi * nt + j, 0))
        out_shape = jax.ShapeDtypeStruct((R_HD, g * seq, LANES), F32)
    else:
        out_spec = pl.BlockSpec((LAYOUT_TB, R_HD, LANES), lambda gi, j: (gi * nt + j, 0, 0))
        out_shape = jax.ShapeDtypeStruct((g * seq, R_HD, LANES), F32)
    return pl.pallas_call(
        functools.partial(_lanes_in_kernel, rows=rows),
        name="lanes_in_rows" if rows else "lanes_in_full",
        grid=(g, nt),
        in_specs=[pl.BlockSpec((nb, LAYOUT_TB, w), lambda gi, j: (gi, j, 0))],
        out_specs=out_spec,
        out_shape=out_shape,
        compiler_params=_params(("arbitrary", "arbitrary"), 2 * _nbytes((nb, LAYOUT_TB, w), F32)),
    )(x.reshape(batch, seq, w))


def _lanes_out_kernel(y_ref, o_ref):
    nb = HEADS_PER_LANE_GROUP
    for p in range(R_HD // SUBLANES):
        pieces = [[] for _ in range(nb)]
        for i in range(SUBLANES):
            st = y_ref[:, SUBLANES * p + i, :].T
            for b in range(nb):
                pieces[b].append(st[b * R_HEADS:(b + 1) * R_HEADS, :])
        for b in range(nb):
            zt = jnp.concatenate(pieces[b], axis=0)
            o_ref[b, :, p * LANES:(p + 1) * LANES] = zt.T


def _lanes_out(y, *, batch, seq):
    nb = HEADS_PER_LANE_GROUP
    g = batch // nb
    w = R_HEADS * R_HD
    if seq % LAYOUT_TB != 0:
        y = y.reshape(g, seq, R_HD, nb, R_HEADS)
        return y.transpose(0, 3, 1, 2, 4).reshape(batch * seq, w)
    nt = seq // LAYOUT_TB
    out = pl.pallas_call(
        _lanes_out_kernel,
        name="lanes_out",
        grid=(g, nt),
        in_specs=[pl.BlockSpec((LAYOUT_TB, R_HD, LANES), lambda gi, j: (gi * nt + j, 0, 0))],
        out_specs=pl.BlockSpec((nb, LAYOUT_TB, w), lambda gi, j: (gi, j, 0)),
        out_shape=jax.ShapeDtypeStruct((batch, seq, w), F32),
        compiler_params=_params(("arbitrary", "arbitrary"), 2 * _nbytes((nb, LAYOUT_TB, w), F32)),
    )(y)
    return out.reshape(batch * seq, w)


def _param_lanes(p):
    p = p.reshape(R_HEADS, R_HD).T
    return jnp.tile(p, (1, HEADS_PER_LANE_GROUP))


def _rwkv_scan(rr, rk, rv, dec, a, kks, ka, rrk, gng, gnb, s0, *, batch, seq, name):
    assert batch % HEADS_PER_LANE_GROUP == 0
    g = batch // HEADS_PER_LANE_GROUP
    tb = _pick_tile(seq, (32, 16, 8))
    nt = seq // tb
    rows = [_lanes_in(v, batch=batch, seq=seq, rows=True) for v in (rr, rk, dec, a)]
    val = _lanes_in(rv, batch=batch, seq=seq, rows=False)
    s0l = s0.reshape(g, HEADS_PER_LANE_GROUP, R_HEADS, R_HD, R_HD).transpose(0, 4, 3, 1, 2)
    s0l = s0l.reshape(g, R_HD, R_HD, LANES)
    row_spec = pl.BlockSpec((R_HD, tb, LANES), lambda gi, j: (0, gi * nt + j, 0))
    val_spec = pl.BlockSpec((tb, R_HD, LANES), lambda gi, j: (gi * nt + j, 0, 0))
    kpar_spec = pl.BlockSpec((R_HD, 1, LANES), lambda gi, j: (0, 0, 0))
    vpar_spec = pl.BlockSpec((R_HD, LANES), lambda gi, j: (0, 0))
    st_spec = pl.BlockSpec((1, R_HD, R_HD, LANES), lambda gi, j: (gi, 0, 0, 0))
    vec_bytes = _nbytes((tb, R_HD, LANES), F32)
    blocks = 11 * vec_bytes + 3 * _nbytes((R_HD, R_HD, LANES), F32)
    kpar = lambda p: _param_lanes(p)[:, None, :]
    y, s1l = pl.pallas_call(
        functools.partial(_rwkv_scan_kernel, tb=tb),
        name=name,
        grid=(g, nt),
        in_specs=[row_spec] * 4 + [val_spec] + [kpar_spec] * 3 + [vpar_spec] * 2 + [st_spec],
        out_specs=[val_spec, st_spec],
        out_shape=[
            jax.ShapeDtypeStruct((g * seq, R_HD, LANES), F32),
            jax.ShapeDtypeStruct(s0l.shape, F32),
        ],
        scratch_shapes=[pltpu.VMEM((R_HD, R_HD, LANES), F32)]
        + [pltpu.VMEM((R_HD, tb, LANES), F32)] * 5
        + [pltpu.VMEM((3, tb, LANES), F32)],
        compiler_params=_params(("arbitrary", "arbitrary"), blocks),
    )(*rows, val, kpar(kks), kpar(ka), kpar(rrk), _param_lanes(gng), _param_lanes(gnb), s0l)
    s1 = s1l.reshape(g, R_HD, R_HD, HEADS_PER_LANE_GROUP, R_HEADS).transpose(0, 3, 4, 2, 1)
    return _lanes_out(y, batch=batch, seq=seq), s1.reshape(batch, R_HEADS, R_HD, R_HD)


def _mix_out_kernel(hm_ref, yr_ref, g_ref, gm_ref, gr_ref, x_ref, wm_ref, wr_ref, wo_ref, *rest):
    o_ref = rest[-1]
    bm = jnp.dot(hm_ref[...].astype(BF16), wm_ref[...], preferred_element_type=F32)
    yr = (yr_ref[...] * g_ref[...]).astype(BF16)
    br = jnp.dot(yr, wr_ref[...], preferred_element_type=F32)
    merged = _sigmoid(gm_ref[...]) * bm + _sigmoid(gr_ref[...]) * br
    o_ref[...] = x_ref[...] + jnp.dot(merged.astype(BF16), wo_ref[...], preferred_element_type=F32)


def _mix_out(hm, yr, g, p_mix, x, wm, wr, wo, out_buf, *, row0, name):
    n_all, d = x.shape
    n = yr.shape[0]
    tm = _pick_tile(n, (256, 128, 64, 32, 16, 8))
    assert row0 % tm == 0
    rb0 = row0 // tm
    gm_blk = (2 * M_HEADS * (M_DK + M_DV)) // d
    shared = lambda i: (rb0 + i, 0)
    own = lambda i: (i, 0)
    fixed = lambda i: (0, 0)
    aliased = out_buf is not None
    in_specs = [
        pl.BlockSpec((tm, wm.shape[0]), shared),
        pl.BlockSpec((tm, wr.shape[0]), own),
        pl.BlockSpec((tm, wr.shape[0]), own),
        pl.BlockSpec((tm, d), lambda i: (rb0 + i, gm_blk)),
        pl.BlockSpec((tm, d), lambda i: (rb0 + i, gm_blk + 1)),
        pl.BlockSpec((tm, d), shared),
        pl.BlockSpec(wm.shape, fixed),
        pl.BlockSpec(wr.shape, fixed),
        pl.BlockSpec(wo.shape, fixed),
    ]
    args = [hm, yr, g, p_mix, p_mix, x, wm, wr, wo]
    if aliased:
        in_specs.append(ANY_SPEC)
        args.append(out_buf)
    blocks = 7 * _nbytes((tm, d), F32) + 3 * _nbytes((d, d), BF16)
    return pl.pallas_call(
        _mix_out_kernel,
        name=name,
        grid=(n // tm,),
        in_specs=in_specs,
        out_specs=pl.BlockSpec((tm, d), shared),
        out_shape=jax.ShapeDtypeStruct((n_all, d), F32),
        input_output_aliases={9: 0} if aliased else {},
        compiler_params=_params(("arbitrary",), blocks),
    )(*args)


def _conv_down_kernel(u_ref, h1_ref, h2_ref, x_ref, cw_ref, cb_ref, wd_ref, *rest,
                      seq, whole_seqs):
    o_ref, st_ref, carry_sc = rest[-3:]
    u = u_ref[...]
    tb = u.shape[0]
    dff = u.shape[1] // 2
    if whole_seqs:
        p1 = _shift_rows(u, 1, seq=seq, tile_has_whole_seqs=True, head_rows=h1_ref[...])
        p2 = _shift_rows(u, 2, seq=seq, tile_has_whole_seqs=True, head_rows=h2_ref[...])
        st_ref[0] = _seq_rows(u, seq - 2, seq=seq)
        st_ref[1] = _seq_rows(u, seq - 1, seq=seq)
    else:
        @pl.when(pl.program_id(1) == 0)
        def _():
            carry_sc[...] = h1_ref[0]
        carry = carry_sc[...]
        p1 = _shift_rows(u, 1, seq=seq, tile_has_whole_seqs=False, head_rows=carry[1:2, :])
        p2 = _shift_rows(u, 2, seq=seq, tile_has_whole_seqs=False, head_rows=carry)
        carry_sc[...] = u[tb - 2:tb, :]
        st_ref[0] = u[tb - 2:tb, :]
    cw = cw_ref[...]
    conv = cb_ref[...] + p2 * cw[0:1, :] + p1 * cw[1:2, :] + u * cw[2:3, :]
    gate = conv[:, :dff]
    act = gate * _sigmoid(gate) * conv[:, dff:]
    o_ref[...] = x_ref[...] + jnp.dot(act.astype(BF16), wd_ref[...], preferred_element_type=F32)


def _conv_down(u, conv_state, x, cw, cb, wd, out_buf, *, row0, batch, seq, name):
    c = u.shape[1]
    n_all, d = x.shape
    n_tok = batch * seq
    taps = cw.shape[0]
    assert taps == 3 and seq >= taps - 1
    whole = seq <= 64
    if whole:
        tb = _pick_tile(n_tok, (128, 64, 32, 16, 8))
        assert tb % seq == 0 and row0 % tb == 0
        grid = (n_tok // tb, 1)
        rb0 = row0 // tb
        tok_map = lambda i, j: (rb0 + i, 0)
        own_map = lambda i, j: (i, 0)
        zeros = jnp.zeros((batch, seq, c), F32)
        h1 = zeros.at[:, 0].set(conv_state[:, 1]).reshape(n_tok, c)
        h2 = zeros.at[:, 0].set(conv_state[:, 0]).at[:, 1].set(conv_state[:, 1]).reshape(n_tok, c)
        h_spec = pl.BlockSpec((tb, c), own_map)
        st_shape = jax.ShapeDtypeStruct((taps - 1, batch, c), F32)
        st_spec = pl.BlockSpec((taps - 1, tb // seq, c), lambda i, j: (0, i, 0))
    else:
        tb = _pick_tile(seq, (128, 64, 32, 16, 8))
        nt = seq // tb
        grid = (batch, nt)
        rb0 = row0 // tb
        tok_map = lambda b, j: (rb0 + b * nt + j, 0)
        h1 = conv_state
        h2 = conv_state
        h_spec = pl.BlockSpec((1, taps - 1, c), lambda b, j: (b, 0, 0))
        st_shape = jax.ShapeDtypeStruct((batch, taps - 1, c), F32)
        st_spec = h_spec
    aliased = out_buf is not None
    in_specs = [
        pl.BlockSpec((tb, c), tok_map),
        h_spec,
        h_spec,
        pl.BlockSpec((tb, d), tok_map),
        pl.BlockSpec(cw.shape, lambda i, j: (0, 0)),
        pl.BlockSpec((1, c), lambda i, j: (0, 0)),
        pl.BlockSpec(wd.shape, lambda i, j: (0, 0)),
    ]
    args = [u, h1, h2, x, cw, cb, wd]
    if aliased:
        in_specs.append(ANY_SPEC)
        args.append(out_buf)
    blocks = 5 * _nbytes((tb, c), F32) + 2 * _nbytes((tb, d), F32) + _nbytes(wd.shape, BF16)
    y, st = pl.pallas_call(
        functools.partial(_conv_down_kernel, seq=seq, whole_seqs=whole),
        name=name,
        grid=grid,
        in_specs=in_specs,
        out_specs=[pl.BlockSpec((tb, d), tok_map), st_spec],
        out_shape=[jax.ShapeDtypeStruct((n_all, d), F32), st_shape],
        scratch_shapes=[pltpu.VMEM((taps - 1, c), F32)],
        input_output_aliases={7: 0} if aliased else {},
        compiler_params=_params(("arbitrary", "arbitrary"), blocks),
    )(*args)
    return y, (jnp.swapaxes(st, 0, 1) if whole else st)


def _final_norm_kernel(x_ref, g_ref, o_ref):
    x = x_ref[...]
    ms = jnp.mean(x * x, axis=-1, keepdims=True)
    o_ref[...] = x * lax.rsqrt(ms + RMS_EPS) * g_ref[...]


def _final_norm(x, g, *, row0, n, name):
    d = x.shape[1]
    tm = _pick_tile(n, (1024, 512, 256, 128, 64, 32, 16, 8))
    assert row0 % tm == 0
    rb0 = row0 // tm
    return pl.pallas_call(
        _final_norm_kernel,
        name=name,
        grid=(n // tm,),
        in_specs=[pl.BlockSpec((tm, d), lambda i: (rb0 + i, 0)),
                  pl.BlockSpec((1, d), lambda i: (0, 0))],
        out_specs=pl.BlockSpec((tm, d), lambda i: (i, 0)),
        out_shape=jax.ShapeDtypeStruct((n, d), F32),
        compiler_params=_params(("arbitrary",), 2 * _nbytes((tm, d), F32)),
    )(x, g)


def kernel(x_prompt, x_sample, state_mlstm_C, state_mlstm_n, state_mlstm_m, state_rwkv_S, state_rwkv_shift, state_ffn_conv, norm_mix_g, w_in, b_in, mu_shift, m_norm_g, r_w0, r_w_up, r_a0, r_a_up, r_g_up, r_k_k, r_k_a, r_r_k, r_gn_g, r_gn_b, w_br_m, w_br_r, w_out, norm_ffn_g, w_up, conv_w, conv_b, w_down, norm_final_g):
    bp, tp, d = x_prompt.shape
    bs, ts, _ = x_sample.shape
    depth = w_in.shape[0]
    n_p = bp * tp
    n_s = bs * ts
    rw = R_HEADS * R_HD
    qkvo = 2 * M_HEADS * (M_DK + M_DV)
    n_gate = 2 * M_HEADS
    mix_cols = qkvo + n_gate + 2 * d
    rw_cols = 3 * rw + R_LORA
    dff2 = w_up.shape[2]
    assert w_in.shape[2] == mix_cols + rw_cols

    x = jnp.concatenate([x_prompt.reshape(n_p, d), x_sample.reshape(n_s, d)], axis=0)

    zc = jnp.zeros((bp, M_HEADS, M_DV, M_DK), F32)
    zn = jnp.zeros((bp, M_HEADS, M_DK), F32)
    zm = jnp.zeros((bp, M_HEADS), F32)
    zs = jnp.zeros((bp, R_HEADS, R_HD, R_HD), F32)
    zsh = jnp.zeros((bp, rw_cols), F32)
    zcv = jnp.zeros((bp, conv_w.shape[1] - 1, dff2), F32)

    outs_p = [[] for _ in range(6)]
    outs_s = [[] for _ in range(6)]
    for l in range(depth):
        wl = w_in[l]
        bl = b_in[l]
        w_mix = jnp.concatenate([wl[:, :qkvo], wl[:, qkvo + n_gate:mix_cols]], axis=1).astype(BF16)
        b_mix = jnp.concatenate([bl[:qkvo], bl[qkvo + n_gate:mix_cols]])[None]
        w_gate = jnp.pad(wl[:, qkvo:qkvo + n_gate], ((0, 0), (0, LANES - n_gate))).astype(BF16)
        b_gate = jnp.pad(bl[qkvo:qkvo + n_gate], (0, LANES - n_gate))[None]
        w_rw = _sections(wl[:, mix_cols:], 1, _hk_to_kh).astype(BF16)
        b_rw = _sections(bl[mix_cols:], 0, _hk_to_kh)[None]
        mu_rw = _sections(mu_shift[l], 0, _hk_to_kh)[None]
        w_lora = jnp.zeros((R_LORA, 3 * rw), F32)
        w_lora = w_lora.at[0:64, 0:rw].set(r_w_up[l]).at[64:128, rw:2 * rw].set(r_a_up[l])
        w_lora = _sections(w_lora.at[128:256, 2 * rw:].set(r_g_up[l]), 1, _hk_to_kh).astype(BF16)
        w_br_r_kh = _hk_to_kh(w_br_r[l], 0).astype(BF16)
        g_mix = norm_mix_g[l][None]

        p_mix = _norm_matmul(x, g_mix, w_mix, b_mix, tn=1024, name="in_proj_mix")
        gates = _norm_matmul(x, g_mix, w_gate, b_gate, tn=LANES, name="in_proj_gates")
        p_rw = _norm_matmul(x, g_mix, w_rw, b_rw, tn=rw_cols // 2, name="in_proj_rwkv")

        gn = m_norm_g[l].reshape(1, M_HEADS * M_DV)
        hm, c_p, nn_p, m_p = _mlstm(p_mix, gates, zc, zn, zm, gn, None,
                                    row0=0, batch=bp, seq=tp, name="mlstm_prompt")
        hm, c_s, nn_s, m_s = _mlstm(p_mix, gates, state_mlstm_C[l], state_mlstm_n[l],
                                    state_mlstm_m[l], gn, hm,
                                    row0=n_p, batch=bs, seq=ts, name="mlstm_sample")

        prep_args = (mu_rw, w_lora, _hk_to_kh(r_w0[l], 0)[None], _hk_to_kh(r_a0[l], 0)[None])
        rr_p, rk_p, rv_p, wp_p, ap_p, g_p, sh_p = _rwkv_prep(
            p_rw, zsh, *prep_args, row0=0, batch=bp, seq=tp, name="rwkv_prep_prompt")
        rr_s, rk_s, rv_s, wp_s, ap_s, g_s, sh_s = _rwkv_prep(
            p_rw, _sections(state_rwkv_shift[l], 1, _hk_to_kh), *prep_args, row0=n_p, batch=bs,
            seq=ts, name="rwkv_prep_sample")
        sh_p = _sections(sh_p, 1, _kh_to_hk)
        sh_s = _sections(sh_s, 1, _kh_to_hk)
        scan_args = (r_k_k[l], r_k_a[l], r_r_k[l], r_gn_g[l], r_gn_b[l])
        y_p, s_p = _rwkv_scan(rr_p, rk_p, rv_p, wp_p, ap_p, *scan_args, zs,
                              batch=bp, seq=tp, name="rwkv_scan_prompt")
        y_s, s_s = _rwkv_scan(rr_s, rk_s, rv_s, wp_s, ap_s, *scan_args, state_rwkv_S[l],
                              batch=bs, seq=ts, name="rwkv_scan_sample")

        w_branches = (w_br_m[l].astype(BF16), w_br_r_kh, w_out[l].astype(BF16))
        x_mid = _mix_out(hm, y_p, g_p, p_mix, x, *w_branches, None, row0=0, name="mix_out_prompt")
        x_mid = _mix_out(hm, y_s, g_s, p_mix, x, *w_branches, x_mid, row0=n_p,
                         name="mix_out_sample")

        u = _norm_matmul(x_mid, norm_ffn_g[l][None], w_up[l].astype(BF16),
                         jnp.zeros((1, dff2), F32), tn=dff2 // 4, name="ffn_up")
        wd = w_down[l].astype(BF16)
        conv_args = (conv_w[l], conv_b[l][None], wd)
        x, cv_p = _conv_down(u, zcv, x_mid, *conv_args, None, row0=0, batch=bp, seq=tp,
                             name="conv_down_prompt")
        x, cv_s = _conv_down(u, state_ffn_conv[l], x_mid, *conv_args, x, row0=n_p, batch=bs,
                             seq=ts, name="conv_down_sample")

        for lst, val in zip(outs_p, (c_p, nn_p.reshape(bp, M_HEADS, M_DK), m_p.reshape(bp, M_HEADS),
                                     s_p, sh_p, cv_p)):
            lst.append(val)
        for lst, val in zip(outs_s, (c_s, nn_s.reshape(bs, M_HEADS, M_DK), m_s.reshape(bs, M_HEADS),
                                     s_s, sh_s, cv_s)):
            lst.append(val)

    g_fin = norm_final_g[None]
    y_prompt = _final_norm(x, g_fin, row0=0, n=n_p, name="final_norm_prompt").reshape(bp, tp, d)
    y_sample = _final_norm(x, g_fin, row0=n_p, n=n_s, name="final_norm_sample").reshape(bs, ts, d)
    return (y_prompt, y_sample, *[jnp.stack(o) for o in outs_p], *[jnp.stack(o) for o in outs_s])
```

```python
import functools

import jax
import jax.numpy as jnp
from jax import lax
from jax.experimental import pallas as pl
from jax.experimental.pallas import tpu as pltpu

F32 = jnp.float32
BF16 = jnp.bfloat16

M_HEADS = 4
M_DK = 128
M_DV = 256
M_CHUNK = 64
R_HEADS = 16
R_HD = 64
R_LORA = 256
RMS_EPS = 1e-6
GN_EPS = 64e-5

LANES = 128
SUBLANES = 8
VMEM_LIMIT_CAP = 56 * 1024 * 1024
HEADS_PER_LANE_GROUP = LANES // R_HEADS

HI = lax.Precision.HIGHEST
ANY_SPEC = pl.BlockSpec(memory_space=pl.ANY)


def _vmem_limit(block_bytes):
    return int(min(VMEM_LIMIT_CAP, max(16 * 1024 * 1024, 3 * block_bytes)))


def _nbytes(shape, dtype):
    n = 1
    for s in shape:
        n *= s
    return n * jnp.dtype(dtype).itemsize


def _pick_tile(n, candidates):
    for c in candidates:
        if n % c == 0:
            return c
    raise ValueError(f"no tile in {candidates} divides {n}")


def _sigmoid(x):
    return 1.0 / (1.0 + jnp.exp(-x))


def _softplus(x):
    return jnp.maximum(x, 0.0) + jnp.log(1.0 + jnp.exp(-jnp.abs(x)))


def _params(semantics, block_bytes):
    return pltpu.CompilerParams(dimension_semantics=semantics,
                                vmem_limit_bytes=_vmem_limit(block_bytes))


def _norm_matmul_kernel(x_ref, g_ref, w_ref, b_ref, o_ref, h_ref):
    @pl.when(pl.program_id(1) == 0)
    def _():
        x = x_ref[...]
        ms = jnp.mean(x * x, axis=-1, keepdims=True)
        h_ref[...] = (x * lax.rsqrt(ms + RMS_EPS) * g_ref[...]).astype(BF16)

    o_ref[...] = jnp.dot(h_ref[...], w_ref[...], preferred_element_type=F32) + b_ref[...]


def _norm_matmul(x, g, w, b, *, tn, name):
    n, d = x.shape
    c = w.shape[1]
    tm = _pick_tile(n, (1024, 512, 256, 128, 64, 32, 16, 8))
    blocks = (_nbytes((tm, d), F32) + _nbytes((d, tn), BF16) + _nbytes((tm, tn), F32)
              + _nbytes((tm, d), BF16))
    return pl.pallas_call(
        _norm_matmul_kernel,
        name=name,
        grid=(n // tm, c // tn),
        in_specs=[
            pl.BlockSpec((tm, d), lambda i, j: (i, 0)),
            pl.BlockSpec((1, d), lambda i, j: (0, 0)),
            pl.BlockSpec((d, tn), lambda i, j: (0, j)),
            pl.BlockSpec((1, tn), lambda i, j: (0, j)),
        ],
        out_specs=pl.BlockSpec((tm, tn), lambda i, j: (i, j)),
        out_shape=jax.ShapeDtypeStruct((n, c), F32),
        scratch_shapes=[pltpu.VMEM((tm, d), BF16)],
        compiler_params=_params(("arbitrary", "arbitrary"), blocks),
    )(x, g, w, b)


def _mlstm_kernel(*refs, chunk, nseq):
    (q_ref, k_ref, v_ref, o_ref, gt_ref, c0_ref, n0_ref, m0_ref, gn_ref) = refs[:9]
    h_ref, c1_ref, n1_ref, m1_ref, c_sc, n_sc, m_sc = refs[-7:]
    L = chunk
    ci = pl.program_id(1)

    @pl.when(ci == 0)
    def _():
        c_sc[...] = c0_ref[0]
        n_sc[...] = n0_ref[...]
        m_sc[...] = m0_ref[...]

    lane = lax.broadcasted_iota(jnp.int32, (1, LANES), 1)
    row = lax.broadcasted_iota(jnp.int32, (L, L), 0)
    col = lax.broadcasted_iota(jnp.int32, (L, L), 1)
    causal = row >= col
    tri = causal.astype(F32)
    eye8 = (lax.broadcasted_iota(jnp.int32, (SUBLANES, LANES), 0)
            == lax.broadcasted_iota(jnp.int32, (SUBLANES, LANES), 1)).astype(F32)

    for s in range(nseq):
        r0 = s * L
        gates = gt_ref[r0:r0 + L, :]
        b_all = jnp.dot(tri, -_softplus(-gates), precision=HI, preferred_element_type=F32)
        z = jnp.where(lane < M_HEADS, gates, b_all)
        z_rows = lax.dot_general(eye8, z, (((1,), (1,)), ((), ())), precision=HI,
                                 preferred_element_type=F32)
        for h in range(M_HEADS):
            q = q_ref[r0:r0 + L, h * M_DK:(h + 1) * M_DK] * (M_DK ** -0.5)
            k = k_ref[r0:r0 + L, h * M_DK:(h + 1) * M_DK]
            v = v_ref[r0:r0 + L, h * M_DV:(h + 1) * M_DV]
            i_col = z[:, h:h + 1]
            b_col = z[:, M_HEADS + h:M_HEADS + h + 1]
            i_row = z_rows[h:h + 1, :]
            b_row = z_rows[M_HEADS + h:M_HEADS + h + 1, :]
            m_prev = m_sc[s, h]
            c_prev = c_sc[s, h]
            n_prev = n_sc[s, h]

            dmat = jnp.where(causal, b_col - b_row + i_row, -jnp.inf)
            m_inter = b_col + m_prev
            m_t = jnp.maximum(m_inter, jnp.max(dmat, axis=-1, keepdims=True))
            sc = lax.dot_general(q, k, (((1,), (1,)), ((), ())), preferred_element_type=F32)
            sc = sc * jnp.exp(dmat - m_t)
            inter = jnp.exp(m_inter - m_t)
            qc = lax.dot_general(q, c_prev, (((1,), (1,)), ((), ())), preferred_element_type=F32)
            num = jnp.dot(sc, v, preferred_element_type=F32) + inter * qc
            den = (jnp.sum(sc, axis=-1, keepdims=True)
                   + inter * jnp.sum(q * n_prev, axis=-1, keepdims=True))
            hh = num / jnp.maximum(jnp.abs(den), jnp.exp(-m_t))

            m_new = m_t[L - 1:L, :]
            b_last = b_col[L - 1:L, :]
            w_s = jnp.exp(b_last - b_col + i_col - m_new)
            decay = jnp.exp(b_last + m_prev - m_new)
            c_sc[s, h] = decay * c_prev + lax.dot_general(
                v * w_s, k, (((0,), (0,)), ((), ())), preferred_element_type=F32)
            n_sc[s, h] = decay * n_prev + jnp.sum(w_s * k, axis=0, keepdims=True)
            m_sc[s, h] = m_new

            hn = (hh * lax.rsqrt(jnp.mean(hh * hh, axis=-1, keepdims=True) + RMS_EPS)
                  * gn_ref[:, h * M_DV:(h + 1) * M_DV])
            h_ref[r0:r0 + L, h * M_DV:(h + 1) * M_DV] = hn * _sigmoid(
                o_ref[r0:r0 + L, h * M_DV:(h + 1) * M_DV])

    @pl.when(ci == pl.num_programs(1) - 1)
    def _():
        c1_ref[0] = c_sc[...]
        n1_ref[...] = n_sc[...]
        m1_ref[...] = m_sc[...]


def _mlstm(p_mix, gates, c0_all, n0, m0, gn, hm_buf, c_buf, *, layer, depth, row0, batch, seq, name):
    n_all = p_mix.shape[0]
    layer_in = min(layer, c0_all.shape[0] - 1)
    L = M_CHUNK if seq % M_CHUNK == 0 else seq
    nc = seq // L
    nseq = 1 if nc > 1 else _pick_tile(batch, (4, 2, 1))
    rows = nseq * L
    rb0 = row0 // rows
    assert row0 % rows == 0
    qk_w = M_HEADS * M_DK
    v_w = M_HEADS * M_DV
    assert qk_w * 2 == v_w

    def tok(bi, c):
        return rb0 + bi * nc + c

    state_map = lambda bi, c: (bi, 0, 0, 0)
    c_blk = (1, nseq, M_HEADS, M_DV, M_DK)
    in_specs = [
        pl.BlockSpec((rows, qk_w), lambda bi, c: (tok(bi, c), 0)),
        pl.BlockSpec((rows, qk_w), lambda bi, c: (tok(bi, c), 1)),
        pl.BlockSpec((rows, v_w), lambda bi, c: (tok(bi, c), 1)),
        pl.BlockSpec((rows, v_w), lambda bi, c: (tok(bi, c), 2)),
        pl.BlockSpec((rows, LANES), lambda bi, c: (tok(bi, c), 0)),
        pl.BlockSpec(c_blk, lambda bi, c: (layer_in, bi, 0, 0, 0)),
        pl.BlockSpec((nseq, M_HEADS, 1, M_DK), state_map),
        pl.BlockSpec((nseq, M_HEADS, 1, 1), state_map),
        pl.BlockSpec((1, v_w), lambda bi, c: (0, 0)),
    ]
    args = [p_mix, p_mix, p_mix, p_mix, gates, c0_all, n0.reshape(batch, M_HEADS, 1, M_DK),
            m0.reshape(batch, M_HEADS, 1, 1), gn]
    aliases = {}
    for buf, out_idx in ((hm_buf, 0), (c_buf, 1)):
        if buf is not None:
            aliases[len(args)] = out_idx
            in_specs.append(ANY_SPEC)
            args.append(buf)
    blocks = (2 * _nbytes((rows, qk_w), F32) + 3 * _nbytes((rows, v_w), F32)
              + 3 * _nbytes((nseq, M_HEADS, M_DV, M_DK), F32))
    return pl.pallas_call(
        functools.partial(_mlstm_kernel, chunk=L, nseq=nseq),
        name=name,
        grid=(batch // nseq, nc),
        in_specs=in_specs,
        out_specs=[
            pl.BlockSpec((rows, v_w), lambda bi, c: (tok(bi, c), 0)),
            pl.BlockSpec(c_blk, lambda bi, c: (layer, bi, 0, 0, 0)),
            pl.BlockSpec((nseq, M_HEADS, 1, M_DK), state_map),
            pl.BlockSpec((nseq, M_HEADS, 1, 1), state_map),
        ],
        out_shape=[
            jax.ShapeDtypeStruct((n_all, v_w), F32),
            jax.ShapeDtypeStruct((depth, batch, M_HEADS, M_DV, M_DK), F32),
            jax.ShapeDtypeStruct((batch, M_HEADS, 1, M_DK), F32),
            jax.ShapeDtypeStruct((batch, M_HEADS, 1, 1), F32),
        ],
        scratch_shapes=[
            pltpu.VMEM((nseq, M_HEADS, M_DV, M_DK), F32),
            pltpu.VMEM((nseq, M_HEADS, 1, M_DK), F32),
            pltpu.VMEM((nseq, M_HEADS, 1, 1), F32),
        ],
        input_output_aliases=aliases,
        compiler_params=_params(("arbitrary", "arbitrary"), blocks),
    )(*args)


def _seq_rows(x, offset, *, seq):
    rows = x.shape[0]
    i = lax.broadcasted_iota(jnp.int32, (rows // seq, rows), 0)
    r = lax.broadcasted_iota(jnp.int32, (rows // seq, rows), 1)
    pick = (r == i * seq + offset).astype(F32)
    return jnp.dot(pick, x, precision=HI, preferred_element_type=F32)


def _seq_place(per_seq, offset, *, seq):
    nseq = per_seq.shape[0]
    r = lax.broadcasted_iota(jnp.int32, (nseq * seq, nseq), 0)
    i = lax.broadcasted_iota(jnp.int32, (nseq * seq, nseq), 1)
    pick = (r == i * seq + offset).astype(F32)
    return jnp.dot(pick, per_seq, precision=HI, preferred_element_type=F32)


def _shift_rows(x, shift, *, seq, tile_has_whole_seqs, before):
    rows = x.shape[0]
    rolled = pltpu.roll(x, shift, 0)
    r = lax.broadcasted_iota(jnp.int32, (rows, 1), 0)
    if tile_has_whole_seqs:
        head = _seq_place(before[0], 0, seq=seq)
        for s in range(1, shift):
            head = head + _seq_place(before[s], s, seq=seq)
        return jnp.where((r % seq) < shift, head, rolled)
    out = rolled
    for s in range(shift):
        out = jnp.where(r == s, before[s:s + 1, :], out)
    return out


def _hk_to_kh(a, axis):
    shape = a.shape
    a = a.reshape(shape[:axis] + (R_HEADS, R_HD) + shape[axis + 1:])
    return jnp.swapaxes(a, axis, axis + 1).reshape(shape)


def _kh_to_hk(a, axis):
    shape = a.shape
    a = a.reshape(shape[:axis] + (R_HD, R_HEADS) + shape[axis + 1:])
    return jnp.swapaxes(a, axis, axis + 1).reshape(shape)


def _sections(a, axis, fn):
    rw = R_HEADS * R_HD
    idx = lambda lo, hi: tuple(slice(lo, hi) if d == axis else slice(None) for d in range(a.ndim))
    parts = [fn(a[idx(i * rw, (i + 1) * rw)], axis) for i in range(3)]
    if a.shape[axis] > 3 * rw:
        parts.append(a[idx(3 * rw, a.shape[axis])])
    return jnp.concatenate(parts, axis=axis)


def _rwkv_prep_kernel(x_ref, gn_ref, wp_ref, bp_ref, st_ref, mu_ref, wl_ref, w0_ref, a0_ref,
                      rr_ref, rk_ref, rv_ref, dec_ref, a_ref, g_ref, sh_ref, carry_sc,
                      *, seq, whole_seqs):
    xin = x_ref[...]
    tb = xin.shape[0]
    rw = R_HEADS * R_HD
    ms = jnp.mean(xin * xin, axis=-1, keepdims=True)
    h = (xin * lax.rsqrt(ms + RMS_EPS) * gn_ref[...]).astype(BF16)
    x = jnp.dot(h, wp_ref[...], preferred_element_type=F32) + bp_ref[...]
    if whole_seqs:
        prev = _shift_rows(x, 1, seq=seq, tile_has_whole_seqs=True, before=[st_ref[...]])
        sh_ref[...] = _seq_rows(x, seq - 1, seq=seq)
    else:
        @pl.when(pl.program_id(1) == 0)
        def _():
            carry_sc[...] = st_ref[0]
        prev = _shift_rows(x, 1, seq=seq, tile_has_whole_seqs=False, before=carry_sc[...])
        carry_sc[...] = x[tb - 1:tb, :]
        sh_ref[0] = x[tb - 1:tb, :]
    xs = x + (prev - x) * mu_ref[...]
    rr_ref[...] = xs[:, 0:rw]
    rk_ref[...] = xs[:, rw:2 * rw]
    rv_ref[...] = xs[:, 2 * rw:3 * rw]
    z = xs[:, 3 * rw:3 * rw + R_LORA]
    lane = lax.broadcasted_iota(jnp.int32, (1, R_LORA), 1)
    act = jnp.where(lane < 64, jnp.tanh(z), jnp.where(lane < 128, z, _sigmoid(z)))
    lo = jnp.dot(act.astype(BF16), wl_ref[...], preferred_element_type=F32)
    dec_ref[...] = jnp.exp(-jnp.exp(-_softplus(-(w0_ref[...] + lo[:, 0:rw])) - 0.5))
    a_ref[...] = _sigmoid(a0_ref[...] + lo[:, rw:2 * rw])
    g_ref[...] = lo[:, 2 * rw:3 * rw]


def _rwkv_prep(x, gn, w_rw, b_rw, shift_state, mu, w_lora, w0, a0, *, row0, batch, seq, name):
    d = x.shape[1]
    c = w_rw.shape[1]
    rw = R_HEADS * R_HD
    n_tok = batch * seq
    whole = seq <= 64
    if whole:
        tb = _pick_tile(n_tok, (256, 128, 64, 32, 16, 8))
        assert tb % seq == 0 and row0 % tb == 0
        grid = (n_tok // tb, 1)
        rb0 = row0 // tb
        tok_map = lambda i, j: (rb0 + i, 0)
        out_map = lambda i, j: (i, 0)
        state = shift_state
        st_spec = pl.BlockSpec((tb // seq, c), out_map)
        sh_shape = jax.ShapeDtypeStruct((batch, c), F32)
    else:
        tb = _pick_tile(seq, (256, 128, 64, 32, 16, 8))
        nt = seq // tb
        grid = (batch, nt)
        rb0 = row0 // tb
        tok_map = lambda b, j: (rb0 + b * nt + j, 0)
        out_map = lambda b, j: (b * nt + j, 0)
        state = shift_state.reshape(batch, 1, c)
        st_spec = pl.BlockSpec((1, 1, c), lambda b, j: (b, 0, 0))
        sh_shape = jax.ShapeDtypeStruct((batch, 1, c), F32)
    fixed = lambda i, j: (0, 0)
    tok_out = jax.ShapeDtypeStruct((n_tok, rw), F32)
    tok_spec = pl.BlockSpec((tb, rw), out_map)
    blocks = (4 * _nbytes((tb, c), F32) + 6 * _nbytes((tb, rw), F32) + _nbytes(w_lora.shape, BF16)
              + _nbytes(w_rw.shape, BF16))
    outs = pl.pallas_call(
        functools.partial(_rwkv_prep_kernel, seq=seq, whole_seqs=whole),
        name=name,
        grid=grid,
        in_specs=[
            pl.BlockSpec((tb, d), tok_map),
            pl.BlockSpec((1, d), fixed),
            pl.BlockSpec(w_rw.shape, fixed),
            pl.BlockSpec((1, c), fixed),
            st_spec,
            pl.BlockSpec((1, c), fixed),
            pl.BlockSpec(w_lora.shape, fixed),
            pl.BlockSpec((1, rw), fixed),
            pl.BlockSpec((1, rw), fixed),
        ],
        out_specs=[tok_spec] * 6 + [st_spec],
        out_shape=[tok_out] * 6 + [sh_shape],
        scratch_shapes=[pltpu.VMEM((1, c), F32)],
        compiler_params=_params(("arbitrary", "arbitrary"), blocks),
    )(x, gn, w_rw, b_rw, state, mu, w_lora, w0, a0)
    return (*outs[:6], outs[6].reshape(batch, c))


def _rwkv_scan_kernel(rr_ref, rk_ref, dec_ref, a_ref, rv_ref, kks_ref, ka_ref, rrk_ref,
                      gng_ref, gnb_ref, s0_ref, y_ref, s1_ref,
                      s_sc, w_sc, kka_sc, km_sc, nkk_sc, wr_sc, coef_sc, *, tb):
    j = pl.program_id(1)
    nk = R_HD

    @pl.when(j == 0)
    def _():
        s_sc[...] = s0_ref[0]

    rr = rr_ref[...]
    rk = rk_ref[...]
    a = a_ref[...]
    dec = dec_ref[...]
    kk = rk * kks_ref[...]
    nrm = jnp.sqrt(jnp.sum(kk * kk, axis=0, keepdims=True))
    kk = kk / jnp.maximum(nrm, 1e-12)
    kmod = rk * (1.0 + (a - 1.0) * ka_ref[...])
    kka = kk * a
    g2 = dec.reshape(nk * tb, LANES)
    tpos = lax.broadcasted_iota(jnp.int32, (nk * tb, 1), 0) % tb
    shift = 1
    while shift < tb:
        g2 = g2 * jnp.where(tpos >= shift, pltpu.roll(g2, shift, 0), 1.0)
        shift *= 2
    g_prev = jnp.where(tpos >= 1, pltpu.roll(g2, 1, 0), 1.0).reshape(nk, tb, LANES)
    g_run = g2.reshape(nk, tb, LANES)
    g_inv = 1.0 / g_run
    w_sc[...] = g_run
    kka_sc[...] = kka * g_inv
    km_sc[...] = kmod * g_inv
    nkk_sc[...] = -kk * g_prev
    wr_sc[...] = rr * g_run
    coef_sc[0] = jnp.sum(kka * rr, axis=0)
    coef_sc[1] = jnp.sum(kmod * rr, axis=0)
    coef_sc[2] = jnp.sum(rr * kmod * rrk_ref[...], axis=0)

    def row(ref, t, k):
        return jnp.broadcast_to(ref[k, pl.ds(t, 1), :], (nk, LANES))

    def coef(i, t):
        return jnp.broadcast_to(coef_sc[i, pl.ds(t, 1), :], (nk, LANES))

    def matvecs(t):
        acc_sa = jnp.zeros((nk, LANES), F32)
        acc_y = jnp.zeros((nk, LANES), F32)
        for k in range(nk):
            sk = s_sc[k]
            acc_sa = acc_sa + sk * row(nkk_sc, t, k)
            acc_y = acc_y + sk * row(wr_sc, t, k)
        return acc_sa, acc_y

    def emit(t, sa, y0):
        val = rv_ref[t]
        y = y0 + sa * coef(0, t) + val * coef(1, t)
        mean = jnp.sum(y, axis=0, keepdims=True) * (1.0 / nk)
        d = y - mean
        var = jnp.sum(d * d, axis=0, keepdims=True) * (1.0 / nk)
        yn = d * lax.rsqrt(var + GN_EPS) * gng_ref[...] + gnb_ref[...]
        y_ref[t] = yn + coef(2, t) * val
        return val

    def step(t, carry):
        sa, y0 = carry
        val = emit(t, sa, y0)
        acc_sa = jnp.zeros((nk, LANES), F32)
        acc_y = jnp.zeros((nk, LANES), F32)
        for k in range(nk):
            sk = s_sc[k] + sa * row(kka_sc, t, k) + val * row(km_sc, t, k)
            s_sc[k] = sk
            acc_sa = acc_sa + sk * row(nkk_sc, t + 1, k)
            acc_y = acc_y + sk * row(wr_sc, t + 1, k)
        return acc_sa, acc_y

    sa, y0 = lax.fori_loop(0, tb - 1, step, matvecs(0))
    t_last = tb - 1
    val = emit(t_last, sa, y0)
    for k in range(nk):
        s_sc[k] = ((s_sc[k] + sa * row(kka_sc, t_last, k) + val * row(km_sc, t_last, k))
                   * row(w_sc, t_last, k))

    @pl.when(j == pl.num_programs(1) - 1)
    def _():
        s1_ref[0] = s_sc[...]


LAYOUT_TB = 128


def _lane_slabs(in_ref, p):
    nb = HEADS_PER_LANE_GROUP
    zts = [in_ref[b, :, p * LANES:(p + 1) * LANES].T for b in range(nb)]
    for i in range(SUBLANES):
        d = jnp.concatenate([zt[i * R_HEADS:(i + 1) * R_HEADS, :] for zt in zts], axis=0)
        yield SUBLANES * p + i, d.T


def _lanes_in_kernel(src_ref, dst_ref, *, rows):
    for p in range(R_HD // SUBLANES):
        for i, slab in _lane_slabs(src_ref, p):
            if rows:
                dst_ref[i] = slab
            else:
                dst_ref[:, pl.ds(i, 1), :] = slab[:, None, :]


def _lanes_in(x, *, batch, seq, rows):
    nb = HEADS_PER_LANE_GROUP
    g = batch // nb
    w = R_HEADS * R_HD
    if seq % LAYOUT_TB != 0:
        x = x.reshape(g, nb, seq, R_HD, R_HEADS)
        if rows:
            return x.transpose(3, 0, 2, 1, 4).reshape(R_HD, g * seq, LANES)
        return x.transpose(0, 2, 3, 1, 4).reshape(g * seq, R_HD, LANES)
    nt = seq // LAYOUT_TB
    if rows:
        out_spec = pl.BlockSpec((R_HD, LAYOUT_TB, LANES), lambda gi, j: (0, gi * nt + j, 0))
        out_shape = jax.ShapeDtypeStruct((R_HD, g * seq, LANES), F32)
    else:
        out_spec = pl.BlockSpec((LAYOUT_TB, R_HD, LANES), lambda gi, j: (gi * nt + j, 0, 0))
        out_shape = jax.ShapeDtypeStruct((g * seq, R_HD, LANES), F32)
    return pl.pallas_call(
        functools.partial(_lanes_in_kernel, rows=rows),
        name="lanes_in_rows" if rows else "lanes_in_full",
        grid=(g, nt),
        in_specs=[pl.BlockSpec((nb, LAYOUT_TB, w), lambda gi, j: (gi, j, 0))],
        out_specs=out_spec,
        out_shape=out_shape,
        compiler_params=_params(("arbitrary", "arbitrary"), 2 * _nbytes((nb, LAYOUT_TB, w), F32)),
    )(x.reshape(batch, seq, w))


def _lanes_out_kernel(y_ref, o_ref):
    nb = HEADS_PER_LANE_GROUP
    for p in range(R_HD // SUBLANES):
        pieces = [[] for _ in range(nb)]
        for i in range(SUBLANES):
            st = y_ref[:, SUBLANES * p + i, :].T
            for b in range(nb):
                pieces[b].append(st[b * R_HEADS:(b + 1) * R_HEADS, :])
        for b in range(nb):
            zt = jnp.concatenate(pieces[b], axis=0)
            o_ref[b, :, p * LANES:(p + 1) * LANES] = zt.T


def _lanes_out(y, *, batch, seq):
    nb = HEADS_PER_LANE_GROUP
    g = batch // nb
    w = R_HEADS * R_HD
    if seq % LAYOUT_TB != 0:
        y = y.reshape(g, seq, R_HD, nb, R_HEADS)
        return y.transpose(0, 3, 1, 2, 4).reshape(batch * seq, w)
    nt = seq // LAYOUT_TB
    out = pl.pallas_call(
        _lanes_out_kernel,
        name="lanes_out",
        grid=(g, nt),
        in_specs=[pl.BlockSpec((LAYOUT_TB, R_HD, LANES), lambda gi, j: (gi * nt + j, 0, 0))],
        out_specs=pl.BlockSpec((nb, LAYOUT_TB, w), lambda gi, j: (gi, j, 0)),
        out_shape=jax.ShapeDtypeStruct((batch, seq, w), F32),
        compiler_params=_params(("arbitrary", "arbitrary"), 2 * _nbytes((nb, LAYOUT_TB, w), F32)),
    )(y)
    return out.reshape(batch * seq, w)


def _param_lanes(p):
    p = p.reshape(R_HEADS, R_HD).T
    return jnp.tile(p, (1, HEADS_PER_LANE_GROUP))


def _rwkv_scan(rr, rk, rv, dec, a, kks, ka, rrk, gng, gnb, s0, *, batch, seq, name):
    assert batch % HEADS_PER_LANE_GROUP == 0
    g = batch // HEADS_PER_LANE_GROUP
    tb = _pick_tile(seq, (32, 16, 8))
    nt = seq // tb
    rows = [_lanes_in(v, batch=batch, seq=seq, rows=True) for v in (rr, rk, dec, a)]
    val = _lanes_in(rv, batch=batch, seq=seq, rows=False)
    s0l = s0.reshape(g, HEADS_PER_LANE_GROUP, R_HEADS, R_HD, R_HD).transpose(0, 4, 3, 1, 2)
    s0l = s0l.reshape(g, R_HD, R_HD, LANES)
    row_spec = pl.BlockSpec((R_HD, tb, LANES), lambda gi, j: (0, gi * nt + j, 0))
    val_spec = pl.BlockSpec((tb, R_HD, LANES), lambda gi, j: (gi * nt + j, 0, 0))
    kpar_spec = pl.BlockSpec((R_HD, 1, LANES), lambda gi, j: (0, 0, 0))
    vpar_spec = pl.BlockSpec((R_HD, LANES), lambda gi, j: (0, 0))
    st_spec = pl.BlockSpec((1, R_HD, R_HD, LANES), lambda gi, j: (gi, 0, 0, 0))
    vec_bytes = _nbytes((tb, R_HD, LANES), F32)
    blocks = 11 * vec_bytes + 3 * _nbytes((R_HD, R_HD, LANES), F32)
    kpar = lambda p: _param_lanes(p)[:, None, :]
    y, s1l = pl.pallas_call(
        functools.partial(_rwkv_scan_kernel, tb=tb),
        name=name,
        grid=(g, nt),
        in_specs=[row_spec] * 4 + [val_spec] + [kpar_spec] * 3 + [vpar_spec] * 2 + [st_spec],
        out_specs=[val_spec, st_spec],
        out_shape=[
            jax.ShapeDtypeStruct((g * seq, R_HD, LANES), F32),
            jax.ShapeDtypeStruct(s0l.shape, F32),
        ],
        scratch_shapes=[pltpu.VMEM((R_HD, R_HD, LANES), F32)]
        + [pltpu.VMEM((R_HD, tb, LANES), F32)] * 5
        + [pltpu.VMEM((3, tb, LANES), F32)],
        compiler_params=_params(("arbitrary", "arbitrary"), blocks),
    )(*rows, val, kpar(kks), kpar(ka), kpar(rrk), _param_lanes(gng), _param_lanes(gnb), s0l)
    s1 = s1l.reshape(g, R_HD, R_HD, HEADS_PER_LANE_GROUP, R_HEADS).transpose(0, 3, 4, 2, 1)
    return _lanes_out(y, batch=batch, seq=seq), s1.reshape(batch, R_HEADS, R_HD, R_HD)


def _mix_out_kernel(hm_ref, yr_ref, g_ref, gm_ref, gr_ref, x_ref, wm_ref, wr_ref, wo_ref, *rest):
    o_ref = rest[-1]
    bm = jnp.dot(hm_ref[...].astype(BF16), wm_ref[...], preferred_element_type=F32)
    yr = (yr_ref[...] * g_ref[...]).astype(BF16)
    br = jnp.dot(yr, wr_ref[...], preferred_element_type=F32)
    merged = _sigmoid(gm_ref[...]) * bm + _sigmoid(gr_ref[...]) * br
    o_ref[...] = x_ref[...] + jnp.dot(merged.astype(BF16), wo_ref[...], preferred_element_type=F32)


def _mix_out(hm, yr, g, p_mix, x, wm, wr, wo, out_buf, *, row0, name):
    n_all, d = x.shape
    n = yr.shape[0]
    tm = _pick_tile(n, (256, 128, 64, 32, 16, 8))
    assert row0 % tm == 0
    rb0 = row0 // tm
    gm_blk = (2 * M_HEADS * (M_DK + M_DV)) // d
    shared = lambda i: (rb0 + i, 0)
    own = lambda i: (i, 0)
    fixed = lambda i: (0, 0)
    aliased = out_buf is not None
    in_specs = [
        pl.BlockSpec((tm, wm.shape[0]), shared),
        pl.BlockSpec((tm, wr.shape[0]), own),
        pl.BlockSpec((tm, wr.shape[0]), own),
        pl.BlockSpec((tm, d), lambda i: (rb0 + i, gm_blk)),
        pl.BlockSpec((tm, d), lambda i: (rb0 + i, gm_blk + 1)),
        pl.BlockSpec((tm, d), shared),
        pl.BlockSpec(wm.shape, fixed),
        pl.BlockSpec(wr.shape, fixed),
        pl.BlockSpec(wo.shape, fixed),
    ]
    args = [hm, yr, g, p_mix, p_mix, x, wm, wr, wo]
    if aliased:
        in_specs.append(ANY_SPEC)
        args.append(out_buf)
    blocks = 7 * _nbytes((tm, d), F32) + 3 * _nbytes((d, d), BF16)
    return pl.pallas_call(
        _mix_out_kernel,
        name=name,
        grid=(n // tm,),
        in_specs=in_specs,
        out_specs=pl.BlockSpec((tm, d), shared),
        out_shape=jax.ShapeDtypeStruct((n_all, d), F32),
        input_output_aliases={9: 0} if aliased else {},
        compiler_params=_params(("arbitrary",), blocks),
    )(*args)


FFN_CHUNK = 256


def _ffn_kernel(x_ref, gn_ref, wu_ref, cs_ref, cw_ref, cb_ref, wd_ref, *rest, seq, whole_seqs):
    o_ref, st_ref, carry_sc = rest[-3:]
    x = x_ref[...]
    tb = x.shape[0]
    dff = wd_ref.shape[0]
    ms = jnp.mean(x * x, axis=-1, keepdims=True)
    h = (x * lax.rsqrt(ms + RMS_EPS) * gn_ref[...]).astype(BF16)
    if not whole_seqs:
        @pl.when(pl.program_id(1) == 0)
        def _():
            carry_sc[...] = cs_ref[0]
    acc = x
    for c0 in range(0, dff, FFN_CHUNK):
        halves = []
        for lo in (c0, dff + c0):
            cols = slice(lo, lo + FFN_CHUNK)
            u = jnp.dot(h, wu_ref[:, cols], preferred_element_type=F32)
            if whole_seqs:
                before = [cs_ref[0, :, cols], cs_ref[1, :, cols]]
                p1 = _shift_rows(u, 1, seq=seq, tile_has_whole_seqs=True, before=before[1:])
                p2 = _shift_rows(u, 2, seq=seq, tile_has_whole_seqs=True, before=before)
                st_ref[0, :, cols] = _seq_rows(u, seq - 2, seq=seq)
                st_ref[1, :, cols] = _seq_rows(u, seq - 1, seq=seq)
            else:
                carry = carry_sc[:, cols]
                p1 = _shift_rows(u, 1, seq=seq, tile_has_whole_seqs=False, before=carry[1:2, :])
                p2 = _shift_rows(u, 2, seq=seq, tile_has_whole_seqs=False, before=carry)
                carry_sc[:, cols] = u[tb - 2:tb, :]
                st_ref[0, :, cols] = u[tb - 2:tb, :]
            halves.append(cb_ref[:, cols] + p2 * cw_ref[0:1, cols] + p1 * cw_ref[1:2, cols]
                          + u * cw_ref[2:3, cols])
        gate, val = halves
        act = (gate * _sigmoid(gate) * val).astype(BF16)
        acc = acc + jnp.dot(act, wd_ref[c0:c0 + FFN_CHUNK, :], preferred_element_type=F32)
    o_ref[...] = acc


def _ffn(x, gn, wu, conv_state, cw, cb, wd, out_buf, *, row0, batch, seq, name):
    n_all, d = x.shape
    c = wu.shape[1]
    n_tok = batch * seq
    taps = cw.shape[0]
    assert taps == 3 and seq >= taps - 1 and wd.shape[0] % FFN_CHUNK == 0
    whole = seq <= 64
    if whole:
        tb = _pick_tile(n_tok, (128, 64, 32, 16, 8))
        assert tb % seq == 0 and row0 % tb == 0
        grid = (n_tok // tb, 1)
        rb0 = row0 // tb
        tok_map = lambda i, j: (rb0 + i, 0)
        state = jnp.swapaxes(conv_state, 0, 1)
        st_shape = jax.ShapeDtypeStruct((taps - 1, batch, c), F32)
        st_spec = pl.BlockSpec((taps - 1, tb // seq, c), lambda i, j: (0, i, 0))
    else:
        tb = _pick_tile(seq, (256, 128, 64, 32, 16, 8))
        nt = seq // tb
        grid = (batch, nt)
        rb0 = row0 // tb
        tok_map = lambda b, j: (rb0 + b * nt + j, 0)
        state = conv_state
        st_shape = jax.ShapeDtypeStruct((batch, taps - 1, c), F32)
        st_spec = pl.BlockSpec((1, taps - 1, c), lambda b, j: (b, 0, 0))
    fixed = lambda i, j: (0, 0)
    aliased = out_buf is not None
    in_specs = [
        pl.BlockSpec((tb, d), tok_map),
        pl.BlockSpec((1, d), fixed),
        pl.BlockSpec(wu.shape, fixed),
        st_spec,
        pl.BlockSpec(cw.shape, fixed),
        pl.BlockSpec((1, c), fixed),
        pl.BlockSpec(wd.shape, fixed),
    ]
    args = [x, gn, wu, state, cw, cb, wd]
    if aliased:
        in_specs.append(ANY_SPEC)
        args.append(out_buf)
    blocks = _nbytes(wu.shape, BF16) + _nbytes(wd.shape, BF16) + 6 * _nbytes((tb, d), F32)
    y, st = pl.pallas_call(
        functools.partial(_ffn_kernel, seq=seq, whole_seqs=whole),
        name=name,
        grid=grid,
        in_specs=in_specs,
        out_specs=[pl.BlockSpec((tb, d), tok_map), st_spec],
        out_shape=[jax.ShapeDtypeStruct((n_all, d), F32), st_shape],
        scratch_shapes=[pltpu.VMEM((taps - 1, c), F32)],
        input_output_aliases={7: 0} if aliased else {},
        compiler_params=_params(("arbitrary", "arbitrary"), blocks),
    )(*args)
    return y, (jnp.swapaxes(st, 0, 1) if whole else st)


def _final_norm_kernel(x_ref, g_ref, o_ref):
    x = x_ref[...]
    ms = jnp.mean(x * x, axis=-1, keepdims=True)
    o_ref[...] = x * lax.rsqrt(ms + RMS_EPS) * g_ref[...]


def _final_norm(x, g, *, row0, n, name):
    d = x.shape[1]
    tm = _pick_tile(n, (1024, 512, 256, 128, 64, 32, 16, 8))
    assert row0 % tm == 0
    rb0 = row0 // tm
    return pl.pallas_call(
        _final_norm_kernel,
        name=name,
        grid=(n // tm,),
        in_specs=[pl.BlockSpec((tm, d), lambda i: (rb0 + i, 0)),
                  pl.BlockSpec((1, d), lambda i: (0, 0))],
        out_specs=pl.BlockSpec((tm, d), lambda i: (i, 0)),
        out_shape=jax.ShapeDtypeStruct((n, d), F32),
        compiler_params=_params(("arbitrary",), 2 * _nbytes((tm, d), F32)),
    )(x, g)


def kernel(x_prompt, x_sample, state_mlstm_C, state_mlstm_n, state_mlstm_m, state_rwkv_S, state_rwkv_shift, state_ffn_conv, norm_mix_g, w_in, b_in, mu_shift, m_norm_g, r_w0, r_w_up, r_a0, r_a_up, r_g_up, r_k_k, r_k_a, r_r_k, r_gn_g, r_gn_b, w_br_m, w_br_r, w_out, norm_ffn_g, w_up, conv_w, conv_b, w_down, norm_final_g):
    bp, tp, d = x_prompt.shape
    bs, ts, _ = x_sample.shape
    depth = w_in.shape[0]
    n_p = bp * tp
    n_s = bs * ts
    rw = R_HEADS * R_HD
    qkvo = 2 * M_HEADS * (M_DK + M_DV)
    n_gate = 2 * M_HEADS
    mix_cols = qkvo + n_gate + 2 * d
    rw_cols = 3 * rw + R_LORA
    dff2 = w_up.shape[2]
    assert w_in.shape[2] == mix_cols + rw_cols

    x = jnp.concatenate([x_prompt.reshape(n_p, d), x_sample.reshape(n_s, d)], axis=0)

    zc = jnp.zeros((1, bp, M_HEADS, M_DV, M_DK), F32)
    c_all_p = c_all_s = None
    zn = jnp.zeros((bp, M_HEADS, M_DK), F32)
    zm = jnp.zeros((bp, M_HEADS), F32)
    zs = jnp.zeros((bp, R_HEADS, R_HD, R_HD), F32)
    zsh = jnp.zeros((bp, rw_cols), F32)
    zcv = jnp.zeros((bp, conv_w.shape[1] - 1, dff2), F32)

    outs_p = [[] for _ in range(5)]
    outs_s = [[] for _ in range(5)]
    for l in range(depth):
        wl = w_in[l]
        bl = b_in[l]
        w_mix = jnp.concatenate([wl[:, :qkvo], wl[:, qkvo + n_gate:mix_cols]], axis=1).astype(BF16)
        b_mix = jnp.concatenate([bl[:qkvo], bl[qkvo + n_gate:mix_cols]])[None]
        w_gate = jnp.pad(wl[:, qkvo:qkvo + n_gate], ((0, 0), (0, LANES - n_gate))).astype(BF16)
        b_gate = jnp.pad(bl[qkvo:qkvo + n_gate], (0, LANES - n_gate))[None]
        w_rw = _sections(wl[:, mix_cols:], 1, _hk_to_kh).astype(BF16)
        b_rw = _sections(bl[mix_cols:], 0, _hk_to_kh)[None]
        mu_rw = _sections(mu_shift[l], 0, _hk_to_kh)[None]
        w_lora = jnp.zeros((R_LORA, 3 * rw), F32)
        w_lora = w_lora.at[0:64, 0:rw].set(r_w_up[l]).at[64:128, rw:2 * rw].set(r_a_up[l])
        w_lora = _sections(w_lora.at[128:256, 2 * rw:].set(r_g_up[l]), 1, _hk_to_kh).astype(BF16)
        w_br_r_kh = _hk_to_kh(w_br_r[l], 0).astype(BF16)
        g_mix = norm_mix_g[l][None]

        p_mix = _norm_matmul(x, g_mix, w_mix, b_mix, tn=1024, name="in_proj_mix")
        gates = _norm_matmul(x, g_mix, w_gate, b_gate, tn=LANES, name="in_proj_gates")

        gn = m_norm_g[l].reshape(1, M_HEADS * M_DV)
        hm, c_all_p, nn_p, m_p = _mlstm(
            p_mix, gates, zc, zn, zm, gn, None, c_all_p, layer=l, depth=depth,
            row0=0, batch=bp, seq=tp, name="mlstm_prompt")
        hm, c_all_s, nn_s, m_s = _mlstm(
            p_mix, gates, state_mlstm_C, state_mlstm_n[l], state_mlstm_m[l], gn, hm, c_all_s,
            layer=l, depth=depth, row0=n_p, batch=bs, seq=ts, name="mlstm_sample")

        prep_args = (mu_rw, w_lora, _hk_to_kh(r_w0[l], 0)[None], _hk_to_kh(r_a0[l], 0)[None])
        rr_p, rk_p, rv_p, wp_p, ap_p, g_p, sh_p = _rwkv_prep(
            x, g_mix, w_rw, b_rw, zsh, *prep_args, row0=0, batch=bp, seq=tp,
            name="rwkv_prep_prompt")
        rr_s, rk_s, rv_s, wp_s, ap_s, g_s, sh_s = _rwkv_prep(
            x, g_mix, w_rw, b_rw, _sections(state_rwkv_shift[l], 1, _hk_to_kh), *prep_args,
            row0=n_p, batch=bs, seq=ts, name="rwkv_prep_sample")
        sh_p = _sections(sh_p, 1, _kh_to_hk)
        sh_s = _sections(sh_s, 1, _kh_to_hk)
        scan_args = (r_k_k[l], r_k_a[l], r_r_k[l], r_gn_g[l], r_gn_b[l])
        y_p, s_p = _rwkv_scan(rr_p, rk_p, rv_p, wp_p, ap_p, *scan_args, zs,
                              batch=bp, seq=tp, name="rwkv_scan_prompt")
        y_s, s_s = _rwkv_scan(rr_s, rk_s, rv_s, wp_s, ap_s, *scan_args, state_rwkv_S[l],
                              batch=bs, seq=ts, name="rwkv_scan_sample")

        w_branches = (w_br_m[l].astype(BF16), w_br_r_kh, w_out[l].astype(BF16))
        x_mid = _mix_out(hm, y_p, g_p, p_mix, x, *w_branches, None, row0=0, name="mix_out_prompt")
        x_mid = _mix_out(hm, y_s, g_s, p_mix, x, *w_branches, x_mid, row0=n_p,
                         name="mix_out_sample")

        ffn_w = (conv_w[l], conv_b[l][None], w_down[l].astype(BF16))
        g_ffn = norm_ffn_g[l][None]
        wu = w_up[l].astype(BF16)
        x, cv_p = _ffn(x_mid, g_ffn, wu, zcv, *ffn_w, None, row0=0, batch=bp, seq=tp,
                       name="ffn_prompt")
        x, cv_s = _ffn(x_mid, g_ffn, wu, state_ffn_conv[l], *ffn_w, x, row0=n_p, batch=bs, seq=ts,
                       name="ffn_sample")

        for lst, val in zip(outs_p, (nn_p.reshape(bp, M_HEADS, M_DK), m_p.reshape(bp, M_HEADS),
                                     s_p, sh_p, cv_p)):
            lst.append(val)
        for lst, val in zip(outs_s, (nn_s.reshape(bs, M_HEADS, M_DK), m_s.reshape(bs, M_HEADS),
                                     s_s, sh_s, cv_s)):
            lst.append(val)

    g_fin = norm_final_g[None]
    y_prompt = _final_norm(x, g_fin, row0=0, n=n_p, name="final_norm_prompt").reshape(bp, tp, d)
    y_sample = _final_norm(x, g_fin, row0=n_p, n=n_s, name="final_norm_sample").reshape(bs, ts, d)
    return (y_prompt, y_sample, c_all_p, *[jnp.stack(o) for o in outs_p],
            c_all_s, *[jnp.stack(o) for o in outs_s])
```

```python
import functools

import jax
import jax.numpy as jnp
from jax import lax
from jax.experimental import pallas as pl
from jax.experimental.pallas import tpu as pltpu

F32 = jnp.float32
BF16 = jnp.bfloat16

M_HEADS = 4
M_DK = 128
M_DV = 256
M_CHUNK = 64
R_HEADS = 16
R_HD = 64
R_LORA = 256
RMS_EPS = 1e-6
GN_EPS = 64e-5

LANES = 128
SUBLANES = 8
VMEM_LIMIT_CAP = 56 * 1024 * 1024
HEADS_PER_LANE_GROUP = LANES // R_HEADS

HI = lax.Precision.HIGHEST
ANY_SPEC = pl.BlockSpec(memory_space=pl.ANY)


def _vmem_limit(block_bytes):
    return int(min(VMEM_LIMIT_CAP, max(16 * 1024 * 1024, 3 * block_bytes)))


def _nbytes(shape, dtype):
    n = 1
    for s in shape:
        n *= s
    return n * jnp.dtype(dtype).itemsize


def _pick_tile(n, candidates):
    for c in candidates:
        if n % c == 0:
            return c
    raise ValueError(f"no tile in {candidates} divides {n}")


def _sigmoid(x):
    return 1.0 / (1.0 + jnp.exp(-x))


def _softplus(x):
    return jnp.maximum(x, 0.0) + jnp.log(1.0 + jnp.exp(-jnp.abs(x)))


def _params(semantics, block_bytes):
    return pltpu.CompilerParams(dimension_semantics=semantics,
                                vmem_limit_bytes=_vmem_limit(block_bytes))


def _norm_matmul_kernel(x_ref, g_ref, w_ref, b_ref, o_ref, h_ref):
    @pl.when(pl.program_id(1) == 0)
    def _():
        x = x_ref[...]
        ms = jnp.mean(x * x, axis=-1, keepdims=True)
        h_ref[...] = (x * lax.rsqrt(ms + RMS_EPS) * g_ref[...]).astype(BF16)

    o_ref[...] = jnp.dot(h_ref[...], w_ref[...], preferred_element_type=F32) + b_ref[...]


def _norm_matmul(x, g, w, b, *, tn, name):
    n, d = x.shape
    c = w.shape[1]
    tm = _pick_tile(n, (1024, 512, 256, 128, 64, 32, 16, 8))
    blocks = (_nbytes((tm, d), F32) + _nbytes((d, tn), BF16) + _nbytes((tm, tn), F32)
              + _nbytes((tm, d), BF16))
    return pl.pallas_call(
        _norm_matmul_kernel,
        name=name,
        grid=(n // tm, c // tn),
        in_specs=[
            pl.BlockSpec((tm, d), lambda i, j: (i, 0)),
            pl.BlockSpec((1, d), lambda i, j: (0, 0)),
            pl.BlockSpec((d, tn), lambda i, j: (0, j)),
            pl.BlockSpec((1, tn), lambda i, j: (0, j)),
        ],
        out_specs=pl.BlockSpec((tm, tn), lambda i, j: (i, j)),
        out_shape=jax.ShapeDtypeStruct((n, c), F32),
        scratch_shapes=[pltpu.VMEM((tm, d), BF16)],
        compiler_params=_params(("arbitrary", "arbitrary"), blocks),
    )(x, g, w, b)


def _mlstm_kernel(*refs, chunk, nseq):
    (q_ref, k_ref, v_ref, o_ref, gt_ref, c0_ref, n0_ref, m0_ref, gn_ref) = refs[:9]
    h_ref, c1_ref, n1_ref, m1_ref, c_sc, n_sc, m_sc = refs[-7:]
    L = chunk
    ci = pl.program_id(1)

    @pl.when(ci == 0)
    def _():
        c_sc[...] = c0_ref[0]
        n_sc[...] = n0_ref[...]
        m_sc[...] = m0_ref[...]

    lane = lax.broadcasted_iota(jnp.int32, (1, LANES), 1)
    row = lax.broadcasted_iota(jnp.int32, (L, L), 0)
    col = lax.broadcasted_iota(jnp.int32, (L, L), 1)
    causal = row >= col
    tri = causal.astype(F32)
    eye8 = (lax.broadcasted_iota(jnp.int32, (SUBLANES, LANES), 0)
            == lax.broadcasted_iota(jnp.int32, (SUBLANES, LANES), 1)).astype(F32)

    for s in range(nseq):
        r0 = s * L
        gates = gt_ref[r0:r0 + L, :]
        b_all = jnp.dot(tri, -_softplus(-gates), precision=HI, preferred_element_type=F32)
        z = jnp.where(lane < M_HEADS, gates, b_all)
        z_rows = lax.dot_general(eye8, z, (((1,), (1,)), ((), ())), precision=HI,
                                 preferred_element_type=F32)
        for h in range(M_HEADS):
            q = q_ref[r0:r0 + L, h * M_DK:(h + 1) * M_DK] * (M_DK ** -0.5)
            k = k_ref[r0:r0 + L, h * M_DK:(h + 1) * M_DK]
            v = v_ref[r0:r0 + L, h * M_DV:(h + 1) * M_DV]
            i_col = z[:, h:h + 1]
            b_col = z[:, M_HEADS + h:M_HEADS + h + 1]
            i_row = z_rows[h:h + 1, :]
            b_row = z_rows[M_HEADS + h:M_HEADS + h + 1, :]
            m_prev = m_sc[s, h]
            c_prev = c_sc[s, h]
            n_prev = n_sc[s, h]

            dmat = jnp.where(causal, b_col - b_row + i_row, -jnp.inf)
            m_inter = b_col + m_prev
            m_t = jnp.maximum(m_inter, jnp.max(dmat, axis=-1, keepdims=True))
            sc = lax.dot_general(q, k, (((1,), (1,)), ((), ())), preferred_element_type=F32)
            sc = sc * jnp.exp(dmat - m_t)
            inter = jnp.exp(m_inter - m_t)
            qc = lax.dot_general(q, c_prev, (((1,), (1,)), ((), ())), preferred_element_type=F32)
            num = jnp.dot(sc, v, preferred_element_type=F32) + inter * qc
            den = (jnp.sum(sc, axis=-1, keepdims=True)
                   + inter * jnp.sum(q * n_prev, axis=-1, keepdims=True))
            hh = num / jnp.maximum(jnp.abs(den), jnp.exp(-m_t))

            m_new = m_t[L - 1:L, :]
            b_last = b_col[L - 1:L, :]
            w_s = jnp.exp(b_last - b_col + i_col - m_new)
            decay = jnp.exp(b_last + m_prev - m_new)
            c_sc[s, h] = decay * c_prev + lax.dot_general(
                v * w_s, k, (((0,), (0,)), ((), ())), preferred_element_type=F32)
            n_sc[s, h] = decay * n_prev + jnp.sum(w_s * k, axis=0, keepdims=True)
            m_sc[s, h] = m_new

            hn = (hh * lax.rsqrt(jnp.mean(hh * hh, axis=-1, keepdims=True) + RMS_EPS)
                  * gn_ref[:, h * M_DV:(h + 1) * M_DV])
            h_ref[r0:r0 + L, h * M_DV:(h + 1) * M_DV] = hn * _sigmoid(
                o_ref[r0:r0 + L, h * M_DV:(h + 1) * M_DV])

    @pl.when(ci == pl.num_programs(1) - 1)
    def _():
        c1_ref[0] = c_sc[...]
        n1_ref[...] = n_sc[...]
        m1_ref[...] = m_sc[...]


def _mlstm(p_mix, gates, c0_all, n0, m0, gn, hm_buf, c_buf, *, layer, depth, row0, batch, seq, name):
    n_all = p_mix.shape[0]
    layer_in = min(layer, c0_all.shape[0] - 1)
    L = M_CHUNK if seq % M_CHUNK == 0 else seq
    nc = seq // L
    nseq = 1 if nc > 1 else _pick_tile(batch, (4, 2, 1))
    rows = nseq * L
    rb0 = row0 // rows
    assert row0 % rows == 0
    qk_w = M_HEADS * M_DK
    v_w = M_HEADS * M_DV
    assert qk_w * 2 == v_w

    def tok(bi, c):
        return rb0 + bi * nc + c

    state_map = lambda bi, c: (bi, 0, 0, 0)
    c_blk = (1, nseq, M_HEADS, M_DV, M_DK)
    in_specs = [
        pl.BlockSpec((rows, qk_w), lambda bi, c: (tok(bi, c), 0)),
        pl.BlockSpec((rows, qk_w), lambda bi, c: (tok(bi, c), 1)),
        pl.BlockSpec((rows, v_w), lambda bi, c: (tok(bi, c), 1)),
        pl.BlockSpec((rows, v_w), lambda bi, c: (tok(bi, c), 2)),
        pl.BlockSpec((rows, LANES), lambda bi, c: (tok(bi, c), 0)),
        pl.BlockSpec(c_blk, lambda bi, c: (layer_in, bi, 0, 0, 0)),
        pl.BlockSpec((nseq, M_HEADS, 1, M_DK), state_map),
        pl.BlockSpec((nseq, M_HEADS, 1, 1), state_map),
        pl.BlockSpec((1, v_w), lambda bi, c: (0, 0)),
    ]
    args = [p_mix, p_mix, p_mix, p_mix, gates, c0_all, n0.reshape(batch, M_HEADS, 1, M_DK),
            m0.reshape(batch, M_HEADS, 1, 1), gn]
    aliases = {}
    for buf, out_idx in ((hm_buf, 0), (c_buf, 1)):
        if buf is not None:
            aliases[len(args)] = out_idx
            in_specs.append(ANY_SPEC)
            args.append(buf)
    blocks = (2 * _nbytes((rows, qk_w), F32) + 3 * _nbytes((rows, v_w), F32)
              + 3 * _nbytes((nseq, M_HEADS, M_DV, M_DK), F32))
    return pl.pallas_call(
        functools.partial(_mlstm_kernel, chunk=L, nseq=nseq),
        name=name,
        grid=(batch // nseq, nc),
        in_specs=in_specs,
        out_specs=[
            pl.BlockSpec((rows, v_w), lambda bi, c: (tok(bi, c), 0)),
            pl.BlockSpec(c_blk, lambda bi, c: (layer, bi, 0, 0, 0)),
            pl.BlockSpec((nseq, M_HEADS, 1, M_DK), state_map),
            pl.BlockSpec((nseq, M_HEADS, 1, 1), state_map),
        ],
        out_shape=[
            jax.ShapeDtypeStruct((n_all, v_w), F32),
            jax.ShapeDtypeStruct((depth, batch, M_HEADS, M_DV, M_DK), F32),
            jax.ShapeDtypeStruct((batch, M_HEADS, 1, M_DK), F32),
            jax.ShapeDtypeStruct((batch, M_HEADS, 1, 1), F32),
        ],
        scratch_shapes=[
            pltpu.VMEM((nseq, M_HEADS, M_DV, M_DK), F32),
            pltpu.VMEM((nseq, M_HEADS, 1, M_DK), F32),
            pltpu.VMEM((nseq, M_HEADS, 1, 1), F32),
        ],
        input_output_aliases=aliases,
        compiler_params=_params(("arbitrary", "arbitrary"), blocks),
    )(*args)


def _seq_rows(x, offset, *, seq):
    rows = x.shape[0]
    i = lax.broadcasted_iota(jnp.int32, (rows // seq, rows), 0)
    r = lax.broadcasted_iota(jnp.int32, (rows // seq, rows), 1)
    pick = (r == i * seq + offset).astype(F32)
    return jnp.dot(pick, x, precision=HI, preferred_element_type=F32)


def _seq_place(per_seq, offset, *, seq):
    nseq = per_seq.shape[0]
    r = lax.broadcasted_iota(jnp.int32, (nseq * seq, nseq), 0)
    i = lax.broadcasted_iota(jnp.int32, (nseq * seq, nseq), 1)
    pick = (r == i * seq + offset).astype(F32)
    return jnp.dot(pick, per_seq, precision=HI, preferred_element_type=F32)


def _shift_rows(x, shift, *, seq, tile_has_whole_seqs, before):
    rows = x.shape[0]
    rolled = pltpu.roll(x, shift, 0)
    r = lax.broadcasted_iota(jnp.int32, (rows, 1), 0)
    if tile_has_whole_seqs:
        head = _seq_place(before[0], 0, seq=seq)
        for s in range(1, shift):
            head = head + _seq_place(before[s], s, seq=seq)
        return jnp.where((r % seq) < shift, head, rolled)
    out = rolled
    for s in range(shift):
        out = jnp.where(r == s, before[s:s + 1, :], out)
    return out


def _hk_to_kh(a, axis):
    shape = a.shape
    a = a.reshape(shape[:axis] + (R_HEADS, R_HD) + shape[axis + 1:])
    return jnp.swapaxes(a, axis, axis + 1).reshape(shape)


def _kh_to_hk(a, axis):
    shape = a.shape
    a = a.reshape(shape[:axis] + (R_HD, R_HEADS) + shape[axis + 1:])
    return jnp.swapaxes(a, axis, axis + 1).reshape(shape)


def _sections(a, axis, fn):
    rw = R_HEADS * R_HD
    idx = lambda lo, hi: tuple(slice(lo, hi) if d == axis else slice(None) for d in range(a.ndim))
    parts = [fn(a[idx(i * rw, (i + 1) * rw)], axis) for i in range(3)]
    if a.shape[axis] > 3 * rw:
        parts.append(a[idx(3 * rw, a.shape[axis])])
    return jnp.concatenate(parts, axis=axis)


def _rwkv_prep_kernel(x_ref, gn_ref, wp_ref, bp_ref, st_ref, mu_ref, wl_ref, w0_ref, a0_ref,
                      rr_ref, rk_ref, rv_ref, dec_ref, a_ref, g_ref, sh_ref, carry_sc,
                      *, seq, whole_seqs):
    xin = x_ref[...]
    tb = xin.shape[0]
    rw = R_HEADS * R_HD
    ms = jnp.mean(xin * xin, axis=-1, keepdims=True)
    h = (xin * lax.rsqrt(ms + RMS_EPS) * gn_ref[...]).astype(BF16)
    if not whole_seqs:
        @pl.when(pl.program_id(1) == 0)
        def _():
            carry_sc[...] = st_ref[0]

    def proj(lo, hi):
        return jnp.dot(h, wp_ref[:, lo:hi], preferred_element_type=F32) + bp_ref[:, lo:hi]

    def shifted(x, lo, hi):
        if whole_seqs:
            prev = _shift_rows(x, 1, seq=seq, tile_has_whole_seqs=True, before=[st_ref[:, lo:hi]])
            sh_ref[:, lo:hi] = _seq_rows(x, seq - 1, seq=seq)
        else:
            prev = _shift_rows(x, 1, seq=seq, tile_has_whole_seqs=False,
                               before=carry_sc[:, lo:hi])
            carry_sc[:, lo:hi] = x[tb - 1:tb, :]
            sh_ref[0, :, lo:hi] = x[tb - 1:tb, :]
        return x + (prev - x) * mu_ref[:, lo:hi]

    x_lora = proj(3 * rw, 3 * rw + R_LORA)
    x_r = proj(0, rw)
    z = shifted(x_lora, 3 * rw, 3 * rw + R_LORA)
    lane = lax.broadcasted_iota(jnp.int32, (1, R_LORA), 1)
    act = jnp.where(lane < 64, jnp.tanh(z), jnp.where(lane < 128, z, _sigmoid(z))).astype(BF16)
    lo_w = jnp.dot(act, wl_ref[:, 0:rw], preferred_element_type=F32)
    x_k = proj(rw, 2 * rw)
    rr_ref[...] = shifted(x_r, 0, rw)
    lo_a = jnp.dot(act, wl_ref[:, rw:2 * rw], preferred_element_type=F32)
    x_v = proj(2 * rw, 3 * rw)
    rk_ref[...] = shifted(x_k, rw, 2 * rw)
    dec_ref[...] = jnp.exp(-jnp.exp(-_softplus(-(w0_ref[...] + lo_w)) - 0.5))
    g_ref[...] = jnp.dot(act, wl_ref[:, 2 * rw:3 * rw], preferred_element_type=F32)
    rv_ref[...] = shifted(x_v, 2 * rw, 3 * rw)
    a_ref[...] = _sigmoid(a0_ref[...] + lo_a)


def _rwkv_prep(x, gn, w_rw, b_rw, shift_state, mu, w_lora, w0, a0, *, row0, batch, seq, name):
    d = x.shape[1]
    c = w_rw.shape[1]
    rw = R_HEADS * R_HD
    n_tok = batch * seq
    whole = seq <= 64
    if whole:
        tb = _pick_tile(n_tok, (256, 128, 64, 32, 16, 8))
        assert tb % seq == 0 and row0 % tb == 0
        grid = (n_tok // tb, 1)
        rb0 = row0 // tb
        tok_map = lambda i, j: (rb0 + i, 0)
        out_map = lambda i, j: (i, 0)
        state = shift_state
        st_spec = pl.BlockSpec((tb // seq, c), out_map)
        sh_shape = jax.ShapeDtypeStruct((batch, c), F32)
    else:
        tb = _pick_tile(seq, (256, 128, 64, 32, 16, 8))
        nt = seq // tb
        grid = (batch, nt)
        rb0 = row0 // tb
        tok_map = lambda b, j: (rb0 + b * nt + j, 0)
        out_map = lambda b, j: (b * nt + j, 0)
        state = shift_state.reshape(batch, 1, c)
        st_spec = pl.BlockSpec((1, 1, c), lambda b, j: (b, 0, 0))
        sh_shape = jax.ShapeDtypeStruct((batch, 1, c), F32)
    fixed = lambda i, j: (0, 0)
    tok_out = jax.ShapeDtypeStruct((n_tok, rw), F32)
    tok_spec = pl.BlockSpec((tb, rw), out_map)
    blocks = (4 * _nbytes((tb, c), F32) + 6 * _nbytes((tb, rw), F32) + _nbytes(w_lora.shape, BF16)
              + _nbytes(w_rw.shape, BF16))
    outs = pl.pallas_call(
        functools.partial(_rwkv_prep_kernel, seq=seq, whole_seqs=whole),
        name=name,
        grid=grid,
        in_specs=[
            pl.BlockSpec((tb, d), tok_map),
            pl.BlockSpec((1, d), fixed),
            pl.BlockSpec(w_rw.shape, fixed),
            pl.BlockSpec((1, c), fixed),
            st_spec,
            pl.BlockSpec((1, c), fixed),
            pl.BlockSpec(w_lora.shape, fixed),
            pl.BlockSpec((1, rw), fixed),
            pl.BlockSpec((1, rw), fixed),
        ],
        out_specs=[tok_spec] * 6 + [st_spec],
        out_shape=[tok_out] * 6 + [sh_shape],
        scratch_shapes=[pltpu.VMEM((1, c), F32)],
        compiler_params=_params(("arbitrary", "arbitrary"), blocks),
    )(x, gn, w_rw, b_rw, state, mu, w_lora, w0, a0)
    return (*outs[:6], outs[6].reshape(batch, c))


def _rwkv_scan_kernel(rr_ref, rk_ref, dec_ref, a_ref, rv_ref, kks_ref, ka_ref, rrk_ref,
                      gng_ref, gnb_ref, s0_ref, y_ref, s1_ref,
                      s_sc, w_sc, kka_sc, km_sc, nkk_sc, wr_sc, coef_sc, *, tb):
    j = pl.program_id(1)
    nk = R_HD

    @pl.when(j == 0)
    def _():
        s_sc[...] = s0_ref[0]

    rr = rr_ref[...]
    rk = rk_ref[...]
    a = a_ref[...]
    dec = dec_ref[...]
    kk = rk * kks_ref[...]
    nrm = jnp.sqrt(jnp.sum(kk * kk, axis=0, keepdims=True))
    kk = kk / jnp.maximum(nrm, 1e-12)
    kmod = rk * (1.0 + (a - 1.0) * ka_ref[...])
    kka = kk * a
    g2 = dec.reshape(nk * tb, LANES)
    tpos = lax.broadcasted_iota(jnp.int32, (nk * tb, 1), 0) % tb
    shift = 1
    while shift < tb:
        g2 = g2 * jnp.where(tpos >= shift, pltpu.roll(g2, shift, 0), 1.0)
        shift *= 2
    g_prev = jnp.where(tpos >= 1, pltpu.roll(g2, 1, 0), 1.0).reshape(nk, tb, LANES)
    g_run = g2.reshape(nk, tb, LANES)
    g_inv = 1.0 / g_run
    w_sc[...] = g_run
    kka_sc[...] = kka * g_inv
    km_sc[...] = kmod * g_inv
    nkk_sc[...] = -kk * g_prev
    wr_sc[...] = rr * g_run
    coef_sc[0] = jnp.sum(kka * rr, axis=0)
    coef_sc[1] = jnp.sum(kmod * rr, axis=0)
    coef_sc[2] = jnp.sum(rr * kmod * rrk_ref[...], axis=0)

    def row(ref, t, k):
        return jnp.broadcast_to(ref[k, pl.ds(t, 1), :], (nk, LANES))

    def coef(i, t):
        return jnp.broadcast_to(coef_sc[i, pl.ds(t, 1), :], (nk, LANES))

    def matvecs(t):
        acc_sa = jnp.zeros((nk, LANES), F32)
        acc_y = jnp.zeros((nk, LANES), F32)
        for k in range(nk):
            sk = s_sc[k]
            acc_sa = acc_sa + sk * row(nkk_sc, t, k)
            acc_y = acc_y + sk * row(wr_sc, t, k)
        return acc_sa, acc_y

    def emit(t, sa, y0):
        val = rv_ref[t]
        y = y0 + sa * coef(0, t) + val * coef(1, t)
        mean = jnp.sum(y, axis=0, keepdims=True) * (1.0 / nk)
        d = y - mean
        var = jnp.sum(d * d, axis=0, keepdims=True) * (1.0 / nk)
        yn = d * lax.rsqrt(var + GN_EPS) * gng_ref[...] + gnb_ref[...]
        y_ref[t] = yn + coef(2, t) * val
        return val

    def step(t, carry):
        sa, y0 = carry
        val = emit(t, sa, y0)
        acc_sa = jnp.zeros((nk, LANES), F32)
        acc_y = jnp.zeros((nk, LANES), F32)
        for k in range(nk):
            sk = s_sc[k] + sa * row(kka_sc, t, k) + val * row(km_sc, t, k)
            s_sc[k] = sk
            acc_sa = acc_sa + sk * row(nkk_sc, t + 1, k)
            acc_y = acc_y + sk * row(wr_sc, t + 1, k)
        return acc_sa, acc_y

    sa, y0 = lax.fori_loop(0, tb - 1, step, matvecs(0))
    t_last = tb - 1
    val = emit(t_last, sa, y0)
    for k in range(nk):
        s_sc[k] = ((s_sc[k] + sa * row(kka_sc, t_last, k) + val * row(km_sc, t_last, k))
                   * row(w_sc, t_last, k))

    @pl.when(j == pl.num_programs(1) - 1)
    def _():
        s1_ref[0] = s_sc[...]


LAYOUT_TB = 128


def _lane_slabs(in_ref, p):
    nb = HEADS_PER_LANE_GROUP
    zts = [in_ref[b, :, p * LANES:(p + 1) * LANES].T for b in range(nb)]
    for i in range(SUBLANES):
        d = jnp.concatenate([zt[i * R_HEADS:(i + 1) * R_HEADS, :] for zt in zts], axis=0)
        yield SUBLANES * p + i, d.T


def _lanes_in_kernel(src_ref, dst_ref, *, rows):
    for p in range(R_HD // SUBLANES):
        for i, slab in _lane_slabs(src_ref, p):
            if rows:
                dst_ref[i] = slab
            else:
                dst_ref[:, pl.ds(i, 1), :] = slab[:, None, :]


def _lanes_in(x, *, batch, seq, rows):
    nb = HEADS_PER_LANE_GROUP
    g = batch // nb
    w = R_HEADS * R_HD
    if seq % LAYOUT_TB != 0 and (g * seq) % LAYOUT_TB == 0:
        x = x.reshape(g, nb, seq, w).transpose(1, 0, 2, 3).reshape(nb * g * seq, w)
        return _lanes_in(x, batch=nb, seq=g * seq, rows=rows)
    if seq % LAYOUT_TB != 0:
        x = x.reshape(g, nb, seq, R_HD, R_HEADS)
        if rows:
            return x.transpose(3, 0, 2, 1, 4).reshape(R_HD, g * seq, LANES)
        return x.transpose(0, 2, 3, 1, 4).reshape(g * seq, R_HD, LANES)
    nt = seq // LAYOUT_TB
    if rows:
        out_spec = pl.BlockSpec((R_HD, LAYOUT_TB, LANES), lambda gi, j: (0, gi * nt + j, 0))
        out_shape = jax.ShapeDtypeStruct((R_HD, g * seq, LANES), F32)
    else:
        out_spec = pl.BlockSpec((LAYOUT_TB, R_HD, LANES), lambda gi, j: (gi * nt + j, 0, 0))
        out_shape = jax.ShapeDtypeStruct((g * seq, R_HD, LANES), F32)
    return pl.pallas_call(
        functools.partial(_lanes_in_kernel, rows=rows),
        name="lanes_in_rows" if rows else "lanes_in_full",
        grid=(g, nt),
        in_specs=[pl.BlockSpec((nb, LAYOUT_TB, w), lambda gi, j: (gi, j, 0))],
        out_specs=out_spec,
        out_shape=out_shape,
        compiler_params=_params(("arbitrary", "arbitrary"), 2 * _nbytes((nb, LAYOUT_TB, w), F32)),
    )(x.reshape(batch, seq, w))


def _lanes_out_kernel(y_ref, o_ref):
    nb = HEADS_PER_LANE_GROUP
    for p in range(R_HD // SUBLANES):
        pieces = [[] for _ in range(nb)]
        for i in range(SUBLANES):
            st = y_ref[:, SUBLANES * p + i, :].T
            for b in range(nb):
                pieces[b].append(st[b * R_HEADS:(b + 1) * R_HEADS, :])
        for b in range(nb):
            zt = jnp.concatenate(pieces[b], axis=0)
            o_ref[b, :, p * LANES:(p + 1) * LANES] = zt.T


def _lanes_out(y, *, batch, seq):
    nb = HEADS_PER_LANE_GROUP
    g = batch // nb
    w = R_HEADS * R_HD
    if seq % LAYOUT_TB != 0 and (g * seq) % LAYOUT_TB == 0:
        tok = _lanes_out(y, batch=nb, seq=g * seq)
        return tok.reshape(nb, g, seq, w).transpose(1, 0, 2, 3).reshape(batch * seq, w)
    if seq % LAYOUT_TB != 0:
        y = y.reshape(g, seq, R_HD, nb, R_HEADS)
        return y.transpose(0, 3, 1, 2, 4).reshape(batch * seq, w)
    nt = seq // LAYOUT_TB
    out = pl.pallas_call(
        _lanes_out_kernel,
        name="lanes_out",
        grid=(g, nt),
        in_specs=[pl.BlockSpec((LAYOUT_TB, R_HD, LANES), lambda gi, j: (gi * nt + j, 0, 0))],
        out_specs=pl.BlockSpec((nb, LAYOUT_TB, w), lambda gi, j: (gi, j, 0)),
        out_shape=jax.ShapeDtypeStruct((batch, seq, w), F32),
        compiler_params=_params(("arbitrary", "arbitrary"), 2 * _nbytes((nb, LAYOUT_TB, w), F32)),
    )(y)
    return out.reshape(batch * seq, w)


def _param_lanes(p):
    p = p.reshape(R_HEADS, R_HD).T
    return jnp.tile(p, (1, HEADS_PER_LANE_GROUP))


def _rwkv_scan(rr, rk, rv, dec, a, kks, ka, rrk, gng, gnb, s0, *, batch, seq, name):
    assert batch % HEADS_PER_LANE_GROUP == 0
    g = batch // HEADS_PER_LANE_GROUP
    tb = _pick_tile(seq, (32, 16, 8))
    nt = seq // tb
    rows = [_lanes_in(v, batch=batch, seq=seq, rows=True) for v in (rr, rk, dec, a)]
    val = _lanes_in(rv, batch=batch, seq=seq, rows=False)
    s0l = s0.reshape(g, HEADS_PER_LANE_GROUP, R_HEADS, R_HD, R_HD).transpose(0, 4, 3, 1, 2)
    s0l = s0l.reshape(g, R_HD, R_HD, LANES)
    row_spec = pl.BlockSpec((R_HD, tb, LANES), lambda gi, j: (0, gi * nt + j, 0))
    val_spec = pl.BlockSpec((tb, R_HD, LANES), lambda gi, j: (gi * nt + j, 0, 0))
    kpar_spec = pl.BlockSpec((R_HD, 1, LANES), lambda gi, j: (0, 0, 0))
    vpar_spec = pl.BlockSpec((R_HD, LANES), lambda gi, j: (0, 0))
    st_spec = pl.BlockSpec((1, R_HD, R_HD, LANES), lambda gi, j: (gi, 0, 0, 0))
    vec_bytes = _nbytes((tb, R_HD, LANES), F32)
    blocks = 11 * vec_bytes + 3 * _nbytes((R_HD, R_HD, LANES), F32)
    kpar = lambda p: _param_lanes(p)[:, None, :]
    y, s1l = pl.pallas_call(
        functools.partial(_rwkv_scan_kernel, tb=tb),
        name=name,
        grid=(g, nt),
        in_specs=[row_spec] * 4 + [val_spec] + [kpar_spec] * 3 + [vpar_spec] * 2 + [st_spec],
        out_specs=[val_spec, st_spec],
        out_shape=[
            jax.ShapeDtypeStruct((g * seq, R_HD, LANES), F32),
            jax.ShapeDtypeStruct(s0l.shape, F32),
        ],
        scratch_shapes=[pltpu.VMEM((R_HD, R_HD, LANES), F32)]
        + [pltpu.VMEM((R_HD, tb, LANES), F32)] * 5
        + [pltpu.VMEM((3, tb, LANES), F32)],
        compiler_params=_params(("arbitrary", "arbitrary"), blocks),
    )(*rows, val, kpar(kks), kpar(ka), kpar(rrk), _param_lanes(gng), _param_lanes(gnb), s0l)
    s1 = s1l.reshape(g, R_HD, R_HD, HEADS_PER_LANE_GROUP, R_HEADS).transpose(0, 3, 4, 2, 1)
    return _lanes_out(y, batch=batch, seq=seq), s1.reshape(batch, R_HEADS, R_HD, R_HD)


def _mix_out_kernel(hm_ref, yr_ref, g_ref, gm_ref, gr_ref, x_ref, wm_ref, wr_ref, wo_ref, *rest):
    o_ref = rest[-1]
    bm = jnp.dot(hm_ref[...].astype(BF16), wm_ref[...], preferred_element_type=F32)
    yr = (yr_ref[...] * g_ref[...]).astype(BF16)
    br = jnp.dot(yr, wr_ref[...], preferred_element_type=F32)
    merged = _sigmoid(gm_ref[...]) * bm + _sigmoid(gr_ref[...]) * br
    o_ref[...] = x_ref[...] + jnp.dot(merged.astype(BF16), wo_ref[...], preferred_element_type=F32)


def _mix_out(hm, yr, g, p_mix, x, wm, wr, wo, out_buf, *, row0, name):
    n_all, d = x.shape
    n = yr.shape[0]
    tm = _pick_tile(n, (256, 128, 64, 32, 16, 8))
    assert row0 % tm == 0
    rb0 = row0 // tm
    gm_blk = (2 * M_HEADS * (M_DK + M_DV)) // d
    shared = lambda i: (rb0 + i, 0)
    own = lambda i: (i, 0)
    fixed = lambda i: (0, 0)
    aliased = out_buf is not None
    in_specs = [
        pl.BlockSpec((tm, wm.shape[0]), shared),
        pl.BlockSpec((tm, wr.shape[0]), own),
        pl.BlockSpec((tm, wr.shape[0]), own),
        pl.BlockSpec((tm, d), lambda i: (rb0 + i, gm_blk)),
        pl.BlockSpec((tm, d), lambda i: (rb0 + i, gm_blk + 1)),
        pl.BlockSpec((tm, d), shared),
        pl.BlockSpec(wm.shape, fixed),
        pl.BlockSpec(wr.shape, fixed),
        pl.BlockSpec(wo.shape, fixed),
    ]
    args = [hm, yr, g, p_mix, p_mix, x, wm, wr, wo]
    if aliased:
        in_specs.append(ANY_SPEC)
        args.append(out_buf)
    blocks = 7 * _nbytes((tm, d), F32) + 3 * _nbytes((d, d), BF16)
    return pl.pallas_call(
        _mix_out_kernel,
        name=name,
        grid=(n // tm,),
        in_specs=in_specs,
        out_specs=pl.BlockSpec((tm, d), shared),
        out_shape=jax.ShapeDtypeStruct((n_all, d), F32),
        input_output_aliases={9: 0} if aliased else {},
        compiler_params=_params(("arbitrary",), blocks),
    )(*args)


FFN_CHUNK = 256


def _ffn_kernel(x_ref, gn_ref, wu_ref, cs_ref, cw_ref, cb_ref, wd_ref, *rest, seq, whole_seqs):
    o_ref, st_ref, carry_sc = rest[-3:]
    x = x_ref[...]
    tb = x.shape[0]
    dff = wd_ref.shape[0]
    ms = jnp.mean(x * x, axis=-1, keepdims=True)
    h = (x * lax.rsqrt(ms + RMS_EPS) * gn_ref[...]).astype(BF16)
    if not whole_seqs:
        @pl.when(pl.program_id(1) == 0)
        def _():
            carry_sc[...] = cs_ref[0]
    def up(c0):
        return [jnp.dot(h, wu_ref[:, lo:lo + FFN_CHUNK], preferred_element_type=F32)
                for lo in (c0, dff + c0)]

    acc = x
    starts = list(range(0, dff, FFN_CHUNK))
    u_next = up(starts[0])
    for ci, c0 in enumerate(starts):
        u_cur = u_next
        if ci + 1 < len(starts):
            u_next = up(starts[ci + 1])
        halves = []
        for lo, u in zip((c0, dff + c0), u_cur):
            cols = slice(lo, lo + FFN_CHUNK)
            if whole_seqs:
                before = [cs_ref[0, :, cols], cs_ref[1, :, cols]]
                p1 = _shift_rows(u, 1, seq=seq, tile_has_whole_seqs=True, before=before[1:])
                p2 = _shift_rows(u, 2, seq=seq, tile_has_whole_seqs=True, before=before)
                st_ref[0, :, cols] = _seq_rows(u, seq - 2, seq=seq)
                st_ref[1, :, cols] = _seq_rows(u, seq - 1, seq=seq)
            else:
                carry = carry_sc[:, cols]
                p1 = _shift_rows(u, 1, seq=seq, tile_has_whole_seqs=False, before=carry[1:2, :])
                p2 = _shift_rows(u, 2, seq=seq, tile_has_whole_seqs=False, before=carry)
                carry_sc[:, cols] = u[tb - 2:tb, :]
                st_ref[0, :, cols] = u[tb - 2:tb, :]
            halves.append(cb_ref[:, cols] + p2 * cw_ref[0:1, cols] + p1 * cw_ref[1:2, cols]
                          + u * cw_ref[2:3, cols])
        gate, val = halves
        act = (gate * _sigmoid(gate) * val).astype(BF16)
        acc = acc + jnp.dot(act, wd_ref[c0:c0 + FFN_CHUNK, :], preferred_element_type=F32)
    o_ref[...] = acc


def _ffn(x, gn, wu, conv_state, cw, cb, wd, out_buf, *, row0, batch, seq, name):
    n_all, d = x.shape
    c = wu.shape[1]
    n_tok = batch * seq
    taps = cw.shape[0]
    assert taps == 3 and seq >= taps - 1 and wd.shape[0] % FFN_CHUNK == 0
    whole = seq <= 64
    if whole:
        tb = _pick_tile(n_tok, (128, 64, 32, 16, 8))
        assert tb % seq == 0 and row0 % tb == 0
        grid = (n_tok // tb, 1)
        rb0 = row0 // tb
        tok_map = lambda i, j: (rb0 + i, 0)
        state = jnp.swapaxes(conv_state, 0, 1)
        st_shape = jax.ShapeDtypeStruct((taps - 1, batch, c), F32)
        st_spec = pl.BlockSpec((taps - 1, tb // seq, c), lambda i, j: (0, i, 0))
    else:
        tb = _pick_tile(seq, (256, 128, 64, 32, 16, 8))
        nt = seq // tb
        grid = (batch, nt)
        rb0 = row0 // tb
        tok_map = lambda b, j: (rb0 + b * nt + j, 0)
        state = conv_state
        st_shape = jax.ShapeDtypeStruct((batch, taps - 1, c), F32)
        st_spec = pl.BlockSpec((1, taps - 1, c), lambda b, j: (b, 0, 0))
    fixed = lambda i, j: (0, 0)
    aliased = out_buf is not None
    in_specs = [
        pl.BlockSpec((tb, d), tok_map),
        pl.BlockSpec((1, d), fixed),
        pl.BlockSpec(wu.shape, fixed),
        st_spec,
        pl.BlockSpec(cw.shape, fixed),
        pl.BlockSpec((1, c), fixed),
        pl.BlockSpec(wd.shape, fixed),
    ]
    args = [x, gn, wu, state, cw, cb, wd]
    if aliased:
        in_specs.append(ANY_SPEC)
        args.append(out_buf)
    blocks = _nbytes(wu.shape, BF16) + _nbytes(wd.shape, BF16) + 6 * _nbytes((tb, d), F32)
    y, st = pl.pallas_call(
        functools.partial(_ffn_kernel, seq=seq, whole_seqs=whole),
        name=name,
        grid=grid,
        in_specs=in_specs,
        out_specs=[pl.BlockSpec((tb, d), tok_map), st_spec],
        out_shape=[jax.ShapeDtypeStruct((n_all, d), F32), st_shape],
        scratch_shapes=[pltpu.VMEM((taps - 1, c), F32)],
        input_output_aliases={7: 0} if aliased else {},
        compiler_params=_params(("arbitrary", "arbitrary"), blocks),
    )(*args)
    return y, (jnp.swapaxes(st, 0, 1) if whole else st)


def _final_norm_kernel(x_ref, g_ref, o_ref):
    x = x_ref[...]
    ms = jnp.mean(x * x, axis=-1, keepdims=True)
    o_ref[...] = x * lax.rsqrt(ms + RMS_EPS) * g_ref[...]


def _final_norm(x, g, *, row0, n, name):
    d = x.shape[1]
    tm = _pick_tile(n, (1024, 512, 256, 128, 64, 32, 16, 8))
    assert row0 % tm == 0
    rb0 = row0 // tm
    return pl.pallas_call(
        _final_norm_kernel,
        name=name,
        grid=(n // tm,),
        in_specs=[pl.BlockSpec((tm, d), lambda i: (rb0 + i, 0)),
                  pl.BlockSpec((1, d), lambda i: (0, 0))],
        out_specs=pl.BlockSpec((tm, d), lambda i: (i, 0)),
        out_shape=jax.ShapeDtypeStruct((n, d), F32),
        compiler_params=_params(("arbitrary",), 2 * _nbytes((tm, d), F32)),
    )(x, g)


def kernel(x_prompt, x_sample, state_mlstm_C, state_mlstm_n, state_mlstm_m, state_rwkv_S, state_rwkv_shift, state_ffn_conv, norm_mix_g, w_in, b_in, mu_shift, m_norm_g, r_w0, r_w_up, r_a0, r_a_up, r_g_up, r_k_k, r_k_a, r_r_k, r_gn_g, r_gn_b, w_br_m, w_br_r, w_out, norm_ffn_g, w_up, conv_w, conv_b, w_down, norm_final_g):
    bp, tp, d = x_prompt.shape
    bs, ts, _ = x_sample.shape
    depth = w_in.shape[0]
    n_p = bp * tp
    n_s = bs * ts
    rw = R_HEADS * R_HD
    qkvo = 2 * M_HEADS * (M_DK + M_DV)
    n_gate = 2 * M_HEADS
    mix_cols = qkvo + n_gate + 2 * d
    rw_cols = 3 * rw + R_LORA
    dff2 = w_up.shape[2]
    assert w_in.shape[2] == mix_cols + rw_cols

    x = jnp.concatenate([x_prompt.reshape(n_p, d), x_sample.reshape(n_s, d)], axis=0)

    zc = jnp.zeros((1, bp, M_HEADS, M_DV, M_DK), F32)
    c_all_p = c_all_s = None
    zn = jnp.zeros((bp, M_HEADS, M_DK), F32)
    zm = jnp.zeros((bp, M_HEADS), F32)
    zs = jnp.zeros((bp, R_HEADS, R_HD, R_HD), F32)
    zsh = jnp.zeros((bp, rw_cols), F32)
    zcv = jnp.zeros((bp, conv_w.shape[1] - 1, dff2), F32)

    outs_p = [[] for _ in range(5)]
    outs_s = [[] for _ in range(5)]
    for l in range(depth):
        wl = w_in[l]
        bl = b_in[l]
        w_mix = jnp.concatenate([wl[:, :qkvo], wl[:, qkvo + n_gate:mix_cols]], axis=1).astype(BF16)
        b_mix = jnp.concatenate([bl[:qkvo], bl[qkvo + n_gate:mix_cols]])[None]
        w_gate = jnp.pad(wl[:, qkvo:qkvo + n_gate], ((0, 0), (0, LANES - n_gate))).astype(BF16)
        b_gate = jnp.pad(bl[qkvo:qkvo + n_gate], (0, LANES - n_gate))[None]
        w_rw = _sections(wl[:, mix_cols:], 1, _hk_to_kh).astype(BF16)
        b_rw = _sections(bl[mix_cols:], 0, _hk_to_kh)[None]
        mu_rw = _sections(mu_shift[l], 0, _hk_to_kh)[None]
        w_lora = jnp.zeros((R_LORA, 3 * rw), F32)
        w_lora = w_lora.at[0:64, 0:rw].set(r_w_up[l]).at[64:128, rw:2 * rw].set(r_a_up[l])
        w_lora = _sections(w_lora.at[128:256, 2 * rw:].set(r_g_up[l]), 1, _hk_to_kh).astype(BF16)
        w_br_r_kh = _hk_to_kh(w_br_r[l], 0).astype(BF16)
        g_mix = norm_mix_g[l][None]

        p_mix = _norm_matmul(x, g_mix, w_mix, b_mix, tn=1024, name="in_proj_mix")
        gates = _norm_matmul(x, g_mix, w_gate, b_gate, tn=LANES, name="in_proj_gates")

        gn = m_norm_g[l].reshape(1, M_HEADS * M_DV)
        hm, c_all_p, nn_p, m_p = _mlstm(
            p_mix, gates, zc, zn, zm, gn, None, c_all_p, layer=l, depth=depth,
            row0=0, batch=bp, seq=tp, name="mlstm_prompt")
        hm, c_all_s, nn_s, m_s = _mlstm(
            p_mix, gates, state_mlstm_C, state_mlstm_n[l], state_mlstm_m[l], gn, hm, c_all_s,
            layer=l, depth=depth, row0=n_p, batch=bs, seq=ts, name="mlstm_sample")

        prep_args = (mu_rw, w_lora, _hk_to_kh(r_w0[l], 0)[None], _hk_to_kh(r_a0[l], 0)[None])
        rr_p, rk_p, rv_p, wp_p, ap_p, g_p, sh_p = _rwkv_prep(
            x, g_mix, w_rw, b_rw, zsh, *prep_args, row0=0, batch=bp, seq=tp,
            name="rwkv_prep_prompt")
        rr_s, rk_s, rv_s, wp_s, ap_s, g_s, sh_s = _rwkv_prep(
            x, g_mix, w_rw, b_rw, _sections(state_rwkv_shift[l], 1, _hk_to_kh), *prep_args,
            row0=n_p, batch=bs, seq=ts, name="rwkv_prep_sample")
        sh_p = _sections(sh_p, 1, _kh_to_hk)
        sh_s = _sections(sh_s, 1, _kh_to_hk)
        scan_args = (r_k_k[l], r_k_a[l], r_r_k[l], r_gn_g[l], r_gn_b[l])
        y_p, s_p = _rwkv_scan(rr_p, rk_p, rv_p, wp_p, ap_p, *scan_args, zs,
                              batch=bp, seq=tp, name="rwkv_scan_prompt")
        y_s, s_s = _rwkv_scan(rr_s, rk_s, rv_s, wp_s, ap_s, *scan_args, state_rwkv_S[l],
                              batch=bs, seq=ts, name="rwkv_scan_sample")

        w_branches = (w_br_m[l].astype(BF16), w_br_r_kh, w_out[l].astype(BF16))
        x_mid = _mix_out(hm, y_p, g_p, p_mix, x, *w_branches, None, row0=0, name="mix_out_prompt")
        x_mid = _mix_out(hm, y_s, g_s, p_mix, x, *w_branches, x_mid, row0=n_p,
                         name="mix_out_sample")

        ffn_w = (conv_w[l], conv_b[l][None], w_down[l].astype(BF16))
        g_ffn = norm_ffn_g[l][None]
        wu = w_up[l].astype(BF16)
        x, cv_p = _ffn(x_mid, g_ffn, wu, zcv, *ffn_w, None, row0=0, batch=bp, seq=tp,
                       name="ffn_prompt")
        x, cv_s = _ffn(x_mid, g_ffn, wu, state_ffn_conv[l], *ffn_w, x, row0=n_p, batch=bs, seq=ts,
                       name="ffn_sample")

        for lst, val in zip(outs_p, (nn_p.reshape(bp, M_HEADS, M_DK), m_p.reshape(bp, M_HEADS),
                                     s_p, sh_p, cv_p)):
            lst.append(val)
        for lst, val in zip(outs_s, (nn_s.reshape(bs, M_HEADS, M_DK), m_s.reshape(bs, M_HEADS),
                                     s_s, sh_s, cv_s)):
            lst.append(val)

    g_fin = norm_final_g[None]
    y_prompt = _final_norm(x, g_fin, row0=0, n=n_p, name="final_norm_prompt").reshape(bp, tp, d)
    y_sample = _final_norm(x, g_fin, row0=n_p, n=n_s, name="final_norm_sample").reshape(bs, ts, d)
    return (y_prompt, y_sample, c_all_p, *[jnp.stack(o) for o in outs_p],
            c_all_s, *[jnp.stack(o) for o in outs_s])
```

```python
import functools

import jax
import jax.numpy as jnp
from jax import lax
from jax.experimental import pallas as pl
from jax.experimental.pallas import tpu as pltpu

F32 = jnp.float32
BF16 = jnp.bfloat16

M_HEADS = 4
M_DK = 128
M_DV = 256
M_CHUNK = 64
R_HEADS = 16
R_HD = 64
R_LORA = 256
RMS_EPS = 1e-6
GN_EPS = 64e-5

LANES = 128
SUBLANES = 8
VMEM_LIMIT_CAP = 56 * 1024 * 1024
HEADS_PER_LANE_GROUP = LANES // R_HEADS

HI = lax.Precision.HIGHEST
ANY_SPEC = pl.BlockSpec(memory_space=pl.ANY)


def _vmem_limit(block_bytes):
    return int(min(VMEM_LIMIT_CAP, max(16 * 1024 * 1024, 3 * block_bytes)))


def _nbytes(shape, dtype):
    n = 1
    for s in shape:
        n *= s
    return n * jnp.dtype(dtype).itemsize


def _pick_tile(n, candidates):
    for c in candidates:
        if n % c == 0:
            return c
    raise ValueError(f"no tile in {candidates} divides {n}")


def _sigmoid(x):
    return 1.0 / (1.0 + jnp.exp(-x))


def _softplus(x):
    return jnp.maximum(x, 0.0) + jnp.log(1.0 + jnp.exp(-jnp.abs(x)))


def _params(semantics, block_bytes):
    return pltpu.CompilerParams(dimension_semantics=semantics,
                                vmem_limit_bytes=_vmem_limit(block_bytes))


def _norm_matmul_kernel(x_ref, g_ref, w_ref, b_ref, o_ref, h_ref):
    @pl.when(pl.program_id(1) == 0)
    def _():
        x = x_ref[...]
        ms = jnp.mean(x * x, axis=-1, keepdims=True)
        h_ref[...] = (x * lax.rsqrt(ms + RMS_EPS) * g_ref[...]).astype(BF16)

    acc = jnp.dot(h_ref[...], w_ref[...], preferred_element_type=F32) + b_ref[...]
    o_ref[...] = acc.astype(o_ref.dtype)


def _norm_matmul(x, g, w, b, *, tn, name, out_dtype=F32):
    n, d = x.shape
    c = w.shape[1]
    tm = _pick_tile(n, (1024, 512, 256, 128, 64, 32, 16, 8))
    blocks = (_nbytes((tm, d), F32) + _nbytes((d, tn), BF16) + _nbytes((tm, tn), F32)
              + _nbytes((tm, d), BF16))
    return pl.pallas_call(
        _norm_matmul_kernel,
        name=name,
        grid=(n // tm, c // tn),
        in_specs=[
            pl.BlockSpec((tm, d), lambda i, j: (i, 0)),
            pl.BlockSpec((1, d), lambda i, j: (0, 0)),
            pl.BlockSpec((d, tn), lambda i, j: (0, j)),
            pl.BlockSpec((1, tn), lambda i, j: (0, j)),
        ],
        out_specs=pl.BlockSpec((tm, tn), lambda i, j: (i, j)),
        out_shape=jax.ShapeDtypeStruct((n, c), out_dtype),
        scratch_shapes=[pltpu.VMEM((tm, d), BF16)],
        compiler_params=_params(("arbitrary", "arbitrary"), blocks),
    )(x, g, w, b)


def _mlstm_kernel(*refs, chunk, nseq):
    (q_ref, k_ref, v_ref, o_ref, gt_ref, c0_ref, n0_ref, m0_ref, gn_ref) = refs[:9]
    h_ref, c1_ref, n1_ref, m1_ref, c_sc, n_sc, m_sc = refs[-7:]
    L = chunk
    ci = pl.program_id(1)

    @pl.when(ci == 0)
    def _():
        c_sc[...] = c0_ref[0]
        n_sc[...] = n0_ref[...]
        m_sc[...] = m0_ref[...]

    lane = lax.broadcasted_iota(jnp.int32, (1, LANES), 1)
    row = lax.broadcasted_iota(jnp.int32, (L, L), 0)
    col = lax.broadcasted_iota(jnp.int32, (L, L), 1)
    causal = row >= col
    q_all = q_ref[...].astype(F32) * (M_DK ** -0.5)
    k_all = k_ref[...].astype(F32)
    v_all = v_ref[...].astype(F32)
    o_all = o_ref[...].astype(F32)

    nt_dims = (((1,), (1,)), ((), ()))
    chains = [(s, h) for s in range(nseq) for h in range(M_HEADS)]

    zs, z_rows = [], []
    tri = causal.astype(F32)
    eye8 = (lax.broadcasted_iota(jnp.int32, (SUBLANES, LANES), 0)
            == lax.broadcasted_iota(jnp.int32, (SUBLANES, LANES), 1)).astype(F32)
    for s in range(nseq):
        gates = gt_ref[s * L:(s + 1) * L, :]
        b_all = jnp.dot(tri, -_softplus(-gates), precision=HI, preferred_element_type=F32)
        zs.append(jnp.where(lane < M_HEADS, gates, b_all))
    for s in range(nseq):
        z_rows.append(lax.dot_general(eye8, zs[s], nt_dims, precision=HI,
                                      preferred_element_type=F32))

    def qkv(s, h):
        rows = slice(s * L, (s + 1) * L)
        return (q_all[rows, h * M_DK:(h + 1) * M_DK], k_all[rows, h * M_DK:(h + 1) * M_DK],
                v_all[rows, h * M_DV:(h + 1) * M_DV])

    raw, qcs = [], []
    for s, h in chains:
        q, k, _ = qkv(s, h)
        raw.append(lax.dot_general(q, k, nt_dims, preferred_element_type=F32))
        qcs.append(lax.dot_general(q, c_sc[s, h], nt_dims, preferred_element_type=F32))

    scs, inters, m_ts = [], [], []
    for idx, (s, h) in enumerate(chains):
        i_row = z_rows[s][h:h + 1, :]
        b_row = z_rows[s][M_HEADS + h:M_HEADS + h + 1, :]
        b_col = zs[s][:, M_HEADS + h:M_HEADS + h + 1]
        dmat = jnp.where(causal, b_col - b_row + i_row, -jnp.inf)
        m_inter = b_col + m_sc[s, h]
        m_t = jnp.maximum(m_inter, jnp.max(dmat, axis=-1, keepdims=True))
        scs.append(raw[idx] * jnp.exp(dmat - m_t))
        inters.append(jnp.exp(m_inter - m_t))
        m_ts.append(m_t)

    svs = [jnp.dot(scs[idx], qkv(s, h)[2], preferred_element_type=F32)
           for idx, (s, h) in enumerate(chains)]

    upds = []
    for idx, (s, h) in enumerate(chains):
        _, k, v = qkv(s, h)
        i_col = zs[s][:, h:h + 1]
        b_col = zs[s][:, M_HEADS + h:M_HEADS + h + 1]
        m_new = m_ts[idx][L - 1:L, :]
        b_last = b_col[L - 1:L, :]
        w_s = jnp.exp(b_last - b_col + i_col - m_new)
        decay = jnp.exp(b_last + m_sc[s, h] - m_new)
        kv = lax.dot_general(v * w_s, k, (((0,), (0,)), ((), ())), preferred_element_type=F32)
        upds.append((decay, kv, jnp.sum(w_s * k, axis=0, keepdims=True), m_new))

    for idx, (s, h) in enumerate(chains):
        q, _, _ = qkv(s, h)
        rows = slice(s * L, (s + 1) * L)
        num = svs[idx] + inters[idx] * qcs[idx]
        den = (jnp.sum(scs[idx], axis=-1, keepdims=True)
               + inters[idx] * jnp.sum(q * n_sc[s, h], axis=-1, keepdims=True))
        hh = num / jnp.maximum(jnp.abs(den), jnp.exp(-m_ts[idx]))
        hn = (hh * lax.rsqrt(jnp.mean(hh * hh, axis=-1, keepdims=True) + RMS_EPS)
              * gn_ref[:, h * M_DV:(h + 1) * M_DV])
        h_ref[rows, h * M_DV:(h + 1) * M_DV] = hn * _sigmoid(o_all[rows, h * M_DV:(h + 1) * M_DV])

    for (s, h), (decay, kv, kn, m_new) in zip(chains, upds):
        c_sc[s, h] = decay * c_sc[s, h] + kv
        n_sc[s, h] = decay * n_sc[s, h] + kn
        m_sc[s, h] = m_new

    @pl.when(ci == pl.num_programs(1) - 1)
    def _():
        c1_ref[0] = c_sc[...]
        n1_ref[...] = n_sc[...]
        m1_ref[...] = m_sc[...]


def _mlstm(p_mix, gates, c0_all, n0, m0, gn, hm_buf, c_buf, *, layer, depth, row0, batch, seq, name):
    n_all = p_mix.shape[0]
    layer_in = min(layer, c0_all.shape[0] - 1)
    L = M_CHUNK if seq % M_CHUNK == 0 else seq
    nc = seq // L
    nseq = 1 if nc > 1 else _pick_tile(batch, (4, 2, 1))
    rows = nseq * L
    rb0 = row0 // rows
    assert row0 % rows == 0
    qk_w = M_HEADS * M_DK
    v_w = M_HEADS * M_DV
    assert qk_w * 2 == v_w

    def tok(bi, c):
        return rb0 + bi * nc + c

    state_map = lambda bi, c: (bi, 0, 0, 0)
    c_blk = (1, nseq, M_HEADS, M_DV, M_DK)
    in_specs = [
        pl.BlockSpec((rows, qk_w), lambda bi, c: (tok(bi, c), 0)),
        pl.BlockSpec((rows, qk_w), lambda bi, c: (tok(bi, c), 1)),
        pl.BlockSpec((rows, v_w), lambda bi, c: (tok(bi, c), 1)),
        pl.BlockSpec((rows, v_w), lambda bi, c: (tok(bi, c), 2)),
        pl.BlockSpec((rows, LANES), lambda bi, c: (tok(bi, c), 0)),
        pl.BlockSpec(c_blk, lambda bi, c: (layer_in, bi, 0, 0, 0)),
        pl.BlockSpec((nseq, M_HEADS, 1, M_DK), state_map),
        pl.BlockSpec((nseq, M_HEADS, 1, 1), state_map),
        pl.BlockSpec((1, v_w), lambda bi, c: (0, 0)),
    ]
    args = [p_mix, p_mix, p_mix, p_mix, gates, c0_all, n0.reshape(batch, M_HEADS, 1, M_DK),
            m0.reshape(batch, M_HEADS, 1, 1), gn]
    aliases = {}
    for buf, out_idx in ((hm_buf, 0), (c_buf, 1)):
        if buf is not None:
            aliases[len(args)] = out_idx
            in_specs.append(ANY_SPEC)
            args.append(buf)
    blocks = (2 * _nbytes((rows, qk_w), F32) + 3 * _nbytes((rows, v_w), F32)
              + 3 * _nbytes((nseq, M_HEADS, M_DV, M_DK), F32))
    return pl.pallas_call(
        functools.partial(_mlstm_kernel, chunk=L, nseq=nseq),
        name=name,
        grid=(batch // nseq, nc),
        in_specs=in_specs,
        out_specs=[
            pl.BlockSpec((rows, v_w), lambda bi, c: (tok(bi, c), 0)),
            pl.BlockSpec(c_blk, lambda bi, c: (layer, bi, 0, 0, 0)),
            pl.BlockSpec((nseq, M_HEADS, 1, M_DK), state_map),
            pl.BlockSpec((nseq, M_HEADS, 1, 1), state_map),
        ],
        out_shape=[
            jax.ShapeDtypeStruct((n_all, v_w), F32),
            jax.ShapeDtypeStruct((depth, batch, M_HEADS, M_DV, M_DK), F32),
            jax.ShapeDtypeStruct((batch, M_HEADS, 1, M_DK), F32),
            jax.ShapeDtypeStruct((batch, M_HEADS, 1, 1), F32),
        ],
        scratch_shapes=[
            pltpu.VMEM((nseq, M_HEADS, M_DV, M_DK), F32),
            pltpu.VMEM((nseq, M_HEADS, 1, M_DK), F32),
            pltpu.VMEM((nseq, M_HEADS, 1, 1), F32),
        ],
        input_output_aliases=aliases,
        compiler_params=_params(("arbitrary", "arbitrary"), blocks),
    )(*args)


def _seq_rows(x, offset, *, seq):
    rows = x.shape[0]
    i = lax.broadcasted_iota(jnp.int32, (rows // seq, rows), 0)
    r = lax.broadcasted_iota(jnp.int32, (rows // seq, rows), 1)
    pick = (r == i * seq + offset).astype(F32)
    return jnp.dot(pick, x, precision=HI, preferred_element_type=F32)


def _seq_place(per_seq, offset, *, seq):
    nseq = per_seq.shape[0]
    r = lax.broadcasted_iota(jnp.int32, (nseq * seq, nseq), 0)
    i = lax.broadcasted_iota(jnp.int32, (nseq * seq, nseq), 1)
    pick = (r == i * seq + offset).astype(F32)
    return jnp.dot(pick, per_seq, precision=HI, preferred_element_type=F32)


def _shift_rows(x, shift, *, seq, tile_has_whole_seqs, before):
    rows = x.shape[0]
    rolled = pltpu.roll(x, shift, 0)
    r = lax.broadcasted_iota(jnp.int32, (rows, 1), 0)
    if tile_has_whole_seqs:
        head = _seq_place(before[0], 0, seq=seq)
        for s in range(1, shift):
            head = head + _seq_place(before[s], s, seq=seq)
        return jnp.where((r % seq) < shift, head, rolled)
    out = rolled
    for s in range(shift):
        out = jnp.where(r == s, before[s:s + 1, :], out)
    return out


def _hk_to_kh(a, axis):
    shape = a.shape
    a = a.reshape(shape[:axis] + (R_HEADS, R_HD) + shape[axis + 1:])
    return jnp.swapaxes(a, axis, axis + 1).reshape(shape)


def _kh_to_hk(a, axis):
    shape = a.shape
    a = a.reshape(shape[:axis] + (R_HD, R_HEADS) + shape[axis + 1:])
    return jnp.swapaxes(a, axis, axis + 1).reshape(shape)


def _sections(a, axis, fn):
    rw = R_HEADS * R_HD
    idx = lambda lo, hi: tuple(slice(lo, hi) if d == axis else slice(None) for d in range(a.ndim))
    parts = [fn(a[idx(i * rw, (i + 1) * rw)], axis) for i in range(3)]
    if a.shape[axis] > 3 * rw:
        parts.append(a[idx(3 * rw, a.shape[axis])])
    return jnp.concatenate(parts, axis=axis)


def _rwkv_prep_kernel(x_ref, gn_ref, wp_ref, bp_ref, st_ref, mu_ref, wl_ref, w0_ref, a0_ref,
                      rr_ref, rk_ref, rv_ref, dec_ref, a_ref, g_ref, sh_ref, carry_sc,
                      *, seq, whole_seqs):
    xin = x_ref[...]
    tb = xin.shape[0]
    rw = R_HEADS * R_HD
    ms = jnp.mean(xin * xin, axis=-1, keepdims=True)
    h = (xin * lax.rsqrt(ms + RMS_EPS) * gn_ref[...]).astype(BF16)
    if not whole_seqs:
        @pl.when(pl.program_id(1) == 0)
        def _():
            carry_sc[...] = st_ref[0]

    def proj(lo, hi):
        return jnp.dot(h, wp_ref[:, lo:hi], preferred_element_type=F32) + bp_ref[:, lo:hi]

    def shifted(x, lo, hi):
        if whole_seqs:
            prev = _shift_rows(x, 1, seq=seq, tile_has_whole_seqs=True, before=[st_ref[:, lo:hi]])
            sh_ref[:, lo:hi] = _seq_rows(x, seq - 1, seq=seq)
        else:
            prev = _shift_rows(x, 1, seq=seq, tile_has_whole_seqs=False,
                               before=carry_sc[:, lo:hi])
            carry_sc[:, lo:hi] = x[tb - 1:tb, :]
            sh_ref[0, :, lo:hi] = x[tb - 1:tb, :]
        return x + (prev - x) * mu_ref[:, lo:hi]

    x_lora = proj(3 * rw, 3 * rw + R_LORA)
    x_r = proj(0, rw)
    z = shifted(x_lora, 3 * rw, 3 * rw + R_LORA)
    lane = lax.broadcasted_iota(jnp.int32, (1, R_LORA), 1)
    act = jnp.where(lane < 64, jnp.tanh(z), jnp.where(lane < 128, z, _sigmoid(z))).astype(BF16)
    lo_w = jnp.dot(act, wl_ref[:, 0:rw], preferred_element_type=F32)
    x_k = proj(rw, 2 * rw)
    rr_ref[...] = shifted(x_r, 0, rw)
    lo_a = jnp.dot(act, wl_ref[:, rw:2 * rw], preferred_element_type=F32)
    x_v = proj(2 * rw, 3 * rw)
    rk_ref[...] = shifted(x_k, rw, 2 * rw)
    dec_ref[...] = jnp.exp(-jnp.exp(-_softplus(-(w0_ref[...] + lo_w)) - 0.5))
    g_ref[...] = jnp.dot(act, wl_ref[:, 2 * rw:3 * rw], preferred_element_type=F32)
    rv_ref[...] = shifted(x_v, 2 * rw, 3 * rw)
    a_ref[...] = _sigmoid(a0_ref[...] + lo_a)


def _rwkv_prep(x, gn, w_rw, b_rw, shift_state, mu, w_lora, w0, a0, *, row0, batch, seq, name):
    d = x.shape[1]
    c = w_rw.shape[1]
    rw = R_HEADS * R_HD
    n_tok = batch * seq
    whole = seq <= 64
    if whole:
        tb = _pick_tile(n_tok, (256, 128, 64, 32, 16, 8))
        assert tb % seq == 0 and row0 % tb == 0
        grid = (n_tok // tb, 1)
        rb0 = row0 // tb
        tok_map = lambda i, j: (rb0 + i, 0)
        out_map = lambda i, j: (i, 0)
        state = shift_state
        st_spec = pl.BlockSpec((tb // seq, c), out_map)
        sh_shape = jax.ShapeDtypeStruct((batch, c), F32)
    else:
        tb = _pick_tile(seq, (256, 128, 64, 32, 16, 8))
        nt = seq // tb
        grid = (batch, nt)
        rb0 = row0 // tb
        tok_map = lambda b, j: (rb0 + b * nt + j, 0)
        out_map = lambda b, j: (b * nt + j, 0)
        state = shift_state.reshape(batch, 1, c)
        st_spec = pl.BlockSpec((1, 1, c), lambda b, j: (b, 0, 0))
        sh_shape = jax.ShapeDtypeStruct((batch, 1, c), F32)
    fixed = lambda i, j: (0, 0)
    tok_out = jax.ShapeDtypeStruct((n_tok, rw), F32)
    tok_spec = pl.BlockSpec((tb, rw), out_map)
    blocks = (4 * _nbytes((tb, c), F32) + 6 * _nbytes((tb, rw), F32) + _nbytes(w_lora.shape, BF16)
              + _nbytes(w_rw.shape, BF16))
    outs = pl.pallas_call(
        functools.partial(_rwkv_prep_kernel, seq=seq, whole_seqs=whole),
        name=name,
        grid=grid,
        in_specs=[
            pl.BlockSpec((tb, d), tok_map),
            pl.BlockSpec((1, d), fixed),
            pl.BlockSpec(w_rw.shape, fixed),
            pl.BlockSpec((1, c), fixed),
            st_spec,
            pl.BlockSpec((1, c), fixed),
            pl.BlockSpec(w_lora.shape, fixed),
            pl.BlockSpec((1, rw), fixed),
            pl.BlockSpec((1, rw), fixed),
        ],
        out_specs=[tok_spec] * 6 + [st_spec],
        out_shape=[tok_out] * 6 + [sh_shape],
        scratch_shapes=[pltpu.VMEM((1, c), F32)],
        compiler_params=_params(("arbitrary", "arbitrary"), blocks),
    )(x, gn, w_rw, b_rw, state, mu, w_lora, w0, a0)
    return (*outs[:6], outs[6].reshape(batch, c))


def _rwkv_scan_kernel(rr_ref, rk_ref, dec_ref, a_ref, rv_ref, kks_ref, ka_ref, rrk_ref,
                      gng_ref, gnb_ref, s0_ref, y_ref, s1_ref,
                      s_sc, w_sc, kka_sc, km_sc, nkk_sc, wr_sc, coef_sc, *, tb):
    j = pl.program_id(1)
    nk = R_HD

    @pl.when(j == 0)
    def _():
        s_sc[...] = s0_ref[0]

    rr = rr_ref[...]
    rk = rk_ref[...]
    a = a_ref[...]
    dec = dec_ref[...]
    kk = rk * kks_ref[...]
    nrm = jnp.sqrt(jnp.sum(kk * kk, axis=0, keepdims=True))
    kk = kk * (1.0 / jnp.maximum(nrm, 1e-12))
    kmod = rk * (1.0 + (a - 1.0) * ka_ref[...])
    kka = kk * a
    g2 = dec.reshape(nk * tb, LANES)
    tpos = lax.broadcasted_iota(jnp.int32, (nk * tb, 1), 0) % tb
    shift = 1
    while shift < tb:
        g2 = g2 * jnp.where(tpos >= shift, pltpu.roll(g2, shift, 0), 1.0)
        shift *= 2
    g_prev = jnp.where(tpos >= 1, pltpu.roll(g2, 1, 0), 1.0).reshape(nk, tb, LANES)
    g_run = g2.reshape(nk, tb, LANES)
    g_inv = 1.0 / g_run
    w_sc[...] = g_run
    kka_sc[...] = kka * g_inv
    km_sc[...] = kmod * g_inv
    nkk_sc[...] = -kk * g_prev
    wr_sc[...] = rr * g_run
    coef_sc[0] = jnp.sum(kka * rr, axis=0)
    coef_sc[1] = jnp.sum(kmod * rr, axis=0)
    coef_sc[2] = jnp.sum(rr * kmod * rrk_ref[...], axis=0)

    def row(ref, t, k):
        return jnp.broadcast_to(ref[k, pl.ds(t, 1), :], (nk, LANES))

    def coef(i, t):
        return jnp.broadcast_to(coef_sc[i, pl.ds(t, 1), :], (nk, LANES))

    def matvecs(t):
        acc_sa = jnp.zeros((nk, LANES), F32)
        acc_y = jnp.zeros((nk, LANES), F32)
        for k in range(nk):
            sk = s_sc[k]
            acc_sa = acc_sa + sk * row(nkk_sc, t, k)
            acc_y = acc_y + sk * row(wr_sc, t, k)
        return acc_sa, acc_y

    def emit(t, sa, y0):
        val = rv_ref[t]
        y = y0 + sa * coef(0, t) + val * coef(1, t)
        mean = jnp.sum(y, axis=0, keepdims=True) * (1.0 / nk)
        d = y - mean
        var = jnp.sum(d * d, axis=0, keepdims=True) * (1.0 / nk)
        yn = d * lax.rsqrt(var + GN_EPS) * gng_ref[...] + gnb_ref[...]
        y_ref[t] = yn + coef(2, t) * val
        return val

    def step(t, carry):
        sa, y0 = carry
        val = emit(t, sa, y0)
        acc_sa = jnp.zeros((nk, LANES), F32)
        acc_y = jnp.zeros((nk, LANES), F32)
        for k in range(nk):
            sk = s_sc[k] + sa * row(kka_sc, t, k) + val * row(km_sc, t, k)
            s_sc[k] = sk
            acc_sa = acc_sa + sk * row(nkk_sc, t + 1, k)
            acc_y = acc_y + sk * row(wr_sc, t + 1, k)
        return acc_sa, acc_y

    sa, y0 = lax.fori_loop(0, tb - 1, step, matvecs(0))
    t_last = tb - 1
    val = emit(t_last, sa, y0)
    for k in range(nk):
        s_sc[k] = ((s_sc[k] + sa * row(kka_sc, t_last, k) + val * row(km_sc, t_last, k))
                   * row(w_sc, t_last, k))

    @pl.when(j == pl.num_programs(1) - 1)
    def _():
        s1_ref[0] = s_sc[...]


LAYOUT_TB = 128


def _lane_slabs(in_ref, p):
    nb = HEADS_PER_LANE_GROUP
    zts = [in_ref[b, :, p * LANES:(p + 1) * LANES].T for b in range(nb)]
    for i in range(SUBLANES):
        d = jnp.concatenate([zt[i * R_HEADS:(i + 1) * R_HEADS, :] for zt in zts], axis=0)
        yield SUBLANES * p + i, d.T


def _lanes_in_kernel(src_ref, dst_ref, *, rows):
    for p in range(R_HD // SUBLANES):
        for i, slab in _lane_slabs(src_ref, p):
            if rows:
                dst_ref[i] = slab
            else:
                dst_ref[:, pl.ds(i, 1), :] = slab[:, None, :]


def _lanes_in(x, *, batch, seq, rows):
    nb = HEADS_PER_LANE_GROUP
    g = batch // nb
    w = R_HEADS * R_HD
    if seq % LAYOUT_TB != 0 and (g * seq) % LAYOUT_TB == 0:
        x = x.reshape(g, nb, seq, w).transpose(1, 0, 2, 3).reshape(nb * g * seq, w)
        return _lanes_in(x, batch=nb, seq=g * seq, rows=rows)
    if seq % LAYOUT_TB != 0:
        x = x.reshape(g, nb, seq, R_HD, R_HEADS)
        if rows:
            return x.transpose(3, 0, 2, 1, 4).reshape(R_HD, g * seq, LANES)
        return x.transpose(0, 2, 3, 1, 4).reshape(g * seq, R_HD, LANES)
    nt = seq // LAYOUT_TB
    if rows:
        out_spec = pl.BlockSpec((R_HD, LAYOUT_TB, LANES), lambda gi, j: (0, gi * nt + j, 0))
        out_shape = jax.ShapeDtypeStruct((R_HD, g * seq, LANES), F32)
    else:
        out_spec = pl.BlockSpec((LAYOUT_TB, R_HD, LANES), lambda gi, j: (gi * nt + j, 0, 0))
        out_shape = jax.ShapeDtypeStruct((g * seq, R_HD, LANES), F32)
    return pl.pallas_call(
        functools.partial(_lanes_in_kernel, rows=rows),
        name="lanes_in_rows" if rows else "lanes_in_full",
        grid=(g, nt),
        in_specs=[pl.BlockSpec((nb, LAYOUT_TB, w), lambda gi, j: (gi, j, 0))],
        out_specs=out_spec,
        out_shape=out_shape,
        compiler_params=_params(("arbitrary", "arbitrary"), 2 * _nbytes((nb, LAYOUT_TB, w), F32)),
    )(x.reshape(batch, seq, w))


def _lanes_out_kernel(y_ref, o_ref):
    nb = HEADS_PER_LANE_GROUP
    for p in range(R_HD // SUBLANES):
        pieces = [[] for _ in range(nb)]
        for i in range(SUBLANES):
            st = y_ref[:, SUBLANES * p + i, :].T
            for b in range(nb):
                pieces[b].append(st[b * R_HEADS:(b + 1) * R_HEADS, :])
        for b in range(nb):
            zt = jnp.concatenate(pieces[b], axis=0)
            o_ref[b, :, p * LANES:(p + 1) * LANES] = zt.T


def _lanes_out(y, *, batch, seq):
    nb = HEADS_PER_LANE_GROUP
    g = batch // nb
    w = R_HEADS * R_HD
    if seq % LAYOUT_TB != 0 and (g * seq) % LAYOUT_TB == 0:
        tok = _lanes_out(y, batch=nb, seq=g * seq)
        return tok.reshape(nb, g, seq, w).transpose(1, 0, 2, 3).reshape(batch * seq, w)
    if seq % LAYOUT_TB != 0:
        y = y.reshape(g, seq, R_HD, nb, R_HEADS)
        return y.transpose(0, 3, 1, 2, 4).reshape(batch * seq, w)
    nt = seq // LAYOUT_TB
    out = pl.pallas_call(
        _lanes_out_kernel,
        name="lanes_out",
        grid=(g, nt),
        in_specs=[pl.BlockSpec((LAYOUT_TB, R_HD, LANES), lambda gi, j: (gi * nt + j, 0, 0))],
        out_specs=pl.BlockSpec((nb, LAYOUT_TB, w), lambda gi, j: (gi, j, 0)),
        out_shape=jax.ShapeDtypeStruct((batch, seq, w), F32),
        compiler_params=_params(("arbitrary", "arbitrary"), 2 * _nbytes((nb, LAYOUT_TB, w), F32)),
    )(y)
    return out.reshape(batch * seq, w)


def _param_lanes(p):
    p = p.reshape(R_HEADS, R_HD).T
    return jnp.tile(p, (1, HEADS_PER_LANE_GROUP))


def _rwkv_scan(rr, rk, rv, dec, a, kks, ka, rrk, gng, gnb, s0, *, batch, seq, name):
    assert batch % HEADS_PER_LANE_GROUP == 0
    g = batch // HEADS_PER_LANE_GROUP
    tb = _pick_tile(seq, (32, 16, 8))
    nt = seq // tb
    rows = [_lanes_in(v, batch=batch, seq=seq, rows=True) for v in (rr, rk, dec, a)]
    val = _lanes_in(rv, batch=batch, seq=seq, rows=False)
    s0l = s0.reshape(g, HEADS_PER_LANE_GROUP, R_HEADS, R_HD, R_HD).transpose(0, 4, 3, 1, 2)
    s0l = s0l.reshape(g, R_HD, R_HD, LANES)
    row_spec = pl.BlockSpec((R_HD, tb, LANES), lambda gi, j: (0, gi * nt + j, 0))
    val_spec = pl.BlockSpec((tb, R_HD, LANES), lambda gi, j: (gi * nt + j, 0, 0))
    kpar_spec = pl.BlockSpec((R_HD, 1, LANES), lambda gi, j: (0, 0, 0))
    vpar_spec = pl.BlockSpec((R_HD, LANES), lambda gi, j: (0, 0))
    st_spec = pl.BlockSpec((1, R_HD, R_HD, LANES), lambda gi, j: (gi, 0, 0, 0))
    vec_bytes = _nbytes((tb, R_HD, LANES), F32)
    blocks = 11 * vec_bytes + 3 * _nbytes((R_HD, R_HD, LANES), F32)
    kpar = lambda p: _param_lanes(p)[:, None, :]
    y, s1l = pl.pallas_call(
        functools.partial(_rwkv_scan_kernel, tb=tb),
        name=name,
        grid=(g, nt),
        in_specs=[row_spec] * 4 + [val_spec] + [kpar_spec] * 3 + [vpar_spec] * 2 + [st_spec],
        out_specs=[val_spec, st_spec],
        out_shape=[
            jax.ShapeDtypeStruct((g * seq, R_HD, LANES), F32),
            jax.ShapeDtypeStruct(s0l.shape, F32),
        ],
        scratch_shapes=[pltpu.VMEM((R_HD, R_HD, LANES), F32)]
        + [pltpu.VMEM((R_HD, tb, LANES), F32)] * 5
        + [pltpu.VMEM((3, tb, LANES), F32)],
        compiler_params=_params(("arbitrary", "arbitrary"), blocks),
    )(*rows, val, kpar(kks), kpar(ka), kpar(rrk), _param_lanes(gng), _param_lanes(gnb), s0l)
    s1 = s1l.reshape(g, R_HD, R_HD, HEADS_PER_LANE_GROUP, R_HEADS).transpose(0, 3, 4, 2, 1)
    return _lanes_out(y, batch=batch, seq=seq), s1.reshape(batch, R_HEADS, R_HD, R_HD)


def _mix_out_kernel(hm_ref, yr_ref, g_ref, gm_ref, gr_ref, x_ref, wm_ref, wr_ref, wo_ref, *rest):
    o_ref = rest[-1]
    bm = jnp.dot(hm_ref[...].astype(BF16), wm_ref[...], preferred_element_type=F32)
    yr = (yr_ref[...] * g_ref[...]).astype(BF16)
    br = jnp.dot(yr, wr_ref[...], preferred_element_type=F32)
    merged = (_sigmoid(gm_ref[...].astype(F32)) * bm
              + _sigmoid(gr_ref[...].astype(F32)) * br)
    o_ref[...] = x_ref[...] + jnp.dot(merged.astype(BF16), wo_ref[...], preferred_element_type=F32)


def _mix_out(hm, yr, g, p_mix, x, wm, wr, wo, out_buf, *, row0, name):
    n_all, d = x.shape
    n = yr.shape[0]
    tm = _pick_tile(n, (256, 128, 64, 32, 16, 8))
    assert row0 % tm == 0
    rb0 = row0 // tm
    gm_blk = (2 * M_HEADS * (M_DK + M_DV)) // d
    shared = lambda i: (rb0 + i, 0)
    own = lambda i: (i, 0)
    fixed = lambda i: (0, 0)
    aliased = out_buf is not None
    in_specs = [
        pl.BlockSpec((tm, wm.shape[0]), shared),
        pl.BlockSpec((tm, wr.shape[0]), own),
        pl.BlockSpec((tm, wr.shape[0]), own),
        pl.BlockSpec((tm, d), lambda i: (rb0 + i, gm_blk)),
        pl.BlockSpec((tm, d), lambda i: (rb0 + i, gm_blk + 1)),
        pl.BlockSpec((tm, d), shared),
        pl.BlockSpec(wm.shape, fixed),
        pl.BlockSpec(wr.shape, fixed),
        pl.BlockSpec(wo.shape, fixed),
    ]
    args = [hm, yr, g, p_mix, p_mix, x, wm, wr, wo]
    if aliased:
        in_specs.append(ANY_SPEC)
        args.append(out_buf)
    blocks = 7 * _nbytes((tm, d), F32) + 3 * _nbytes((d, d), BF16)
    return pl.pallas_call(
        _mix_out_kernel,
        name=name,
        grid=(n // tm,),
        in_specs=in_specs,
        out_specs=pl.BlockSpec((tm, d), shared),
        out_shape=jax.ShapeDtypeStruct((n_all, d), F32),
        input_output_aliases={9: 0} if aliased else {},
        compiler_params=_params(("arbitrary",), blocks),
    )(*args)


FFN_CHUNK = 256


def _ffn_kernel(x_ref, gn_ref, wu_ref, cs_ref, cw_ref, cb_ref, wd_ref, *rest, seq, whole_seqs):
    o_ref, st_ref, carry_sc = rest[-3:]
    x = x_ref[...]
    tb = x.shape[0]
    dff = wd_ref.shape[0]
    ms = jnp.mean(x * x, axis=-1, keepdims=True)
    h = (x * lax.rsqrt(ms + RMS_EPS) * gn_ref[...]).astype(BF16)
    if not whole_seqs:
        @pl.when(pl.program_id(1) == 0)
        def _():
            carry_sc[...] = cs_ref[0]
    def up(c0):
        return [jnp.dot(h, wu_ref[:, lo:lo + FFN_CHUNK], preferred_element_type=F32)
                for lo in (c0, dff + c0)]

    acc = x
    starts = list(range(0, dff, FFN_CHUNK))
    u_next = up(starts[0])
    for ci, c0 in enumerate(starts):
        u_cur = u_next
        if ci + 1 < len(starts):
            u_next = up(starts[ci + 1])
        halves = []
        for lo, u in zip((c0, dff + c0), u_cur):
            cols = slice(lo, lo + FFN_CHUNK)
            if whole_seqs:
                before = [cs_ref[0, :, cols], cs_ref[1, :, cols]]
                p1 = _shift_rows(u, 1, seq=seq, tile_has_whole_seqs=True, before=before[1:])
                p2 = _shift_rows(u, 2, seq=seq, tile_has_whole_seqs=True, before=before)
                st_ref[0, :, cols] = _seq_rows(u, seq - 2, seq=seq)
                st_ref[1, :, cols] = _seq_rows(u, seq - 1, seq=seq)
            else:
                carry = carry_sc[:, cols]
                p1 = _shift_rows(u, 1, seq=seq, tile_has_whole_seqs=False, before=carry[1:2, :])
                p2 = _shift_rows(u, 2, seq=seq, tile_has_whole_seqs=False, before=carry)
                carry_sc[:, cols] = u[tb - 2:tb, :]
                st_ref[0, :, cols] = u[tb - 2:tb, :]
            halves.append(cb_ref[:, cols] + p2 * cw_ref[0:1, cols] + p1 * cw_ref[1:2, cols]
                          + u * cw_ref[2:3, cols])
        gate, val = halves
        act = (gate * _sigmoid(gate) * val).astype(BF16)
        acc = acc + jnp.dot(act, wd_ref[c0:c0 + FFN_CHUNK, :], preferred_element_type=F32)
    o_ref[...] = acc


def _ffn(x, gn, wu, conv_state, cw, cb, wd, out_buf, *, row0, batch, seq, name):
    n_all, d = x.shape
    c = wu.shape[1]
    n_tok = batch * seq
    taps = cw.shape[0]
    assert taps == 3 and seq >= taps - 1 and wd.shape[0] % FFN_CHUNK == 0
    whole = seq <= 64
    if whole:
        tb = _pick_tile(n_tok, (128, 64, 32, 16, 8))
        assert tb % seq == 0 and row0 % tb == 0
        grid = (n_tok // tb, 1)
        rb0 = row0 // tb
        tok_map = lambda i, j: (rb0 + i, 0)
        state = jnp.swapaxes(conv_state, 0, 1)
        st_shape = jax.ShapeDtypeStruct((taps - 1, batch, c), F32)
        st_spec = pl.BlockSpec((taps - 1, tb // seq, c), lambda i, j: (0, i, 0))
    else:
        tb = _pick_tile(seq, (256, 128, 64, 32, 16, 8))
        nt = seq // tb
        grid = (batch, nt)
        rb0 = row0 // tb
        tok_map = lambda b, j: (rb0 + b * nt + j, 0)
        state = conv_state
        st_shape = jax.ShapeDtypeStruct((batch, taps - 1, c), F32)
        st_spec = pl.BlockSpec((1, taps - 1, c), lambda b, j: (b, 0, 0))
    fixed = lambda i, j: (0, 0)
    aliased = out_buf is not None
    in_specs = [
        pl.BlockSpec((tb, d), tok_map),
        pl.BlockSpec((1, d), fixed),
        pl.BlockSpec(wu.shape, fixed),
        st_spec,
        pl.BlockSpec(cw.shape, fixed),
        pl.BlockSpec((1, c), fixed),
        pl.BlockSpec(wd.shape, fixed),
    ]
    args = [x, gn, wu, state, cw, cb, wd]
    if aliased:
        in_specs.append(ANY_SPEC)
        args.append(out_buf)
    blocks = _nbytes(wu.shape, BF16) + _nbytes(wd.shape, BF16) + 6 * _nbytes((tb, d), F32)
    y, st = pl.pallas_call(
        functools.partial(_ffn_kernel, seq=seq, whole_seqs=whole),
        name=name,
        grid=grid,
        in_specs=in_specs,
        out_specs=[pl.BlockSpec((tb, d), tok_map), st_spec],
        out_shape=[jax.ShapeDtypeStruct((n_all, d), F32), st_shape],
        scratch_shapes=[pltpu.VMEM((taps - 1, c), F32)],
        input_output_aliases={7: 0} if aliased else {},
        compiler_params=_params(("arbitrary", "arbitrary"), blocks),
    )(*args)
    return y, (jnp.swapaxes(st, 0, 1) if whole else st)


def _final_norm_kernel(x_ref, g_ref, o_ref):
    x = x_ref[...]
    ms = jnp.mean(x * x, axis=-1, keepdims=True)
    o_ref[...] = x * lax.rsqrt(ms + RMS_EPS) * g_ref[...]


def _final_norm(x, g, *, row0, n, name):
    d = x.shape[1]
    tm = _pick_tile(n, (1024, 512, 256, 128, 64, 32, 16, 8))
    assert row0 % tm == 0
    rb0 = row0 // tm
    return pl.pallas_call(
        _final_norm_kernel,
        name=name,
        grid=(n // tm,),
        in_specs=[pl.BlockSpec((tm, d), lambda i: (rb0 + i, 0)),
                  pl.BlockSpec((1, d), lambda i: (0, 0))],
        out_specs=pl.BlockSpec((tm, d), lambda i: (i, 0)),
        out_shape=jax.ShapeDtypeStruct((n, d), F32),
        compiler_params=_params(("arbitrary",), 2 * _nbytes((tm, d), F32)),
    )(x, g)


def kernel(x_prompt, x_sample, state_mlstm_C, state_mlstm_n, state_mlstm_m, state_rwkv_S, state_rwkv_shift, state_ffn_conv, norm_mix_g, w_in, b_in, mu_shift, m_norm_g, r_w0, r_w_up, r_a0, r_a_up, r_g_up, r_k_k, r_k_a, r_r_k, r_gn_g, r_gn_b, w_br_m, w_br_r, w_out, norm_ffn_g, w_up, conv_w, conv_b, w_down, norm_final_g):
    bp, tp, d = x_prompt.shape
    bs, ts, _ = x_sample.shape
    depth = w_in.shape[0]
    n_p = bp * tp
    n_s = bs * ts
    rw = R_HEADS * R_HD
    qkvo = 2 * M_HEADS * (M_DK + M_DV)
    n_gate = 2 * M_HEADS
    mix_cols = qkvo + n_gate + 2 * d
    rw_cols = 3 * rw + R_LORA
    dff2 = w_up.shape[2]
    assert w_in.shape[2] == mix_cols + rw_cols

    x = jnp.concatenate([x_prompt.reshape(n_p, d), x_sample.reshape(n_s, d)], axis=0)

    zc = jnp.zeros((1, bp, M_HEADS, M_DV, M_DK), F32)
    c_all_p = c_all_s = None
    zn = jnp.zeros((bp, M_HEADS, M_DK), F32)
    zm = jnp.zeros((bp, M_HEADS), F32)
    zs = jnp.zeros((bp, R_HEADS, R_HD, R_HD), F32)
    zsh = jnp.zeros((bp, rw_cols), F32)
    zcv = jnp.zeros((bp, conv_w.shape[1] - 1, dff2), F32)

    outs_p = [[] for _ in range(5)]
    outs_s = [[] for _ in range(5)]
    for l in range(depth):
        wl = w_in[l]
        bl = b_in[l]
        w_mix = jnp.concatenate([wl[:, :qkvo], wl[:, qkvo + n_gate:mix_cols]], axis=1).astype(BF16)
        b_mix = jnp.concatenate([bl[:qkvo], bl[qkvo + n_gate:mix_cols]])[None]
        w_gate = jnp.pad(wl[:, qkvo:qkvo + n_gate], ((0, 0), (0, LANES - n_gate))).astype(BF16)
        b_gate = jnp.pad(bl[qkvo:qkvo + n_gate], (0, LANES - n_gate))[None]
        w_rw = _sections(wl[:, mix_cols:], 1, _hk_to_kh).astype(BF16)
        b_rw = _sections(bl[mix_cols:], 0, _hk_to_kh)[None]
        mu_rw = _sections(mu_shift[l], 0, _hk_to_kh)[None]
        w_lora = jnp.zeros((R_LORA, 3 * rw), F32)
        w_lora = w_lora.at[0:64, 0:rw].set(r_w_up[l]).at[64:128, rw:2 * rw].set(r_a_up[l])
        w_lora = _sections(w_lora.at[128:256, 2 * rw:].set(r_g_up[l]), 1, _hk_to_kh).astype(BF16)
        w_br_r_kh = _hk_to_kh(w_br_r[l], 0).astype(BF16)
        g_mix = norm_mix_g[l][None]

        p_mix = _norm_matmul(x, g_mix, w_mix, b_mix, tn=1024, name="in_proj_mix", out_dtype=BF16)
        gates = _norm_matmul(x, g_mix, w_gate, b_gate, tn=LANES, name="in_proj_gates")

        gn = m_norm_g[l].reshape(1, M_HEADS * M_DV)
        hm, c_all_p, nn_p, m_p = _mlstm(
            p_mix, gates, zc, zn, zm, gn, None, c_all_p, layer=l, depth=depth,
            row0=0, batch=bp, seq=tp, name="mlstm_prompt")
        hm, c_all_s, nn_s, m_s = _mlstm(
            p_mix, gates, state_mlstm_C, state_mlstm_n[l], state_mlstm_m[l], gn, hm, c_all_s,
            layer=l, depth=depth, row0=n_p, batch=bs, seq=ts, name="mlstm_sample")

        prep_args = (mu_rw, w_lora, _hk_to_kh(r_w0[l], 0)[None], _hk_to_kh(r_a0[l], 0)[None])
        rr_p, rk_p, rv_p, wp_p, ap_p, g_p, sh_p = _rwkv_prep(
            x, g_mix, w_rw, b_rw, zsh, *prep_args, row0=0, batch=bp, seq=tp,
            name="rwkv_prep_prompt")
        rr_s, rk_s, rv_s, wp_s, ap_s, g_s, sh_s = _rwkv_prep(
            x, g_mix, w_rw, b_rw, _sections(state_rwkv_shift[l], 1, _hk_to_kh), *prep_args,
            row0=n_p, batch=bs, seq=ts, name="rwkv_prep_sample")
        sh_p = _sections(sh_p, 1, _kh_to_hk)
        sh_s = _sections(sh_s, 1, _kh_to_hk)
        scan_args = (r_k_k[l], r_k_a[l], r_r_k[l], r_gn_g[l], r_gn_b[l])
        y_p, s_p = _rwkv_scan(rr_p, rk_p, rv_p, wp_p, ap_p, *scan_args, zs,
                              batch=bp, seq=tp, name="rwkv_scan_prompt")
        y_s, s_s = _rwkv_scan(rr_s, rk_s, rv_s, wp_s, ap_s, *scan_args, state_rwkv_S[l],
                              batch=bs, seq=ts, name="rwkv_scan_sample")

        w_branches = (w_br_m[l].astype(BF16), w_br_r_kh, w_out[l].astype(BF16))
        x_mid = _mix_out(hm, y_p, g_p, p_mix, x, *w_branches, None, row0=0, name="mix_out_prompt")
        x_mid = _mix_out(hm, y_s, g_s, p_mix, x, *w_branches, x_mid, row0=n_p,
                         name="mix_out_sample")

        ffn_w = (conv_w[l], conv_b[l][None], w_down[l].astype(BF16))
        g_ffn = norm_ffn_g[l][None]
        wu = w_up[l].astype(BF16)
        x, cv_p = _ffn(x_mid, g_ffn, wu, zcv, *ffn_w, None, row0=0, batch=bp, seq=tp,
                       name="ffn_prompt")
        x, cv_s = _ffn(x_mid, g_ffn, wu, state_ffn_conv[l], *ffn_w, x, row0=n_p, batch=bs, seq=ts,
                       name="ffn_sample")

        for lst, val in zip(outs_p, (nn_p.reshape(bp, M_HEADS, M_DK), m_p.reshape(bp, M_HEADS),
                                     s_p, sh_p, cv_p)):
            lst.append(val)
        for lst, val in zip(outs_s, (nn_s.reshape(bs, M_HEADS, M_DK), m_s.reshape(bs, M_HEADS),
                                     s_s, sh_s, cv_s)):
            lst.append(val)

    g_fin = norm_final_g[None]
    y_prompt = _final_norm(x, g_fin, row0=0, n=n_p, name="final_norm_prompt").reshape(bp, tp, d)
    y_sample = _final_norm(x, g_fin, row0=n_p, n=n_s, name="final_norm_sample").reshape(bs, ts, d)
    return (y_prompt, y_sample, c_all_p, *[jnp.stack(o) for o in outs_p],
            c_all_s, *[jnp.stack(o) for o in outs_s])
```

```python
import functools

import jax
import jax.numpy as jnp
from jax import lax
from jax.experimental import pallas as pl
from jax.experimental.pallas import tpu as pltpu

F32 = jnp.float32
BF16 = jnp.bfloat16

M_HEADS = 4
M_DK = 128
M_DV = 256
M_CHUNK = 64
R_HEADS = 16
R_HD = 64
R_LORA = 256
RMS_EPS = 1e-6
GN_EPS = 64e-5

LANES = 128
SUBLANES = 8
VMEM_LIMIT_CAP = 56 * 1024 * 1024
HEADS_PER_LANE_GROUP = LANES // R_HEADS

HI = lax.Precision.HIGHEST
ANY_SPEC = pl.BlockSpec(memory_space=pl.ANY)


def _vmem_limit(block_bytes):
    return int(min(VMEM_LIMIT_CAP, max(16 * 1024 * 1024, 3 * block_bytes)))


def _nbytes(shape, dtype):
    n = 1
    for s in shape:
        n *= s
    return n * jnp.dtype(dtype).itemsize


def _pick_tile(n, candidates):
    for c in candidates:
        if n % c == 0:
            return c
    raise ValueError(f"no tile in {candidates} divides {n}")


def _sigmoid(x):
    return 1.0 / (1.0 + jnp.exp(-x))


def _softplus(x):
    return jnp.maximum(x, 0.0) + jnp.log(1.0 + jnp.exp(-jnp.abs(x)))


def _params(semantics, block_bytes):
    return pltpu.CompilerParams(dimension_semantics=semantics,
                                vmem_limit_bytes=_vmem_limit(block_bytes))


def _norm_matmul_kernel(x_ref, g_ref, w_ref, b_ref, o_ref, h_ref):
    @pl.when(pl.program_id(1) == 0)
    def _():
        x = x_ref[...]
        ms = jnp.mean(x * x, axis=-1, keepdims=True)
        h_ref[...] = (x * lax.rsqrt(ms + RMS_EPS) * g_ref[...]).astype(BF16)

    acc = jnp.dot(h_ref[...], w_ref[...], preferred_element_type=F32) + b_ref[...]
    o_ref[...] = acc.astype(o_ref.dtype)


def _norm_matmul(x, g, w, b, *, tn, name, out_dtype=F32):
    n, d = x.shape
    c = w.shape[1]
    tm = _pick_tile(n, (1024, 512, 256, 128, 64, 32, 16, 8))
    blocks = (_nbytes((tm, d), F32) + _nbytes((d, tn), BF16) + _nbytes((tm, tn), F32)
              + _nbytes((tm, d), BF16))
    return pl.pallas_call(
        _norm_matmul_kernel,
        name=name,
        grid=(n // tm, c // tn),
        in_specs=[
            pl.BlockSpec((tm, d), lambda i, j: (i, 0)),
            pl.BlockSpec((1, d), lambda i, j: (0, 0)),
            pl.BlockSpec((d, tn), lambda i, j: (0, j)),
            pl.BlockSpec((1, tn), lambda i, j: (0, j)),
        ],
        out_specs=pl.BlockSpec((tm, tn), lambda i, j: (i, j)),
        out_shape=jax.ShapeDtypeStruct((n, c), out_dtype),
        scratch_shapes=[pltpu.VMEM((tm, d), BF16)],
        compiler_params=_params(("arbitrary", "arbitrary"), blocks),
    )(x, g, w, b)


def _mlstm_kernel(*refs, chunk, slots, per_slot):
    tok_refs = [refs[5 * a:5 * a + 5] for a in range(slots)]
    c0_ref, n0_ref, m0_ref, gn_ref = refs[5 * slots:5 * slots + 4]
    h_ref, c1_ref, n1_ref, m1_ref, c_sc, n_sc, m_sc = refs[-7:]
    L = chunk
    nseq = slots * per_slot
    ci = pl.program_id(1)

    @pl.when(ci == 0)
    def _():
        c_sc[...] = c0_ref[0]
        n_sc[...] = n0_ref[...]
        m_sc[...] = m0_ref[...]

    lane = lax.broadcasted_iota(jnp.int32, (1, LANES), 1)
    row = lax.broadcasted_iota(jnp.int32, (L, L), 0)
    col = lax.broadcasted_iota(jnp.int32, (L, L), 1)
    causal = row >= col
    q_all = [r[0][...].astype(F32) * (M_DK ** -0.5) for r in tok_refs]
    k_all = [r[1][...].astype(F32) for r in tok_refs]
    v_all = [r[2][...].astype(F32) for r in tok_refs]
    o_all = [r[3][...].astype(F32) for r in tok_refs]

    def rows_of(s):
        return s // per_slot, slice((s % per_slot) * L, (s % per_slot + 1) * L)

    nt_dims = (((1,), (1,)), ((), ()))
    chains = [(s, h) for s in range(nseq) for h in range(M_HEADS)]

    zs, z_rows = [], []
    tri = causal.astype(F32)
    eye8 = (lax.broadcasted_iota(jnp.int32, (SUBLANES, LANES), 0)
            == lax.broadcasted_iota(jnp.int32, (SUBLANES, LANES), 1)).astype(F32)
    for s in range(nseq):
        a, rows = rows_of(s)
        gates = tok_refs[a][4][rows, :]
        b_all = jnp.dot(tri, -_softplus(-gates), precision=HI, preferred_element_type=F32)
        zs.append(jnp.where(lane < M_HEADS, gates, b_all))
    for s in range(nseq):
        z_rows.append(lax.dot_general(eye8, zs[s], nt_dims, precision=HI,
                                      preferred_element_type=F32))

    def qkv(s, h):
        a, rows = rows_of(s)
        return (q_all[a][rows, h * M_DK:(h + 1) * M_DK], k_all[a][rows, h * M_DK:(h + 1) * M_DK],
                v_all[a][rows, h * M_DV:(h + 1) * M_DV])

    raw, qcs = [], []
    for s, h in chains:
        q, k, _ = qkv(s, h)
        raw.append(lax.dot_general(q, k, nt_dims, preferred_element_type=F32))
        qcs.append(lax.dot_general(q, c_sc[s, h], nt_dims, preferred_element_type=F32))

    scs, inters, m_ts = [], [], []
    for idx, (s, h) in enumerate(chains):
        i_row = z_rows[s][h:h + 1, :]
        b_row = z_rows[s][M_HEADS + h:M_HEADS + h + 1, :]
        b_col = zs[s][:, M_HEADS + h:M_HEADS + h + 1]
        dmat = jnp.where(causal, b_col - b_row + i_row, -jnp.inf)
        m_inter = b_col + m_sc[s, h]
        m_t = jnp.maximum(m_inter, jnp.max(dmat, axis=-1, keepdims=True))
        scs.append(raw[idx] * jnp.exp(dmat - m_t))
        inters.append(jnp.exp(m_inter - m_t))
        m_ts.append(m_t)

    svs = [jnp.dot(scs[idx], qkv(s, h)[2], preferred_element_type=F32)
           for idx, (s, h) in enumerate(chains)]

    upds = []
    for idx, (s, h) in enumerate(chains):
        _, k, v = qkv(s, h)
        i_col = zs[s][:, h:h + 1]
        b_col = zs[s][:, M_HEADS + h:M_HEADS + h + 1]
        m_new = m_ts[idx][L - 1:L, :]
        b_last = b_col[L - 1:L, :]
        w_s = jnp.exp(b_last - b_col + i_col - m_new)
        decay = jnp.exp(b_last + m_sc[s, h] - m_new)
        kv = lax.dot_general(v * w_s, k, (((0,), (0,)), ((), ())), preferred_element_type=F32)
        upds.append((decay, kv, jnp.sum(w_s * k, axis=0, keepdims=True), m_new))

    for idx, (s, h) in enumerate(chains):
        q, _, _ = qkv(s, h)
        a, rows = rows_of(s)
        num = svs[idx] + inters[idx] * qcs[idx]
        den = (jnp.sum(scs[idx], axis=-1, keepdims=True)
               + inters[idx] * jnp.sum(q * n_sc[s, h], axis=-1, keepdims=True))
        hh = num / jnp.maximum(jnp.abs(den), jnp.exp(-m_ts[idx]))
        hn = (hh * lax.rsqrt(jnp.mean(hh * hh, axis=-1, keepdims=True) + RMS_EPS)
              * gn_ref[:, h * M_DV:(h + 1) * M_DV])
        h_ref[s, :, h * M_DV:(h + 1) * M_DV] = hn * _sigmoid(
            o_all[a][rows, h * M_DV:(h + 1) * M_DV])

    for (s, h), (decay, kv, kn, m_new) in zip(chains, upds):
        c_sc[s, h] = decay * c_sc[s, h] + kv
        n_sc[s, h] = decay * n_sc[s, h] + kn
        m_sc[s, h] = m_new

    @pl.when(ci == pl.num_programs(1) - 1)
    def _():
        c1_ref[0] = c_sc[...]
        n1_ref[...] = n_sc[...]
        m1_ref[...] = m_sc[...]


def _mlstm(p_mix, gates, c0_all, n0, m0, gn, c_buf, *, layer, depth, row0, batch, seq, name):
    layer_in = min(layer, c0_all.shape[0] - 1)
    L = M_CHUNK if seq % M_CHUNK == 0 else seq
    nc = seq // L
    slots, per_slot = (_pick_tile(batch, (2, 1)), 1) if nc > 1 else (1, _pick_tile(batch, (4, 2, 1)))
    nseq = slots * per_slot
    rows = per_slot * L
    rb0 = row0 // rows
    assert row0 % rows == 0
    qk_w = M_HEADS * M_DK
    v_w = M_HEADS * M_DV
    assert qk_w * 2 == v_w

    def tok(a):
        return lambda bi, c: rb0 + (bi * slots + a) * nc + c

    def tok_spec(a, width, col_blk):
        return pl.BlockSpec((rows, width), lambda bi, c: (tok(a)(bi, c), col_blk))

    state_map = lambda bi, c: (bi, 0, 0, 0)
    c_blk = (1, nseq, M_HEADS, M_DV, M_DK)
    in_specs, args = [], []
    for a in range(slots):
        in_specs += [tok_spec(a, qk_w, 0), tok_spec(a, qk_w, 1), tok_spec(a, v_w, 1),
                     tok_spec(a, v_w, 2), tok_spec(a, LANES, 0)]
        args += [p_mix, p_mix, p_mix, p_mix, gates]
    in_specs += [
        pl.BlockSpec(c_blk, lambda bi, c: (layer_in, bi, 0, 0, 0)),
        pl.BlockSpec((nseq, M_HEADS, 1, M_DK), state_map),
        pl.BlockSpec((nseq, M_HEADS, 1, 1), state_map),
        pl.BlockSpec((1, v_w), lambda bi, c: (0, 0)),
    ]
    args += [c0_all, n0.reshape(batch, M_HEADS, 1, M_DK), m0.reshape(batch, M_HEADS, 1, 1), gn]
    aliases = {}
    if c_buf is not None:
        aliases[len(args)] = 1
        in_specs.append(ANY_SPEC)
        args.append(c_buf)
    blocks = (slots * (2 * _nbytes((rows, qk_w), F32) + 3 * _nbytes((rows, v_w), F32))
              + 3 * _nbytes((nseq, M_HEADS, M_DV, M_DK), F32))
    return pl.pallas_call(
        functools.partial(_mlstm_kernel, chunk=L, slots=slots, per_slot=per_slot),
        name=name,
        grid=(batch // nseq, nc),
        in_specs=in_specs,
        out_specs=[
            pl.BlockSpec((nseq, L, v_w), lambda bi, c: (bi, c, 0)),
            pl.BlockSpec(c_blk, lambda bi, c: (layer, bi, 0, 0, 0)),
            pl.BlockSpec((nseq, M_HEADS, 1, M_DK), state_map),
            pl.BlockSpec((nseq, M_HEADS, 1, 1), state_map),
        ],
        out_shape=[
            jax.ShapeDtypeStruct((batch, seq, v_w), F32),
            jax.ShapeDtypeStruct((depth, batch, M_HEADS, M_DV, M_DK), F32),
            jax.ShapeDtypeStruct((batch, M_HEADS, 1, M_DK), F32),
            jax.ShapeDtypeStruct((batch, M_HEADS, 1, 1), F32),
        ],
        scratch_shapes=[
            pltpu.VMEM((nseq, M_HEADS, M_DV, M_DK), F32),
            pltpu.VMEM((nseq, M_HEADS, 1, M_DK), F32),
            pltpu.VMEM((nseq, M_HEADS, 1, 1), F32),
        ],
        input_output_aliases=aliases,
        compiler_params=_params(("arbitrary", "arbitrary"), blocks),
    )(*args)


def _seq_rows(x, offset, *, seq):
    rows = x.shape[0]
    i = lax.broadcasted_iota(jnp.int32, (rows // seq, rows), 0)
    r = lax.broadcasted_iota(jnp.int32, (rows // seq, rows), 1)
    pick = (r == i * seq + offset).astype(F32)
    return jnp.dot(pick, x, precision=HI, preferred_element_type=F32)


def _seq_place(per_seq, offset, *, seq):
    nseq = per_seq.shape[0]
    r = lax.broadcasted_iota(jnp.int32, (nseq * seq, nseq), 0)
    i = lax.broadcasted_iota(jnp.int32, (nseq * seq, nseq), 1)
    pick = (r == i * seq + offset).astype(F32)
    return jnp.dot(pick, per_seq, precision=HI, preferred_element_type=F32)


def _shift_rows(x, shift, *, seq, tile_has_whole_seqs, before):
    rows = x.shape[0]
    rolled = pltpu.roll(x, shift, 0)
    r = lax.broadcasted_iota(jnp.int32, (rows, 1), 0)
    if tile_has_whole_seqs:
        head = _seq_place(before[0], 0, seq=seq)
        for s in range(1, shift):
            head = head + _seq_place(before[s], s, seq=seq)
        return jnp.where((r % seq) < shift, head, rolled)
    out = rolled
    for s in range(shift):
        out = jnp.where(r == s, before[s:s + 1, :], out)
    return out


def _scan_block(seq):
    return _pick_tile(seq, (32, 16, 8))


def _hk_to_kh(a, axis):
    shape = a.shape
    a = a.reshape(shape[:axis] + (R_HEADS, R_HD) + shape[axis + 1:])
    return jnp.swapaxes(a, axis, axis + 1).reshape(shape)


def _kh_to_hk(a, axis):
    shape = a.shape
    a = a.reshape(shape[:axis] + (R_HD, R_HEADS) + shape[axis + 1:])
    return jnp.swapaxes(a, axis, axis + 1).reshape(shape)


def _sections(a, axis, fn):
    rw = R_HEADS * R_HD
    idx = lambda lo, hi: tuple(slice(lo, hi) if d == axis else slice(None) for d in range(a.ndim))
    parts = [fn(a[idx(i * rw, (i + 1) * rw)], axis) for i in range(3)]
    if a.shape[axis] > 3 * rw:
        parts.append(a[idx(3 * rw, a.shape[axis])])
    return jnp.concatenate(parts, axis=axis)


def _rwkv_prep_kernel(x_ref, gn_ref, wp_ref, bp_ref, st_ref, mu_ref, wl_ref, w0_ref, a0_ref,
                      rr_ref, rk_ref, rv_ref, dec_ref, a_ref, g_ref, sh_ref, carry_sc,
                      *, seq, whole_seqs):
    xin = x_ref[...]
    tb = xin.shape[0]
    rw = R_HEADS * R_HD
    ms = jnp.mean(xin * xin, axis=-1, keepdims=True)
    h = (xin * lax.rsqrt(ms + RMS_EPS) * gn_ref[...]).astype(BF16)
    if not whole_seqs:
        @pl.when(pl.program_id(1) == 0)
        def _():
            carry_sc[...] = st_ref[0]

    def proj(lo, hi):
        return jnp.dot(h, wp_ref[:, lo:hi], preferred_element_type=F32) + bp_ref[:, lo:hi]

    def shifted(x, lo, hi):
        if whole_seqs:
            prev = _shift_rows(x, 1, seq=seq, tile_has_whole_seqs=True, before=[st_ref[:, lo:hi]])
            sh_ref[:, lo:hi] = _seq_rows(x, seq - 1, seq=seq)
        else:
            prev = _shift_rows(x, 1, seq=seq, tile_has_whole_seqs=False,
                               before=carry_sc[:, lo:hi])
            carry_sc[:, lo:hi] = x[tb - 1:tb, :]
            sh_ref[0, :, lo:hi] = x[tb - 1:tb, :]
        return x + (prev - x) * mu_ref[:, lo:hi]

    x_lora = proj(3 * rw, 3 * rw + R_LORA)
    x_r = proj(0, rw)
    z = shifted(x_lora, 3 * rw, 3 * rw + R_LORA)
    lane = lax.broadcasted_iota(jnp.int32, (1, R_LORA), 1)
    act = jnp.where(lane < 64, jnp.tanh(z), jnp.where(lane < 128, z, _sigmoid(z))).astype(BF16)
    lo_w = jnp.dot(act, wl_ref[:, 0:rw], preferred_element_type=F32)
    x_k = proj(rw, 2 * rw)
    rr_ref[...] = shifted(x_r, 0, rw)
    lo_a = jnp.dot(act, wl_ref[:, rw:2 * rw], preferred_element_type=F32)
    x_v = proj(2 * rw, 3 * rw)
    rk_ref[...] = shifted(x_k, rw, 2 * rw)
    dec_ref[...] = jnp.exp(-jnp.exp(-_softplus(-(w0_ref[...] + lo_w)) - 0.5))
    g_ref[...] = jnp.dot(act, wl_ref[:, 2 * rw:3 * rw], preferred_element_type=F32)
    rv_ref[...] = shifted(x_v, 2 * rw, 3 * rw)
    a_ref[...] = _sigmoid(a0_ref[...] + lo_a)


def _rwkv_prep(x, gn, w_rw, b_rw, shift_state, mu, w_lora, w0, a0, *, row0, batch, seq, name):
    d = x.shape[1]
    c = w_rw.shape[1]
    rw = R_HEADS * R_HD
    n_tok = batch * seq
    whole = seq <= 64
    if whole:
        tb = _pick_tile(n_tok, (256, 128, 64, 32, 16, 8))
        assert tb % seq == 0 and row0 % tb == 0
        grid = (n_tok // tb, 1)
        rb0 = row0 // tb
        tok_map = lambda i, j: (rb0 + i, 0)
        out_map = lambda i, j: (i, 0)
        state = shift_state
        st_spec = pl.BlockSpec((tb // seq, c), out_map)
        sh_shape = jax.ShapeDtypeStruct((batch, c), F32)
    else:
        tb = _pick_tile(seq, (256, 128, 64, 32, 16, 8))
        nt = seq // tb
        grid = (batch, nt)
        rb0 = row0 // tb
        tok_map = lambda b, j: (rb0 + b * nt + j, 0)
        out_map = lambda b, j: (b * nt + j, 0)
        state = shift_state.reshape(batch, 1, c)
        st_spec = pl.BlockSpec((1, 1, c), lambda b, j: (b, 0, 0))
        sh_shape = jax.ShapeDtypeStruct((batch, 1, c), F32)
    fixed = lambda i, j: (0, 0)
    tok_out = jax.ShapeDtypeStruct((n_tok, rw), F32)
    tok_spec = pl.BlockSpec((tb, rw), out_map)
    blocks = (4 * _nbytes((tb, c), F32) + 6 * _nbytes((tb, rw), F32) + _nbytes(w_lora.shape, BF16)
              + _nbytes(w_rw.shape, BF16))
    outs = pl.pallas_call(
        functools.partial(_rwkv_prep_kernel, seq=seq, whole_seqs=whole),
        name=name,
        grid=grid,
        in_specs=[
            pl.BlockSpec((tb, d), tok_map),
            pl.BlockSpec((1, d), fixed),
            pl.BlockSpec(w_rw.shape, fixed),
            pl.BlockSpec((1, c), fixed),
            st_spec,
            pl.BlockSpec((1, c), fixed),
            pl.BlockSpec(w_lora.shape, fixed),
            pl.BlockSpec((1, rw), fixed),
            pl.BlockSpec((1, rw), fixed),
        ],
        out_specs=[tok_spec] * 6 + [st_spec],
        out_shape=[tok_out] * 6 + [sh_shape],
        scratch_shapes=[pltpu.VMEM((1, c), F32)],
        compiler_params=_params(("arbitrary", "arbitrary"), blocks),
    )(x, gn, w_rw, b_rw, state, mu, w_lora, w0, a0)
    return (*outs[:6], outs[6].reshape(batch, c))


def _rwkv_scan_kernel(rr_ref, rk_ref, dec_ref, a_ref, rv_ref, kks_ref, ka_ref, rrk_ref,
                      gng_ref, gnb_ref, s0_ref, y_ref, s1_ref,
                      s_sc, w_sc, kka_sc, km_sc, nkk_sc, wr_sc, coef_sc, *, tb):
    j = pl.program_id(1)
    nk = R_HD

    @pl.when(j == 0)
    def _():
        s_sc[...] = s0_ref[0]

    rr = rr_ref[...]
    rk = rk_ref[...]
    a = a_ref[...]
    dec = dec_ref[...]
    kk = rk * kks_ref[...]
    nrm = jnp.sqrt(jnp.sum(kk * kk, axis=0, keepdims=True))
    kk = kk * (1.0 / jnp.maximum(nrm, 1e-12))
    kmod = rk * (1.0 + (a - 1.0) * ka_ref[...])
    kka = kk * a
    g2 = dec.reshape(nk * tb, LANES)
    tpos = lax.broadcasted_iota(jnp.int32, (nk * tb, 1), 0) % tb
    shift = 1
    while shift < tb:
        g2 = g2 * jnp.where(tpos >= shift, pltpu.roll(g2, shift, 0), 1.0)
        shift *= 2
    g_prev = jnp.where(tpos >= 1, pltpu.roll(g2, 1, 0), 1.0).reshape(nk, tb, LANES)
    g_run = g2.reshape(nk, tb, LANES)
    g_inv = 1.0 / g_run
    w_sc[...] = g_run
    kka_sc[...] = kka * g_inv
    km_sc[...] = kmod * g_inv
    nkk_sc[...] = -kk * g_prev
    wr_sc[...] = rr * g_run
    coef_sc[0] = jnp.sum(kka * rr, axis=0)
    coef_sc[1] = jnp.sum(kmod * rr, axis=0)
    coef_sc[2] = jnp.sum(rr * kmod * rrk_ref[...], axis=0)

    def row(ref, t, k):
        return jnp.broadcast_to(ref[k, pl.ds(t, 1), :], (nk, LANES))

    def coef(i, t):
        return jnp.broadcast_to(coef_sc[i, pl.ds(t, 1), :], (nk, LANES))

    def matvecs(t):
        acc_sa = jnp.zeros((nk, LANES), F32)
        acc_y = jnp.zeros((nk, LANES), F32)
        for k in range(nk):
            sk = s_sc[k]
            acc_sa = acc_sa + sk * row(nkk_sc, t, k)
            acc_y = acc_y + sk * row(wr_sc, t, k)
        return acc_sa, acc_y

    def emit(t, sa, y0):
        val = rv_ref[t]
        y = y0 + sa * coef(0, t) + val * coef(1, t)
        mean = jnp.sum(y, axis=0, keepdims=True) * (1.0 / nk)
        d = y - mean
        var = jnp.sum(d * d, axis=0, keepdims=True) * (1.0 / nk)
        yn = d * lax.rsqrt(var + GN_EPS) * gng_ref[...] + gnb_ref[...]
        y_ref[t] = yn + coef(2, t) * val
        return val

    def step(t, carry):
        sa, y0 = carry
        val = emit(t, sa, y0)
        acc_sa = jnp.zeros((nk, LANES), F32)
        acc_y = jnp.zeros((nk, LANES), F32)
        for k in range(nk):
            sk = s_sc[k] + sa * row(kka_sc, t, k) + val * row(km_sc, t, k)
            s_sc[k] = sk
            acc_sa = acc_sa + sk * row(nkk_sc, t + 1, k)
            acc_y = acc_y + sk * row(wr_sc, t + 1, k)
        return acc_sa, acc_y

    sa, y0 = lax.fori_loop(0, tb - 1, step, matvecs(0))
    t_last = tb - 1
    val = emit(t_last, sa, y0)
    for k in range(nk):
        s_sc[k] = ((s_sc[k] + sa * row(kka_sc, t_last, k) + val * row(km_sc, t_last, k))
                   * row(w_sc, t_last, k))

    @pl.when(j == pl.num_programs(1) - 1)
    def _():
        s1_ref[0] = s_sc[...]


LAYOUT_TB = 128


def _lane_slabs(in_ref, p):
    nb = HEADS_PER_LANE_GROUP
    zts = [in_ref[b, :, p * LANES:(p + 1) * LANES].T for b in range(nb)]
    for i in range(SUBLANES):
        d = jnp.concatenate([zt[i * R_HEADS:(i + 1) * R_HEADS, :] for zt in zts], axis=0)
        yield SUBLANES * p + i, d.T


def _lanes_in_kernel(src_ref, dst_ref, *, rows):
    for p in range(R_HD // SUBLANES):
        for i, slab in _lane_slabs(src_ref, p):
            if rows:
                dst_ref[i] = slab
            else:
                dst_ref[:, pl.ds(i, 1), :] = slab[:, None, :]


def _lanes_in(x, *, batch, seq, rows):
    nb = HEADS_PER_LANE_GROUP
    g = batch // nb
    w = R_HEADS * R_HD
    if seq % LAYOUT_TB != 0 and (g * seq) % LAYOUT_TB == 0:
        x = x.reshape(g, nb, seq, w).transpose(1, 0, 2, 3).reshape(nb * g * seq, w)
        return _lanes_in(x, batch=nb, seq=g * seq, rows=rows)
    if seq % LAYOUT_TB != 0:
        x = x.reshape(g, nb, seq, R_HD, R_HEADS)
        if rows:
            return x.transpose(3, 0, 2, 1, 4).reshape(R_HD, g * seq, LANES)
        return x.transpose(0, 2, 3, 1, 4).reshape(g * seq, R_HD, LANES)
    nt = seq // LAYOUT_TB
    if rows:
        out_spec = pl.BlockSpec((R_HD, LAYOUT_TB, LANES), lambda gi, j: (0, gi * nt + j, 0))
        out_shape = jax.ShapeDtypeStruct((R_HD, g * seq, LANES), F32)
    else:
        out_spec = pl.BlockSpec((LAYOUT_TB, R_HD, LANES), lambda gi, j: (gi * nt + j, 0, 0))
        out_shape = jax.ShapeDtypeStruct((g * seq, R_HD, LANES), F32)
    return pl.pallas_call(
        functools.partial(_lanes_in_kernel, rows=rows),
        name="lanes_in_rows" if rows else "lanes_in_full",
        grid=(g, nt),
        in_specs=[pl.BlockSpec((nb, LAYOUT_TB, w), lambda gi, j: (gi, j, 0))],
        out_specs=out_spec,
        out_shape=out_shape,
        compiler_params=_params(("arbitrary", "arbitrary"), 2 * _nbytes((nb, LAYOUT_TB, w), F32)),
    )(x.reshape(batch, seq, w))


def _lanes_out_kernel(y_ref, o_ref):
    nb = HEADS_PER_LANE_GROUP
    for p in range(R_HD // SUBLANES):
        pieces = [[] for _ in range(nb)]
        for i in range(SUBLANES):
            st = y_ref[:, SUBLANES * p + i, :].T
            for b in range(nb):
                pieces[b].append(st[b * R_HEADS:(b + 1) * R_HEADS, :])
        for b in range(nb):
            zt = jnp.concatenate(pieces[b], axis=0)
            o_ref[b, :, p * LANES:(p + 1) * LANES] = zt.T


def _lanes_out(y, *, batch, seq):
    nb = HEADS_PER_LANE_GROUP
    g = batch // nb
    w = R_HEADS * R_HD
    if seq % LAYOUT_TB != 0 and (g * seq) % LAYOUT_TB == 0:
        tok = _lanes_out(y, batch=nb, seq=g * seq)
        return tok.reshape(nb, g, seq, w).transpose(1, 0, 2, 3).reshape(batch * seq, w)
    if seq % LAYOUT_TB != 0:
        y = y.reshape(g, seq, R_HD, nb, R_HEADS)
        return y.transpose(0, 3, 1, 2, 4).reshape(batch * seq, w)
    nt = seq // LAYOUT_TB
    out = pl.pallas_call(
        _lanes_out_kernel,
        name="lanes_out",
        grid=(g, nt),
        in_specs=[pl.BlockSpec((LAYOUT_TB, R_HD, LANES), lambda gi, j: (gi * nt + j, 0, 0))],
        out_specs=pl.BlockSpec((nb, LAYOUT_TB, w), lambda gi, j: (gi, j, 0)),
        out_shape=jax.ShapeDtypeStruct((batch, seq, w), F32),
        compiler_params=_params(("arbitrary", "arbitrary"), 2 * _nbytes((nb, LAYOUT_TB, w), F32)),
    )(y)
    return out.reshape(batch * seq, w)


def _param_lanes(p):
    p = p.reshape(R_HEADS, R_HD).T
    return jnp.tile(p, (1, HEADS_PER_LANE_GROUP))


def _rwkv_scan(rr, rk, rv, dec, a, kks, ka, rrk, gng, gnb, s0, *, batch, seq, name):
    assert batch % HEADS_PER_LANE_GROUP == 0
    g = batch // HEADS_PER_LANE_GROUP
    tb = _scan_block(seq)
    nt = seq // tb
    rows = [_lanes_in(v, batch=batch, seq=seq, rows=True) for v in (rr, rk, dec, a)]
    val = _lanes_in(rv, batch=batch, seq=seq, rows=False)
    s0l = s0.reshape(g, HEADS_PER_LANE_GROUP, R_HEADS, R_HD, R_HD).transpose(0, 4, 3, 1, 2)
    s0l = s0l.reshape(g, R_HD, R_HD, LANES)
    row_spec = pl.BlockSpec((R_HD, tb, LANES), lambda gi, j: (0, gi * nt + j, 0))
    val_spec = pl.BlockSpec((tb, R_HD, LANES), lambda gi, j: (gi * nt + j, 0, 0))
    kpar_spec = pl.BlockSpec((R_HD, 1, LANES), lambda gi, j: (0, 0, 0))
    vpar_spec = pl.BlockSpec((R_HD, LANES), lambda gi, j: (0, 0))
    st_spec = pl.BlockSpec((1, R_HD, R_HD, LANES), lambda gi, j: (gi, 0, 0, 0))
    vec_bytes = _nbytes((tb, R_HD, LANES), F32)
    blocks = 11 * vec_bytes + 3 * _nbytes((R_HD, R_HD, LANES), F32)
    kpar = lambda p: _param_lanes(p)[:, None, :]
    y, s1l = pl.pallas_call(
        functools.partial(_rwkv_scan_kernel, tb=tb),
        name=name,
        grid=(g, nt),
        in_specs=[row_spec] * 4 + [val_spec] + [kpar_spec] * 3 + [vpar_spec] * 2 + [st_spec],
        out_specs=[val_spec, st_spec],
        out_shape=[
            jax.ShapeDtypeStruct((g * seq, R_HD, LANES), F32),
            jax.ShapeDtypeStruct(s0l.shape, F32),
        ],
        scratch_shapes=[pltpu.VMEM((R_HD, R_HD, LANES), F32)]
        + [pltpu.VMEM((R_HD, tb, LANES), F32)] * 5
        + [pltpu.VMEM((3, tb, LANES), F32)],
        compiler_params=_params(("arbitrary", "arbitrary"), blocks),
    )(*rows, val, kpar(kks), kpar(ka), kpar(rrk), _param_lanes(gng), _param_lanes(gnb), s0l)
    s1 = s1l.reshape(g, R_HD, R_HD, HEADS_PER_LANE_GROUP, R_HEADS).transpose(0, 3, 4, 2, 1)
    return _lanes_out(y, batch=batch, seq=seq), s1.reshape(batch, R_HEADS, R_HD, R_HD)


def _mix_out_kernel(hm_ref, yr_ref, g_ref, gm_ref, gr_ref, x_ref, wm_ref, wr_ref, wo_ref, *rest):
    o_ref = rest[-1]
    bm = jnp.dot(hm_ref[...].astype(BF16), wm_ref[...], preferred_element_type=F32)
    yr = (yr_ref[...] * g_ref[...]).astype(BF16)
    br = jnp.dot(yr, wr_ref[...], preferred_element_type=F32)
    merged = (_sigmoid(gm_ref[...].astype(F32)) * bm
              + _sigmoid(gr_ref[...].astype(F32)) * br)
    o_ref[...] = x_ref[...] + jnp.dot(merged.astype(BF16), wo_ref[...], preferred_element_type=F32)


def _mix_out(hm, yr, g, p_mix, x, wm, wr, wo, out_buf, *, row0, name):
    n_all, d = x.shape
    n = yr.shape[0]
    tm = _pick_tile(n, (256, 128, 64, 32, 16, 8))
    assert row0 % tm == 0
    rb0 = row0 // tm
    gm_blk = (2 * M_HEADS * (M_DK + M_DV)) // d
    shared = lambda i: (rb0 + i, 0)
    own = lambda i: (i, 0)
    fixed = lambda i: (0, 0)
    aliased = out_buf is not None
    in_specs = [
        pl.BlockSpec((tm, wm.shape[0]), own),
        pl.BlockSpec((tm, wr.shape[0]), own),
        pl.BlockSpec((tm, wr.shape[0]), own),
        pl.BlockSpec((tm, d), lambda i: (rb0 + i, gm_blk)),
        pl.BlockSpec((tm, d), lambda i: (rb0 + i, gm_blk + 1)),
        pl.BlockSpec((tm, d), shared),
        pl.BlockSpec(wm.shape, fixed),
        pl.BlockSpec(wr.shape, fixed),
        pl.BlockSpec(wo.shape, fixed),
    ]
    args = [hm, yr, g, p_mix, p_mix, x, wm, wr, wo]
    if aliased:
        in_specs.append(ANY_SPEC)
        args.append(out_buf)
    blocks = 7 * _nbytes((tm, d), F32) + 3 * _nbytes((d, d), BF16)
    return pl.pallas_call(
        _mix_out_kernel,
        name=name,
        grid=(n // tm,),
        in_specs=in_specs,
        out_specs=pl.BlockSpec((tm, d), shared),
        out_shape=jax.ShapeDtypeStruct((n_all, d), F32),
        input_output_aliases={9: 0} if aliased else {},
        compiler_params=_params(("arbitrary",), blocks),
    )(*args)


FFN_CHUNK = 256


def _ffn_kernel(x_ref, gn_ref, wu_ref, cs_ref, cw_ref, cb_ref, wd_ref, *rest, seq, whole_seqs):
    o_ref, st_ref, carry_sc = rest[-3:]
    x = x_ref[...]
    tb = x.shape[0]
    dff = wd_ref.shape[0]
    ms = jnp.mean(x * x, axis=-1, keepdims=True)
    h = (x * lax.rsqrt(ms + RMS_EPS) * gn_ref[...]).astype(BF16)
    if not whole_seqs:
        @pl.when(pl.program_id(1) == 0)
        def _():
            carry_sc[...] = cs_ref[0]
    def up(c0):
        return [jnp.dot(h, wu_ref[:, lo:lo + FFN_CHUNK], preferred_element_type=F32)
                for lo in (c0, dff + c0)]

    def down(acc, act, c0):
        return acc + jnp.dot(act, wd_ref[c0:c0 + FFN_CHUNK, :], preferred_element_type=F32)

    acc = x
    starts = list(range(0, dff, FFN_CHUNK))
    u_next = up(starts[0])
    pending = None
    for ci, c0 in enumerate(starts):
        u_cur = u_next
        if ci + 1 < len(starts):
            u_next = up(starts[ci + 1])
        if pending is not None:
            acc = down(acc, *pending)
        halves = []
        for lo, u in zip((c0, dff + c0), u_cur):
            cols = slice(lo, lo + FFN_CHUNK)
            if whole_seqs:
                before = [cs_ref[0, :, cols], cs_ref[1, :, cols]]
                p1 = _shift_rows(u, 1, seq=seq, tile_has_whole_seqs=True, before=before[1:])
                p2 = _shift_rows(u, 2, seq=seq, tile_has_whole_seqs=True, before=before)
                st_ref[0, :, cols] = _seq_rows(u, seq - 2, seq=seq)
                st_ref[1, :, cols] = _seq_rows(u, seq - 1, seq=seq)
            else:
                carry = carry_sc[:, cols]
                p1 = _shift_rows(u, 1, seq=seq, tile_has_whole_seqs=False, before=carry[1:2, :])
                p2 = _shift_rows(u, 2, seq=seq, tile_has_whole_seqs=False, before=carry)
                carry_sc[:, cols] = u[tb - 2:tb, :]
                st_ref[0, :, cols] = u[tb - 2:tb, :]
            halves.append(cb_ref[:, cols] + p2 * cw_ref[0:1, cols] + p1 * cw_ref[1:2, cols]
                          + u * cw_ref[2:3, cols])
        gate, val = halves
        pending = ((gate * _sigmoid(gate) * val).astype(BF16), c0)
    o_ref[...] = down(acc, *pending)


def _ffn(x, gn, wu, conv_state, cw, cb, wd, out_buf, *, row0, batch, seq, name):
    n_all, d = x.shape
    c = wu.shape[1]
    n_tok = batch * seq
    taps = cw.shape[0]
    assert taps == 3 and seq >= taps - 1 and wd.shape[0] % FFN_CHUNK == 0
    whole = seq <= 64
    if whole:
        tb = _pick_tile(n_tok, (128, 64, 32, 16, 8))
        assert tb % seq == 0 and row0 % tb == 0
        grid = (n_tok // tb, 1)
        rb0 = row0 // tb
        tok_map = lambda i, j: (rb0 + i, 0)
        state = jnp.swapaxes(conv_state, 0, 1)
        st_shape = jax.ShapeDtypeStruct((taps - 1, batch, c), F32)
        st_spec = pl.BlockSpec((taps - 1, tb // seq, c), lambda i, j: (0, i, 0))
    else:
        tb = _pick_tile(seq, (256, 128, 64, 32, 16, 8))
        nt = seq // tb
        grid = (batch, nt)
        rb0 = row0 // tb
        tok_map = lambda b, j: (rb0 + b * nt + j, 0)
        state = conv_state
        st_shape = jax.ShapeDtypeStruct((batch, taps - 1, c), F32)
        st_spec = pl.BlockSpec((1, taps - 1, c), lambda b, j: (b, 0, 0))
    fixed = lambda i, j: (0, 0)
    aliased = out_buf is not None
    in_specs = [
        pl.BlockSpec((tb, d), tok_map),
        pl.BlockSpec((1, d), fixed),
        pl.BlockSpec(wu.shape, fixed),
        st_spec,
        pl.BlockSpec(cw.shape, fixed),
        pl.BlockSpec((1, c), fixed),
        pl.BlockSpec(wd.shape, fixed),
    ]
    args = [x, gn, wu, state, cw, cb, wd]
    if aliased:
        in_specs.append(ANY_SPEC)
        args.append(out_buf)
    blocks = _nbytes(wu.shape, BF16) + _nbytes(wd.shape, BF16) + 6 * _nbytes((tb, d), F32)
    y, st = pl.pallas_call(
        functools.partial(_ffn_kernel, seq=seq, whole_seqs=whole),
        name=name,
        grid=grid,
        in_specs=in_specs,
        out_specs=[pl.BlockSpec((tb, d), tok_map), st_spec],
        out_shape=[jax.ShapeDtypeStruct((n_all, d), F32), st_shape],
        scratch_shapes=[pltpu.VMEM((taps - 1, c), F32)],
        input_output_aliases={7: 0} if aliased else {},
        compiler_params=_params(("arbitrary", "arbitrary"), blocks),
    )(*args)
    return y, (jnp.swapaxes(st, 0, 1) if whole else st)


def _final_norm_kernel(x_ref, g_ref, o_ref):
    x = x_ref[...]
    ms = jnp.mean(x * x, axis=-1, keepdims=True)
    o_ref[...] = x * lax.rsqrt(ms + RMS_EPS) * g_ref[...]


def _final_norm(x, g, *, row0, n, name):
    d = x.shape[1]
    tm = _pick_tile(n, (1024, 512, 256, 128, 64, 32, 16, 8))
    assert row0 % tm == 0
    rb0 = row0 // tm
    return pl.pallas_call(
        _final_norm_kernel,
        name=name,
        grid=(n // tm,),
        in_specs=[pl.BlockSpec((tm, d), lambda i: (rb0 + i, 0)),
                  pl.BlockSpec((1, d), lambda i: (0, 0))],
        out_specs=pl.BlockSpec((tm, d), lambda i: (i, 0)),
        out_shape=jax.ShapeDtypeStruct((n, d), F32),
        compiler_params=_params(("arbitrary",), 2 * _nbytes((tm, d), F32)),
    )(x, g)


def kernel(x_prompt, x_sample, state_mlstm_C, state_mlstm_n, state_mlstm_m, state_rwkv_S, state_rwkv_shift, state_ffn_conv, norm_mix_g, w_in, b_in, mu_shift, m_norm_g, r_w0, r_w_up, r_a0, r_a_up, r_g_up, r_k_k, r_k_a, r_r_k, r_gn_g, r_gn_b, w_br_m, w_br_r, w_out, norm_ffn_g, w_up, conv_w, conv_b, w_down, norm_final_g):
    bp, tp, d = x_prompt.shape
    bs, ts, _ = x_sample.shape
    depth = w_in.shape[0]
    n_p = bp * tp
    n_s = bs * ts
    rw = R_HEADS * R_HD
    qkvo = 2 * M_HEADS * (M_DK + M_DV)
    n_gate = 2 * M_HEADS
    mix_cols = qkvo + n_gate + 2 * d
    rw_cols = 3 * rw + R_LORA
    dff2 = w_up.shape[2]
    assert w_in.shape[2] == mix_cols + rw_cols

    x = jnp.concatenate([x_prompt.reshape(n_p, d), x_sample.reshape(n_s, d)], axis=0)

    zc = jnp.zeros((1, bp, M_HEADS, M_DV, M_DK), F32)
    c_all_p = c_all_s = None
    zn = jnp.zeros((bp, M_HEADS, M_DK), F32)
    zm = jnp.zeros((bp, M_HEADS), F32)
    zs = jnp.zeros((bp, R_HEADS, R_HD, R_HD), F32)
    zsh = jnp.zeros((bp, rw_cols), F32)
    zcv = jnp.zeros((bp, conv_w.shape[1] - 1, dff2), F32)

    outs_p = [[] for _ in range(5)]
    outs_s = [[] for _ in range(5)]
    for l in range(depth):
        wl = w_in[l]
        bl = b_in[l]
        w_mix = jnp.concatenate([wl[:, :qkvo], wl[:, qkvo + n_gate:mix_cols]], axis=1).astype(BF16)
        b_mix = jnp.concatenate([bl[:qkvo], bl[qkvo + n_gate:mix_cols]])[None]
        w_gate = jnp.pad(wl[:, qkvo:qkvo + n_gate], ((0, 0), (0, LANES - n_gate))).astype(BF16)
        b_gate = jnp.pad(bl[qkvo:qkvo + n_gate], (0, LANES - n_gate))[None]
        w_rw = _sections(wl[:, mix_cols:], 1, _hk_to_kh).astype(BF16)
        b_rw = _sections(bl[mix_cols:], 0, _hk_to_kh)[None]
        mu_rw = _sections(mu_shift[l], 0, _hk_to_kh)[None]
        w_lora = jnp.zeros((R_LORA, 3 * rw), F32)
        w_lora = w_lora.at[0:64, 0:rw].set(r_w_up[l]).at[64:128, rw:2 * rw].set(r_a_up[l])
        w_lora = _sections(w_lora.at[128:256, 2 * rw:].set(r_g_up[l]), 1, _hk_to_kh).astype(BF16)
        w_br_r_kh = _hk_to_kh(w_br_r[l], 0).astype(BF16)
        g_mix = norm_mix_g[l][None]

        p_mix = _norm_matmul(x, g_mix, w_mix, b_mix, tn=1024, name="in_proj_mix", out_dtype=BF16)
        gates = _norm_matmul(x, g_mix, w_gate, b_gate, tn=LANES, name="in_proj_gates")

        gn = m_norm_g[l].reshape(1, M_HEADS * M_DV)
        hm_p, c_all_p, nn_p, m_p = _mlstm(
            p_mix, gates, zc, zn, zm, gn, c_all_p, layer=l, depth=depth,
            row0=0, batch=bp, seq=tp, name="mlstm_prompt")
        hm_s, c_all_s, nn_s, m_s = _mlstm(
            p_mix, gates, state_mlstm_C, state_mlstm_n[l], state_mlstm_m[l], gn, c_all_s,
            layer=l, depth=depth, row0=n_p, batch=bs, seq=ts, name="mlstm_sample")

        prep_args = (mu_rw, w_lora, _hk_to_kh(r_w0[l], 0)[None], _hk_to_kh(r_a0[l], 0)[None])
        rr_p, rk_p, rv_p, wp_p, ap_p, g_p, sh_p = _rwkv_prep(
            x, g_mix, w_rw, b_rw, zsh, *prep_args, row0=0, batch=bp, seq=tp,
            name="rwkv_prep_prompt")
        rr_s, rk_s, rv_s, wp_s, ap_s, g_s, sh_s = _rwkv_prep(
            x, g_mix, w_rw, b_rw, _sections(state_rwkv_shift[l], 1, _hk_to_kh), *prep_args,
            row0=n_p, batch=bs, seq=ts, name="rwkv_prep_sample")
        sh_p = _sections(sh_p, 1, _kh_to_hk)
        sh_s = _sections(sh_s, 1, _kh_to_hk)
        scan_args = (r_k_k[l], r_k_a[l], r_r_k[l], r_gn_g[l], r_gn_b[l])
        y_p, s_p = _rwkv_scan(rr_p, rk_p, rv_p, wp_p, ap_p, *scan_args, zs,
                              batch=bp, seq=tp, name="rwkv_scan_prompt")
        y_s, s_s = _rwkv_scan(rr_s, rk_s, rv_s, wp_s, ap_s, *scan_args, state_rwkv_S[l],
                              batch=bs, seq=ts, name="rwkv_scan_sample")

        w_branches = (w_br_m[l].astype(BF16), w_br_r_kh, w_out[l].astype(BF16))
        v_w = M_HEADS * M_DV
        x_mid = _mix_out(hm_p.reshape(n_p, v_w), y_p, g_p, p_mix, x, *w_branches, None, row0=0,
                         name="mix_out_prompt")
        x_mid = _mix_out(hm_s.reshape(n_s, v_w), y_s, g_s, p_mix, x, *w_branches, x_mid, row0=n_p,
                         name="mix_out_sample")

        ffn_w = (conv_w[l], conv_b[l][None], w_down[l].astype(BF16))
        g_ffn = norm_ffn_g[l][None]
        wu = w_up[l].astype(BF16)
        x, cv_p = _ffn(x_mid, g_ffn, wu, zcv, *ffn_w, None, row0=0, batch=bp, seq=tp,
                       name="ffn_prompt")
        x, cv_s = _ffn(x_mid, g_ffn, wu, state_ffn_conv[l], *ffn_w, x, row0=n_p, batch=bs, seq=ts,
                       name="ffn_sample")

        for lst, val in zip(outs_p, (nn_p.reshape(bp, M_HEADS, M_DK), m_p.reshape(bp, M_HEADS),
                                     s_p, sh_p, cv_p)):
            lst.append(val)
        for lst, val in zip(outs_s, (nn_s.reshape(bs, M_HEADS, M_DK), m_s.reshape(bs, M_HEADS),
                                     s_s, sh_s, cv_s)):
            lst.append(val)

    g_fin = norm_final_g[None]
    y_prompt = _final_norm(x, g_fin, row0=0, n=n_p, name="final_norm_prompt").reshape(bp, tp, d)
    y_sample = _final_norm(x, g_fin, row0=n_p, n=n_s, name="final_norm_sample").reshape(bs, ts, d)
    return (y_prompt, y_sample, c_all_p, *[jnp.stack(o) for o in outs_p],
            c_all_s, *[jnp.stack(o) for o in outs_s])
```

```python
import functools

import jax
import jax.numpy as jnp
from jax import lax
from jax.experimental import pallas as pl
from jax.experimental.pallas import tpu as pltpu

F32 = jnp.float32
BF16 = jnp.bfloat16

M_HEADS = 4
M_DK = 128
M_DV = 256
M_CHUNK = 64
R_HEADS = 16
R_HD = 64
R_LORA = 256
RMS_EPS = 1e-6
GN_EPS = 64e-5

LANES = 128
SUBLANES = 8
VMEM_LIMIT_CAP = 56 * 1024 * 1024
HEADS_PER_LANE_GROUP = LANES // R_HEADS

HI = lax.Precision.HIGHEST
ANY_SPEC = pl.BlockSpec(memory_space=pl.ANY)


def _vmem_limit(block_bytes):
    return int(min(VMEM_LIMIT_CAP, max(16 * 1024 * 1024, 3 * block_bytes)))


def _nbytes(shape, dtype):
    n = 1
    for s in shape:
        n *= s
    return n * jnp.dtype(dtype).itemsize


def _pick_tile(n, candidates):
    for c in candidates:
        if n % c == 0:
            return c
    raise ValueError(f"no tile in {candidates} divides {n}")


def _sigmoid(x):
    return 1.0 / (1.0 + jnp.exp(-x))


def _softplus(x):
    return jnp.maximum(x, 0.0) + jnp.log(1.0 + jnp.exp(-jnp.abs(x)))


def _params(semantics, block_bytes):
    return pltpu.CompilerParams(dimension_semantics=semantics,
                                vmem_limit_bytes=_vmem_limit(block_bytes))


def _in_proj_kernel(x_ref, g_ref, w_ref, b_ref, wg_ref, bg_ref, o_ref, og_ref, h_ref):
    @pl.when(pl.program_id(1) == 0)
    def _():
        x = x_ref[...]
        ms = jnp.mean(x * x, axis=-1, keepdims=True)
        h = (x * lax.rsqrt(ms + RMS_EPS) * g_ref[...]).astype(BF16)
        h_ref[...] = h
        og_ref[...] = jnp.dot(h, wg_ref[...], preferred_element_type=F32) + bg_ref[...]

    acc = jnp.dot(h_ref[...], w_ref[...], preferred_element_type=F32) + b_ref[...]
    o_ref[...] = acc.astype(o_ref.dtype)


def _in_proj(x, g, w, b, wg, bg, *, tn, name, out_dtype):
    n, d = x.shape
    c = w.shape[1]
    tm = _pick_tile(n, (1024, 512, 256, 128, 64, 32, 16, 8))
    blocks = (_nbytes((tm, d), F32) + _nbytes((d, tn), BF16) + _nbytes((tm, tn), F32)
              + _nbytes((tm, d), BF16) + _nbytes((d, LANES), BF16) + _nbytes((tm, LANES), F32))
    return pl.pallas_call(
        _in_proj_kernel,
        name=name,
        grid=(n // tm, c // tn),
        in_specs=[
            pl.BlockSpec((tm, d), lambda i, j: (i, 0)),
            pl.BlockSpec((1, d), lambda i, j: (0, 0)),
            pl.BlockSpec((d, tn), lambda i, j: (0, j)),
            pl.BlockSpec((1, tn), lambda i, j: (0, j)),
            pl.BlockSpec((d, LANES), lambda i, j: (0, 0)),
            pl.BlockSpec((1, LANES), lambda i, j: (0, 0)),
        ],
        out_specs=[pl.BlockSpec((tm, tn), lambda i, j: (i, j)),
                   pl.BlockSpec((tm, LANES), lambda i, j: (i, 0))],
        out_shape=[jax.ShapeDtypeStruct((n, c), out_dtype),
                   jax.ShapeDtypeStruct((n, LANES), F32)],
        scratch_shapes=[pltpu.VMEM((tm, d), BF16)],
        compiler_params=_params(("arbitrary", "arbitrary"), blocks),
    )(x, g, w, b, wg, bg)


def _mlstm_kernel(*refs, chunk, slots, per_slot):
    tok_refs = [refs[5 * a:5 * a + 5] for a in range(slots)]
    c0_ref, n0_ref, m0_ref, gn_ref = refs[5 * slots:5 * slots + 4]
    h_ref, c1_ref, n1_ref, m1_ref, c_sc, n_sc, m_sc = refs[-7:]
    L = chunk
    nseq = slots * per_slot
    ci = pl.program_id(1)

    @pl.when(ci == 0)
    def _():
        c_sc[...] = c0_ref[0]
        n_sc[...] = n0_ref[...]
        m_sc[...] = m0_ref[...]

    lane = lax.broadcasted_iota(jnp.int32, (1, LANES), 1)
    row = lax.broadcasted_iota(jnp.int32, (L, L), 0)
    col = lax.broadcasted_iota(jnp.int32, (L, L), 1)
    causal = row >= col
    q_all = [r[0][...].astype(F32) * (M_DK ** -0.5) for r in tok_refs]
    k_all = [r[1][...].astype(F32) for r in tok_refs]
    v_all = [r[2][...].astype(F32) for r in tok_refs]
    o_all = [r[3][...].astype(F32) for r in tok_refs]

    def rows_of(s):
        return s // per_slot, slice((s % per_slot) * L, (s % per_slot + 1) * L)

    nt_dims = (((1,), (1,)), ((), ()))
    chains = [(s, h) for s in range(nseq) for h in range(M_HEADS)]

    zs, z_rows = [], []
    tri = causal.astype(F32)
    eye8 = (lax.broadcasted_iota(jnp.int32, (SUBLANES, LANES), 0)
            == lax.broadcasted_iota(jnp.int32, (SUBLANES, LANES), 1)).astype(F32)
    for s in range(nseq):
        a, rows = rows_of(s)
        gates = tok_refs[a][4][rows, :]
        b_all = jnp.dot(tri, -_softplus(-gates), precision=HI, preferred_element_type=F32)
        zs.append(jnp.where(lane < M_HEADS, gates, b_all))
    for s in range(nseq):
        z_rows.append(lax.dot_general(eye8, zs[s], nt_dims, precision=HI,
                                      preferred_element_type=F32))

    def qkv(s, h):
        a, rows = rows_of(s)
        return (q_all[a][rows, h * M_DK:(h + 1) * M_DK], k_all[a][rows, h * M_DK:(h + 1) * M_DK],
                v_all[a][rows, h * M_DV:(h + 1) * M_DV])

    raw, qcs = [], []
    for s, h in chains:
        q, k, _ = qkv(s, h)
        raw.append(lax.dot_general(q, k, nt_dims, preferred_element_type=F32))
        qcs.append(lax.dot_general(q, c_sc[s, h], nt_dims, preferred_element_type=F32))

    scs, inters, m_ts = [], [], []
    for idx, (s, h) in enumerate(chains):
        i_row = z_rows[s][h:h + 1, :]
        b_row = z_rows[s][M_HEADS + h:M_HEADS + h + 1, :]
        b_col = zs[s][:, M_HEADS + h:M_HEADS + h + 1]
        dmat = jnp.where(causal, b_col - b_row + i_row, -jnp.inf)
        m_inter = b_col + m_sc[s, h]
        m_t = jnp.maximum(m_inter, jnp.max(dmat, axis=-1, keepdims=True))
        scs.append(raw[idx] * jnp.exp(dmat - m_t))
        inters.append(jnp.exp(m_inter - m_t))
        m_ts.append(m_t)

    svs = [jnp.dot(scs[idx], qkv(s, h)[2], preferred_element_type=F32)
           for idx, (s, h) in enumerate(chains)]

    upds = []
    for idx, (s, h) in enumerate(chains):
        _, k, v = qkv(s, h)
        i_col = zs[s][:, h:h + 1]
        b_col = zs[s][:, M_HEADS + h:M_HEADS + h + 1]
        m_new = m_ts[idx][L - 1:L, :]
        b_last = b_col[L - 1:L, :]
        w_s = jnp.exp(b_last - b_col + i_col - m_new)
        decay = jnp.exp(b_last + m_sc[s, h] - m_new)
        kv = lax.dot_general(v * w_s, k, (((0,), (0,)), ((), ())), preferred_element_type=F32)
        upds.append((decay, kv, jnp.sum(w_s * k, axis=0, keepdims=True), m_new))

    for idx, (s, h) in enumerate(chains):
        q, _, _ = qkv(s, h)
        a, rows = rows_of(s)
        num = svs[idx] + inters[idx] * qcs[idx]
        den = (jnp.sum(scs[idx], axis=-1, keepdims=True)
               + inters[idx] * jnp.sum(q * n_sc[s, h], axis=-1, keepdims=True))
        hh = num / jnp.maximum(jnp.abs(den), jnp.exp(-m_ts[idx]))
        hn = (hh * lax.rsqrt(jnp.mean(hh * hh, axis=-1, keepdims=True) + RMS_EPS)
              * gn_ref[:, h * M_DV:(h + 1) * M_DV])
        h_ref[s, :, h * M_DV:(h + 1) * M_DV] = hn * _sigmoid(
            o_all[a][rows, h * M_DV:(h + 1) * M_DV])

    for (s, h), (decay, kv, kn, m_new) in zip(chains, upds):
        c_sc[s, h] = decay * c_sc[s, h] + kv
        n_sc[s, h] = decay * n_sc[s, h] + kn
        m_sc[s, h] = m_new

    @pl.when(ci == pl.num_programs(1) - 1)
    def _():
        c1_ref[0] = c_sc[...]
        n1_ref[...] = n_sc[...]
        m1_ref[...] = m_sc[...]


def _mlstm(p_mix, gates, c0_all, n0, m0, gn, c_buf, *, layer, depth, row0, batch, seq, name):
    layer_in = min(layer, c0_all.shape[0] - 1)
    L = M_CHUNK if seq % M_CHUNK == 0 else seq
    nc = seq // L
    slots, per_slot = (_pick_tile(batch, (2, 1)), 1) if nc > 1 else (1, _pick_tile(batch, (4, 2, 1)))
    nseq = slots * per_slot
    rows = per_slot * L
    rb0 = row0 // rows
    assert row0 % rows == 0
    qk_w = M_HEADS * M_DK
    v_w = M_HEADS * M_DV
    assert qk_w * 2 == v_w

    def tok(a):
        return lambda bi, c: rb0 + (bi * slots + a) * nc + c

    def tok_spec(a, width, col_blk):
        return pl.BlockSpec((rows, width), lambda bi, c: (tok(a)(bi, c), col_blk))

    state_map = lambda bi, c: (bi, 0, 0, 0)
    c_blk = (1, nseq, M_HEADS, M_DV, M_DK)
    in_specs, args = [], []
    for a in range(slots):
        in_specs += [tok_spec(a, qk_w, 0), tok_spec(a, qk_w, 1), tok_spec(a, v_w, 1),
                     tok_spec(a, v_w, 2), tok_spec(a, LANES, 0)]
        args += [p_mix, p_mix, p_mix, p_mix, gates]
    in_specs += [
        pl.BlockSpec(c_blk, lambda bi, c: (layer_in, bi, 0, 0, 0)),
        pl.BlockSpec((nseq, M_HEADS, 1, M_DK), state_map),
        pl.BlockSpec((nseq, M_HEADS, 1, 1), state_map),
        pl.BlockSpec((1, v_w), lambda bi, c: (0, 0)),
    ]
    args += [c0_all, n0.reshape(batch, M_HEADS, 1, M_DK), m0.reshape(batch, M_HEADS, 1, 1), gn]
    aliases = {}
    if c_buf is not None:
        aliases[len(args)] = 1
        in_specs.append(ANY_SPEC)
        args.append(c_buf)
    blocks = (slots * (2 * _nbytes((rows, qk_w), F32) + 3 * _nbytes((rows, v_w), F32))
              + 3 * _nbytes((nseq, M_HEADS, M_DV, M_DK), F32))
    return pl.pallas_call(
        functools.partial(_mlstm_kernel, chunk=L, slots=slots, per_slot=per_slot),
        name=name,
        grid=(batch // nseq, nc),
        in_specs=in_specs,
        out_specs=[
            pl.BlockSpec((nseq, L, v_w), lambda bi, c: (bi, c, 0)),
            pl.BlockSpec(c_blk, lambda bi, c: (layer, bi, 0, 0, 0)),
            pl.BlockSpec((nseq, M_HEADS, 1, M_DK), state_map),
            pl.BlockSpec((nseq, M_HEADS, 1, 1), state_map),
        ],
        out_shape=[
            jax.ShapeDtypeStruct((batch, seq, v_w), F32),
            jax.ShapeDtypeStruct((depth, batch, M_HEADS, M_DV, M_DK), F32),
            jax.ShapeDtypeStruct((batch, M_HEADS, 1, M_DK), F32),
            jax.ShapeDtypeStruct((batch, M_HEADS, 1, 1), F32),
        ],
        scratch_shapes=[
            pltpu.VMEM((nseq, M_HEADS, M_DV, M_DK), F32),
            pltpu.VMEM((nseq, M_HEADS, 1, M_DK), F32),
            pltpu.VMEM((nseq, M_HEADS, 1, 1), F32),
        ],
        input_output_aliases=aliases,
        compiler_params=_params(("arbitrary", "arbitrary"), blocks),
    )(*args)


def _seq_rows(x, offset, *, seq):
    rows = x.shape[0]
    i = lax.broadcasted_iota(jnp.int32, (rows // seq, rows), 0)
    r = lax.broadcasted_iota(jnp.int32, (rows // seq, rows), 1)
    pick = (r == i * seq + offset).astype(F32)
    return jnp.dot(pick, x, precision=HI, preferred_element_type=F32)


def _seq_place(per_seq, offset, *, seq):
    nseq = per_seq.shape[0]
    r = lax.broadcasted_iota(jnp.int32, (nseq * seq, nseq), 0)
    i = lax.broadcasted_iota(jnp.int32, (nseq * seq, nseq), 1)
    pick = (r == i * seq + offset).astype(F32)
    return jnp.dot(pick, per_seq, precision=HI, preferred_element_type=F32)


def _shift_rows(x, shift, *, seq, tile_has_whole_seqs, before):
    rows = x.shape[0]
    rolled = pltpu.roll(x, shift, 0)
    r = lax.broadcasted_iota(jnp.int32, (rows, 1), 0)
    if tile_has_whole_seqs:
        head = _seq_place(before[0], 0, seq=seq)
        for s in range(1, shift):
            head = head + _seq_place(before[s], s, seq=seq)
        return jnp.where((r % seq) < shift, head, rolled)
    out = rolled
    for s in range(shift):
        out = jnp.where(r == s, before[s:s + 1, :], out)
    return out


def _scan_block(seq):
    return _pick_tile(seq, (32, 16, 8))


def _hk_to_kh(a, axis):
    shape = a.shape
    a = a.reshape(shape[:axis] + (R_HEADS, R_HD) + shape[axis + 1:])
    return jnp.swapaxes(a, axis, axis + 1).reshape(shape)


def _kh_to_hk(a, axis):
    shape = a.shape
    a = a.reshape(shape[:axis] + (R_HD, R_HEADS) + shape[axis + 1:])
    return jnp.swapaxes(a, axis, axis + 1).reshape(shape)


def _sections(a, axis, fn):
    rw = R_HEADS * R_HD
    idx = lambda lo, hi: tuple(slice(lo, hi) if d == axis else slice(None) for d in range(a.ndim))
    parts = [fn(a[idx(i * rw, (i + 1) * rw)], axis) for i in range(3)]
    if a.shape[axis] > 3 * rw:
        parts.append(a[idx(3 * rw, a.shape[axis])])
    return jnp.concatenate(parts, axis=axis)


def _rwkv_prep_kernel(x_ref, gn_ref, wp_ref, bp_ref, st_ref, mu_ref, wl_ref, w0_ref, a0_ref,
                      rr_ref, rk_ref, rv_ref, dec_ref, a_ref, g_ref, sh_ref, carry_sc,
                      *, seq, whole_seqs):
    xin = x_ref[...]
    tb = xin.shape[0]
    rw = R_HEADS * R_HD
    ms = jnp.mean(xin * xin, axis=-1, keepdims=True)
    h = (xin * lax.rsqrt(ms + RMS_EPS) * gn_ref[...]).astype(BF16)
    if not whole_seqs:
        @pl.when(pl.program_id(1) == 0)
        def _():
            carry_sc[...] = st_ref[0]

    def proj(lo, hi):
        return jnp.dot(h, wp_ref[:, lo:hi], preferred_element_type=F32) + bp_ref[:, lo:hi]

    def shifted(x, lo, hi):
        if whole_seqs:
            prev = _shift_rows(x, 1, seq=seq, tile_has_whole_seqs=True, before=[st_ref[:, lo:hi]])
            sh_ref[:, lo:hi] = _seq_rows(x, seq - 1, seq=seq)
        else:
            prev = _shift_rows(x, 1, seq=seq, tile_has_whole_seqs=False,
                               before=carry_sc[:, lo:hi])
            carry_sc[:, lo:hi] = x[tb - 1:tb, :]
            sh_ref[0, :, lo:hi] = x[tb - 1:tb, :]
        return x + (prev - x) * mu_ref[:, lo:hi]

    x_lora = proj(3 * rw, 3 * rw + R_LORA)
    x_r = proj(0, rw)
    z = shifted(x_lora, 3 * rw, 3 * rw + R_LORA)
    lane = lax.broadcasted_iota(jnp.int32, (1, R_LORA), 1)
    act = jnp.where(lane < 64, jnp.tanh(z), jnp.where(lane < 128, z, _sigmoid(z))).astype(BF16)
    lo_w = jnp.dot(act, wl_ref[:, 0:rw], preferred_element_type=F32)
    x_k = proj(rw, 2 * rw)
    rr_ref[...] = shifted(x_r, 0, rw)
    lo_a = jnp.dot(act, wl_ref[:, rw:2 * rw], preferred_element_type=F32)
    x_v = proj(2 * rw, 3 * rw)
    rk_ref[...] = shifted(x_k, rw, 2 * rw)
    dec_ref[...] = jnp.exp(-jnp.exp(-_softplus(-(w0_ref[...] + lo_w)) - 0.5))
    g_ref[...] = jnp.dot(act, wl_ref[:, 2 * rw:3 * rw], preferred_element_type=F32)
    rv_ref[...] = shifted(x_v, 2 * rw, 3 * rw)
    a_ref[...] = _sigmoid(a0_ref[...] + lo_a)


def _rwkv_prep(x, gn, w_rw, b_rw, shift_state, mu, w_lora, w0, a0, *, row0, batch, seq, name):
    d = x.shape[1]
    c = w_rw.shape[1]
    rw = R_HEADS * R_HD
    n_tok = batch * seq
    whole = seq <= 64
    if whole:
        tb = _pick_tile(n_tok, (256, 128, 64, 32, 16, 8))
        assert tb % seq == 0 and row0 % tb == 0
        grid = (n_tok // tb, 1)
        rb0 = row0 // tb
        tok_map = lambda i, j: (rb0 + i, 0)
        out_map = lambda i, j: (i, 0)
        state = shift_state
        st_spec = pl.BlockSpec((tb // seq, c), out_map)
        sh_shape = jax.ShapeDtypeStruct((batch, c), F32)
    else:
        tb = _pick_tile(seq, (256, 128, 64, 32, 16, 8))
        nt = seq // tb
        grid = (batch, nt)
        rb0 = row0 // tb
        tok_map = lambda b, j: (rb0 + b * nt + j, 0)
        out_map = lambda b, j: (b * nt + j, 0)
        state = shift_state.reshape(batch, 1, c)
        st_spec = pl.BlockSpec((1, 1, c), lambda b, j: (b, 0, 0))
        sh_shape = jax.ShapeDtypeStruct((batch, 1, c), F32)
    fixed = lambda i, j: (0, 0)
    tok_out = jax.ShapeDtypeStruct((n_tok, rw), F32)
    tok_spec = pl.BlockSpec((tb, rw), out_map)
    blocks = (4 * _nbytes((tb, c), F32) + 6 * _nbytes((tb, rw), F32) + _nbytes(w_lora.shape, BF16)
              + _nbytes(w_rw.shape, BF16))
    outs = pl.pallas_call(
        functools.partial(_rwkv_prep_kernel, seq=seq, whole_seqs=whole),
        name=name,
        grid=grid,
        in_specs=[
            pl.BlockSpec((tb, d), tok_map),
            pl.BlockSpec((1, d), fixed),
            pl.BlockSpec(w_rw.shape, fixed),
            pl.BlockSpec((1, c), fixed),
            st_spec,
            pl.BlockSpec((1, c), fixed),
            pl.BlockSpec(w_lora.shape, fixed),
            pl.BlockSpec((1, rw), fixed),
            pl.BlockSpec((1, rw), fixed),
        ],
        out_specs=[tok_spec] * 6 + [st_spec],
        out_shape=[tok_out] * 6 + [sh_shape],
        scratch_shapes=[pltpu.VMEM((1, c), F32)],
        compiler_params=_params(("arbitrary", "arbitrary"), blocks),
    )(x, gn, w_rw, b_rw, state, mu, w_lora, w0, a0)
    return (*outs[:6], outs[6].reshape(batch, c))


def _rwkv_scan_kernel(rr_ref, rk_ref, dec_ref, a_ref, rv_ref, kks_ref, ka_ref, rrk_ref,
                      gng_ref, gnb_ref, s0_ref, y_ref, s1_ref,
                      s_sc, w_sc, kka_sc, km_sc, nkk_sc, wr_sc, coef_sc, *, tb):
    j = pl.program_id(1)
    nk = R_HD

    @pl.when(j == 0)
    def _():
        s_sc[...] = s0_ref[0]

    rr = rr_ref[...]
    rk = rk_ref[...]
    a = a_ref[...]
    dec = dec_ref[...]
    kk = rk * kks_ref[...]
    nrm = jnp.sqrt(jnp.sum(kk * kk, axis=0, keepdims=True))
    kk = kk * (1.0 / jnp.maximum(nrm, 1e-12))
    kmod = rk * (1.0 + (a - 1.0) * ka_ref[...])
    kka = kk * a
    g2 = dec.reshape(nk * tb, LANES)
    tpos = lax.broadcasted_iota(jnp.int32, (nk * tb, 1), 0) % tb
    shift = 1
    while shift < tb:
        g2 = g2 * jnp.where(tpos >= shift, pltpu.roll(g2, shift, 0), 1.0)
        shift *= 2
    g_prev = jnp.where(tpos >= 1, pltpu.roll(g2, 1, 0), 1.0).reshape(nk, tb, LANES)
    g_run = g2.reshape(nk, tb, LANES)
    g_inv = 1.0 / g_run
    w_sc[...] = g_run
    kka_sc[...] = kka * g_inv
    km_sc[...] = kmod * g_inv
    nkk_sc[...] = -kk * g_prev
    wr_sc[...] = rr * g_run
    coef_sc[0] = jnp.sum(kka * rr, axis=0)
    coef_sc[1] = jnp.sum(kmod * rr, axis=0)
    coef_sc[2] = jnp.sum(rr * kmod * rrk_ref[...], axis=0)

    def row(ref, t, k):
        return jnp.broadcast_to(ref[k, pl.ds(t, 1), :], (nk, LANES))

    def coef(i, t):
        return jnp.broadcast_to(coef_sc[i, pl.ds(t, 1), :], (nk, LANES))

    def matvecs(t):
        acc_sa = jnp.zeros((nk, LANES), F32)
        acc_y = jnp.zeros((nk, LANES), F32)
        for k in range(nk):
            sk = s_sc[k]
            acc_sa = acc_sa + sk * row(nkk_sc, t, k)
            acc_y = acc_y + sk * row(wr_sc, t, k)
        return acc_sa, acc_y

    def emit(t, sa, y0):
        val = rv_ref[t]
        y = y0 + sa * coef(0, t) + val * coef(1, t)
        mean = jnp.sum(y, axis=0, keepdims=True) * (1.0 / nk)
        d = y - mean
        var = jnp.sum(d * d, axis=0, keepdims=True) * (1.0 / nk)
        yn = d * lax.rsqrt(var + GN_EPS) * gng_ref[...] + gnb_ref[...]
        y_ref[t] = yn + coef(2, t) * val
        return val

    def step(t, carry):
        sa, y0 = carry
        val = emit(t, sa, y0)
        acc_sa = jnp.zeros((nk, LANES), F32)
        acc_y = jnp.zeros((nk, LANES), F32)
        for k in range(nk):
            sk = s_sc[k] + sa * row(kka_sc, t, k) + val * row(km_sc, t, k)
            s_sc[k] = sk
            acc_sa = acc_sa + sk * row(nkk_sc, t + 1, k)
            acc_y = acc_y + sk * row(wr_sc, t + 1, k)
        return acc_sa, acc_y

    sa, y0 = lax.fori_loop(0, tb - 1, step, matvecs(0))
    t_last = tb - 1
    val = emit(t_last, sa, y0)
    for k in range(nk):
        s_sc[k] = ((s_sc[k] + sa * row(kka_sc, t_last, k) + val * row(km_sc, t_last, k))
                   * row(w_sc, t_last, k))

    @pl.when(j == pl.num_programs(1) - 1)
    def _():
        s1_ref[0] = s_sc[...]


LAYOUT_TB = 128


def _lane_slabs(in_ref, p):
    nb = HEADS_PER_LANE_GROUP
    zts = [in_ref[b, :, p * LANES:(p + 1) * LANES].T for b in range(nb)]
    for i in range(SUBLANES):
        d = jnp.concatenate([zt[i * R_HEADS:(i + 1) * R_HEADS, :] for zt in zts], axis=0)
        yield SUBLANES * p + i, d.T


def _lanes_in_kernel(src_ref, dst_ref, *, rows):
    for p in range(R_HD // SUBLANES):
        for i, slab in _lane_slabs(src_ref, p):
            if rows:
                dst_ref[i] = slab
            else:
                dst_ref[:, pl.ds(i, 1), :] = slab[:, None, :]


def _lanes_in(x, *, batch, seq, rows):
    nb = HEADS_PER_LANE_GROUP
    g = batch // nb
    w = R_HEADS * R_HD
    if seq % LAYOUT_TB != 0 and (g * seq) % LAYOUT_TB == 0:
        x = x.reshape(g, nb, seq, w).transpose(1, 0, 2, 3).reshape(nb * g * seq, w)
        return _lanes_in(x, batch=nb, seq=g * seq, rows=rows)
    if seq % LAYOUT_TB != 0:
        x = x.reshape(g, nb, seq, R_HD, R_HEADS)
        if rows:
            return x.transpose(3, 0, 2, 1, 4).reshape(R_HD, g * seq, LANES)
        return x.transpose(0, 2, 3, 1, 4).reshape(g * seq, R_HD, LANES)
    nt = seq // LAYOUT_TB
    if rows:
        out_spec = pl.BlockSpec((R_HD, LAYOUT_TB, LANES), lambda gi, j: (0, gi * nt + j, 0))
        out_shape = jax.ShapeDtypeStruct((R_HD, g * seq, LANES), F32)
    else:
        out_spec = pl.BlockSpec((LAYOUT_TB, R_HD, LANES), lambda gi, j: (gi * nt + j, 0, 0))
        out_shape = jax.ShapeDtypeStruct((g * seq, R_HD, LANES), F32)
    return pl.pallas_call(
        functools.partial(_lanes_in_kernel, rows=rows),
        name="lanes_in_rows" if rows else "lanes_in_full",
        grid=(g, nt),
        in_specs=[pl.BlockSpec((nb, LAYOUT_TB, w), lambda gi, j: (gi, j, 0))],
        out_specs=out_spec,
        out_shape=out_shape,
        compiler_params=_params(("arbitrary", "arbitrary"), 2 * _nbytes((nb, LAYOUT_TB, w), F32)),
    )(x.reshape(batch, seq, w))


def _lanes_out_kernel(y_ref, o_ref):
    nb = HEADS_PER_LANE_GROUP
    for p in range(R_HD // SUBLANES):
        pieces = [[] for _ in range(nb)]
        for i in range(SUBLANES):
            st = y_ref[:, SUBLANES * p + i, :].T
            for b in range(nb):
                pieces[b].append(st[b * R_HEADS:(b + 1) * R_HEADS, :])
        for b in range(nb):
            zt = jnp.concatenate(pieces[b], axis=0)
            o_ref[b, :, p * LANES:(p + 1) * LANES] = zt.T


def _lanes_out(y, *, batch, seq):
    nb = HEADS_PER_LANE_GROUP
    g = batch // nb
    w = R_HEADS * R_HD
    if seq % LAYOUT_TB != 0 and (g * seq) % LAYOUT_TB == 0:
        tok = _lanes_out(y, batch=nb, seq=g * seq)
        return tok.reshape(nb, g, seq, w).transpose(1, 0, 2, 3).reshape(batch * seq, w)
    if seq % LAYOUT_TB != 0:
        y = y.reshape(g, seq, R_HD, nb, R_HEADS)
        return y.transpose(0, 3, 1, 2, 4).reshape(batch * seq, w)
    nt = seq // LAYOUT_TB
    out = pl.pallas_call(
        _lanes_out_kernel,
        name="lanes_out",
        grid=(g, nt),
        in_specs=[pl.BlockSpec((LAYOUT_TB, R_HD, LANES), lambda gi, j: (gi * nt + j, 0, 0))],
        out_specs=pl.BlockSpec((nb, LAYOUT_TB, w), lambda gi, j: (gi, j, 0)),
        out_shape=jax.ShapeDtypeStruct((batch, seq, w), F32),
        compiler_params=_params(("arbitrary", "arbitrary"), 2 * _nbytes((nb, LAYOUT_TB, w), F32)),
    )(y)
    return out.reshape(batch * seq, w)


def _param_lanes(p):
    p = p.reshape(R_HEADS, R_HD).T
    return jnp.tile(p, (1, HEADS_PER_LANE_GROUP))


def _rwkv_scan(rr, rk, rv, dec, a, kks, ka, rrk, gng, gnb, s0, *, batch, seq, name):
    assert batch % HEADS_PER_LANE_GROUP == 0
    g = batch // HEADS_PER_LANE_GROUP
    tb = _scan_block(seq)
    nt = seq // tb
    rows = [_lanes_in(v, batch=batch, seq=seq, rows=True) for v in (rr, rk, dec, a)]
    val = _lanes_in(rv, batch=batch, seq=seq, rows=False)
    s0l = s0.reshape(g, HEADS_PER_LANE_GROUP, R_HEADS, R_HD, R_HD).transpose(0, 4, 3, 1, 2)
    s0l = s0l.reshape(g, R_HD, R_HD, LANES)
    row_spec = pl.BlockSpec((R_HD, tb, LANES), lambda gi, j: (0, gi * nt + j, 0))
    val_spec = pl.BlockSpec((tb, R_HD, LANES), lambda gi, j: (gi * nt + j, 0, 0))
    kpar_spec = pl.BlockSpec((R_HD, 1, LANES), lambda gi, j: (0, 0, 0))
    vpar_spec = pl.BlockSpec((R_HD, LANES), lambda gi, j: (0, 0))
    st_spec = pl.BlockSpec((1, R_HD, R_HD, LANES), lambda gi, j: (gi, 0, 0, 0))
    vec_bytes = _nbytes((tb, R_HD, LANES), F32)
    blocks = 11 * vec_bytes + 3 * _nbytes((R_HD, R_HD, LANES), F32)
    kpar = lambda p: _param_lanes(p)[:, None, :]
    y, s1l = pl.pallas_call(
        functools.partial(_rwkv_scan_kernel, tb=tb),
        name=name,
        grid=(g, nt),
        in_specs=[row_spec] * 4 + [val_spec] + [kpar_spec] * 3 + [vpar_spec] * 2 + [st_spec],
        out_specs=[val_spec, st_spec],
        out_shape=[
            jax.ShapeDtypeStruct((g * seq, R_HD, LANES), F32),
            jax.ShapeDtypeStruct(s0l.shape, F32),
        ],
        scratch_shapes=[pltpu.VMEM((R_HD, R_HD, LANES), F32)]
        + [pltpu.VMEM((R_HD, tb, LANES), F32)] * 5
        + [pltpu.VMEM((3, tb, LANES), F32)],
        compiler_params=_params(("arbitrary", "arbitrary"), blocks),
    )(*rows, val, kpar(kks), kpar(ka), kpar(rrk), _param_lanes(gng), _param_lanes(gnb), s0l)
    s1 = s1l.reshape(g, R_HD, R_HD, HEADS_PER_LANE_GROUP, R_HEADS).transpose(0, 3, 4, 2, 1)
    return _lanes_out(y, batch=batch, seq=seq), s1.reshape(batch, R_HEADS, R_HD, R_HD)


def _mix_out_kernel(hm_ref, yr_ref, g_ref, gm_ref, gr_ref, x_ref, wm_ref, wr_ref, wo_ref, *rest):
    o_ref = rest[-1]
    bm = jnp.dot(hm_ref[...].astype(BF16), wm_ref[...], preferred_element_type=F32)
    yr = (yr_ref[...] * g_ref[...]).astype(BF16)
    br = jnp.dot(yr, wr_ref[...], preferred_element_type=F32)
    merged = (_sigmoid(gm_ref[...].astype(F32)) * bm
              + _sigmoid(gr_ref[...].astype(F32)) * br)
    o_ref[...] = x_ref[...] + jnp.dot(merged.astype(BF16), wo_ref[...], preferred_element_type=F32)


def _mix_out(hm, yr, g, p_mix, x, wm, wr, wo, out_buf, *, row0, name):
    n_all, d = x.shape
    n = yr.shape[0]
    tm = _pick_tile(n, (256, 128, 64, 32, 16, 8))
    assert row0 % tm == 0
    rb0 = row0 // tm
    gm_blk = (2 * M_HEADS * (M_DK + M_DV)) // d
    shared = lambda i: (rb0 + i, 0)
    own = lambda i: (i, 0)
    fixed = lambda i: (0, 0)
    aliased = out_buf is not None
    in_specs = [
        pl.BlockSpec((tm, wm.shape[0]), own),
        pl.BlockSpec((tm, wr.shape[0]), own),
        pl.BlockSpec((tm, wr.shape[0]), own),
        pl.BlockSpec((tm, d), lambda i: (rb0 + i, gm_blk)),
        pl.BlockSpec((tm, d), lambda i: (rb0 + i, gm_blk + 1)),
        pl.BlockSpec((tm, d), shared),
        pl.BlockSpec(wm.shape, fixed),
        pl.BlockSpec(wr.shape, fixed),
        pl.BlockSpec(wo.shape, fixed),
    ]
    args = [hm, yr, g, p_mix, p_mix, x, wm, wr, wo]
    if aliased:
        in_specs.append(ANY_SPEC)
        args.append(out_buf)
    blocks = 7 * _nbytes((tm, d), F32) + 3 * _nbytes((d, d), BF16)
    return pl.pallas_call(
        _mix_out_kernel,
        name=name,
        grid=(n // tm,),
        in_specs=in_specs,
        out_specs=pl.BlockSpec((tm, d), shared),
        out_shape=jax.ShapeDtypeStruct((n_all, d), F32),
        input_output_aliases={9: 0} if aliased else {},
        compiler_params=_params(("arbitrary",), blocks),
    )(*args)


FFN_CHUNK = 256


def _ffn_kernel(x_ref, gn_ref, wu_ref, cs_ref, cw_ref, cb_ref, wd_ref, *rest, seq, whole_seqs):
    o_ref, st_ref, carry_sc = rest[-3:]
    x = x_ref[...]
    tb = x.shape[0]
    dff = wd_ref.shape[0]
    ms = jnp.mean(x * x, axis=-1, keepdims=True)
    h = (x * lax.rsqrt(ms + RMS_EPS) * gn_ref[...]).astype(BF16)
    if not whole_seqs:
        @pl.when(pl.program_id(1) == 0)
        def _():
            carry_sc[...] = cs_ref[0]
    def up(c0):
        return [jnp.dot(h, wu_ref[:, lo:lo + FFN_CHUNK], preferred_element_type=F32)
                for lo in (c0, dff + c0)]

    def down(acc, act, c0):
        return acc + jnp.dot(act, wd_ref[c0:c0 + FFN_CHUNK, :], preferred_element_type=F32)

    acc = x
    starts = list(range(0, dff, FFN_CHUNK))
    u_next = up(starts[0])
    pending = None
    for ci, c0 in enumerate(starts):
        u_cur = u_next
        if ci + 1 < len(starts):
            u_next = up(starts[ci + 1])
        if pending is not None:
            acc = down(acc, *pending)
        halves = []
        for lo, u in zip((c0, dff + c0), u_cur):
            cols = slice(lo, lo + FFN_CHUNK)
            if whole_seqs:
                before = [cs_ref[0, :, cols], cs_ref[1, :, cols]]
                p1 = _shift_rows(u, 1, seq=seq, tile_has_whole_seqs=True, before=before[1:])
                p2 = _shift_rows(u, 2, seq=seq, tile_has_whole_seqs=True, before=before)
                st_ref[0, :, cols] = _seq_rows(u, seq - 2, seq=seq)
                st_ref[1, :, cols] = _seq_rows(u, seq - 1, seq=seq)
            else:
                carry = carry_sc[:, cols]
                p1 = _shift_rows(u, 1, seq=seq, tile_has_whole_seqs=False, before=carry[1:2, :])
                p2 = _shift_rows(u, 2, seq=seq, tile_has_whole_seqs=False, before=carry)
                carry_sc[:, cols] = u[tb - 2:tb, :]
                st_ref[0, :, cols] = u[tb - 2:tb, :]
            halves.append(cb_ref[:, cols] + p2 * cw_ref[0:1, cols] + p1 * cw_ref[1:2, cols]
                          + u * cw_ref[2:3, cols])
        gate, val = halves
        pending = ((gate * _sigmoid(gate) * val).astype(BF16), c0)
    o_ref[...] = down(acc, *pending)


def _ffn(x, gn, wu, conv_state, cw, cb, wd, out_buf, *, row0, batch, seq, name):
    n_all, d = x.shape
    c = wu.shape[1]
    n_tok = batch * seq
    taps = cw.shape[0]
    assert taps == 3 and seq >= taps - 1 and wd.shape[0] % FFN_CHUNK == 0
    whole = seq <= 64
    if whole:
        tb = _pick_tile(n_tok, (128, 64, 32, 16, 8))
        assert tb % seq == 0 and row0 % tb == 0
        grid = (n_tok // tb, 1)
        rb0 = row0 // tb
        tok_map = lambda i, j: (rb0 + i, 0)
        state = jnp.swapaxes(conv_state, 0, 1)
        st_shape = jax.ShapeDtypeStruct((taps - 1, batch, c), F32)
        st_spec = pl.BlockSpec((taps - 1, tb // seq, c), lambda i, j: (0, i, 0))
    else:
        tb = _pick_tile(seq, (256, 128, 64, 32, 16, 8))
        nt = seq // tb
        grid = (batch, nt)
        rb0 = row0 // tb
        tok_map = lambda b, j: (rb0 + b * nt + j, 0)
        state = conv_state
        st_shape = jax.ShapeDtypeStruct((batch, taps - 1, c), F32)
        st_spec = pl.BlockSpec((1, taps - 1, c), lambda b, j: (b, 0, 0))
    fixed = lambda i, j: (0, 0)
    aliased = out_buf is not None
    in_specs = [
        pl.BlockSpec((tb, d), tok_map),
        pl.BlockSpec((1, d), fixed),
        pl.BlockSpec(wu.shape, fixed),
        st_spec,
        pl.BlockSpec(cw.shape, fixed),
        pl.BlockSpec((1, c), fixed),
        pl.BlockSpec(wd.shape, fixed),
    ]
    args = [x, gn, wu, state, cw, cb, wd]
    if aliased:
        in_specs.append(ANY_SPEC)
        args.append(out_buf)
    blocks = _nbytes(wu.shape, BF16) + _nbytes(wd.shape, BF16) + 6 * _nbytes((tb, d), F32)
    y, st = pl.pallas_call(
        functools.partial(_ffn_kernel, seq=seq, whole_seqs=whole),
        name=name,
        grid=grid,
        in_specs=in_specs,
        out_specs=[pl.BlockSpec((tb, d), tok_map), st_spec],
        out_shape=[jax.ShapeDtypeStruct((n_all, d), F32), st_shape],
        scratch_shapes=[pltpu.VMEM((taps - 1, c), F32)],
        input_output_aliases={7: 0} if aliased else {},
        compiler_params=_params(("arbitrary", "arbitrary"), blocks),
    )(*args)
    return y, (jnp.swapaxes(st, 0, 1) if whole else st)


def _final_norm_kernel(x_ref, g_ref, o_ref):
    x = x_ref[...]
    ms = jnp.mean(x * x, axis=-1, keepdims=True)
    o_ref[...] = x * lax.rsqrt(ms + RMS_EPS) * g_ref[...]


def _final_norm(x, g, *, row0, n, name):
    d = x.shape[1]
    tm = _pick_tile(n, (1024, 512, 256, 128, 64, 32, 16, 8))
    assert row0 % tm == 0
    rb0 = row0 // tm
    return pl.pallas_call(
        _final_norm_kernel,
        name=name,
        grid=(n // tm,),
        in_specs=[pl.BlockSpec((tm, d), lambda i: (rb0 + i, 0)),
                  pl.BlockSpec((1, d), lambda i: (0, 0))],
        out_specs=pl.BlockSpec((tm, d), lambda i: (i, 0)),
        out_shape=jax.ShapeDtypeStruct((n, d), F32),
        compiler_params=_params(("arbitrary",), 2 * _nbytes((tm, d), F32)),
    )(x, g)


def kernel(x_prompt, x_sample, state_mlstm_C, state_mlstm_n, state_mlstm_m, state_rwkv_S, state_rwkv_shift, state_ffn_conv, norm_mix_g, w_in, b_in, mu_shift, m_norm_g, r_w0, r_w_up, r_a0, r_a_up, r_g_up, r_k_k, r_k_a, r_r_k, r_gn_g, r_gn_b, w_br_m, w_br_r, w_out, norm_ffn_g, w_up, conv_w, conv_b, w_down, norm_final_g):
    bp, tp, d = x_prompt.shape
    bs, ts, _ = x_sample.shape
    depth = w_in.shape[0]
    n_p = bp * tp
    n_s = bs * ts
    rw = R_HEADS * R_HD
    qkvo = 2 * M_HEADS * (M_DK + M_DV)
    n_gate = 2 * M_HEADS
    mix_cols = qkvo + n_gate + 2 * d
    rw_cols = 3 * rw + R_LORA
    dff2 = w_up.shape[2]
    assert w_in.shape[2] == mix_cols + rw_cols

    x = jnp.concatenate([x_prompt.reshape(n_p, d), x_sample.reshape(n_s, d)], axis=0)

    zc = jnp.zeros((1, bp, M_HEADS, M_DV, M_DK), F32)
    c_all_p = c_all_s = None
    zn = jnp.zeros((bp, M_HEADS, M_DK), F32)
    zm = jnp.zeros((bp, M_HEADS), F32)
    zs = jnp.zeros((bp, R_HEADS, R_HD, R_HD), F32)
    zsh = jnp.zeros((bp, rw_cols), F32)
    zcv = jnp.zeros((bp, conv_w.shape[1] - 1, dff2), F32)

    outs_p = [[] for _ in range(5)]
    outs_s = [[] for _ in range(5)]
    for l in range(depth):
        wl = w_in[l]
        bl = b_in[l]
        w_mix = jnp.concatenate([wl[:, :qkvo], wl[:, qkvo + n_gate:mix_cols]], axis=1).astype(BF16)
        b_mix = jnp.concatenate([bl[:qkvo], bl[qkvo + n_gate:mix_cols]])[None]
        w_gate = jnp.pad(wl[:, qkvo:qkvo + n_gate], ((0, 0), (0, LANES - n_gate))).astype(BF16)
        b_gate = jnp.pad(bl[qkvo:qkvo + n_gate], (0, LANES - n_gate))[None]
        w_rw = _sections(wl[:, mix_cols:], 1, _hk_to_kh).astype(BF16)
        b_rw = _sections(bl[mix_cols:], 0, _hk_to_kh)[None]
        mu_rw = _sections(mu_shift[l], 0, _hk_to_kh)[None]
        w_lora = jnp.zeros((R_LORA, 3 * rw), F32)
        w_lora = w_lora.at[0:64, 0:rw].set(r_w_up[l]).at[64:128, rw:2 * rw].set(r_a_up[l])
        w_lora = _sections(w_lora.at[128:256, 2 * rw:].set(r_g_up[l]), 1, _hk_to_kh).astype(BF16)
        w_br_r_kh = _hk_to_kh(w_br_r[l], 0).astype(BF16)
        g_mix = norm_mix_g[l][None]

        p_mix, gates = _in_proj(x, g_mix, w_mix, b_mix, w_gate, b_gate, tn=1024,
                                name="in_proj_mix", out_dtype=BF16)

        gn = m_norm_g[l].reshape(1, M_HEADS * M_DV)
        hm_p, c_all_p, nn_p, m_p = _mlstm(
            p_mix, gates, zc, zn, zm, gn, c_all_p, layer=l, depth=depth,
            row0=0, batch=bp, seq=tp, name="mlstm_prompt")
        hm_s, c_all_s, nn_s, m_s = _mlstm(
            p_mix, gates, state_mlstm_C, state_mlstm_n[l], state_mlstm_m[l], gn, c_all_s,
            layer=l, depth=depth, row0=n_p, batch=bs, seq=ts, name="mlstm_sample")

        prep_args = (mu_rw, w_lora, _hk_to_kh(r_w0[l], 0)[None], _hk_to_kh(r_a0[l], 0)[None])
        rr_p, rk_p, rv_p, wp_p, ap_p, g_p, sh_p = _rwkv_prep(
            x, g_mix, w_rw, b_rw, zsh, *prep_args, row0=0, batch=bp, seq=tp,
            name="rwkv_prep_prompt")
        rr_s, rk_s, rv_s, wp_s, ap_s, g_s, sh_s = _rwkv_prep(
            x, g_mix, w_rw, b_rw, _sections(state_rwkv_shift[l], 1, _hk_to_kh), *prep_args,
            row0=n_p, batch=bs, seq=ts, name="rwkv_prep_sample")
        sh_p = _sections(sh_p, 1, _kh_to_hk)
        sh_s = _sections(sh_s, 1, _kh_to_hk)
        scan_args = (r_k_k[l], r_k_a[l], r_r_k[l], r_gn_g[l], r_gn_b[l])
        y_p, s_p = _rwkv_scan(rr_p, rk_p, rv_p, wp_p, ap_p, *scan_args, zs,
                              batch=bp, seq=tp, name="rwkv_scan_prompt")
        y_s, s_s = _rwkv_scan(rr_s, rk_s, rv_s, wp_s, ap_s, *scan_args, state_rwkv_S[l],
                              batch=bs, seq=ts, name="rwkv_scan_sample")

        w_branches = (w_br_m[l].astype(BF16), w_br_r_kh, w_out[l].astype(BF16))
        v_w = M_HEADS * M_DV
        x_mid = _mix_out(hm_p.reshape(n_p, v_w), y_p, g_p, p_mix, x, *w_branches, None, row0=0,
                         name="mix_out_prompt")
        x_mid = _mix_out(hm_s.reshape(n_s, v_w), y_s, g_s, p_mix, x, *w_branches, x_mid, row0=n_p,
                         name="mix_out_sample")

        ffn_w = (conv_w[l], conv_b[l][None], w_down[l].astype(BF16))
        g_ffn = norm_ffn_g[l][None]
        wu = w_up[l].astype(BF16)
        x, cv_p = _ffn(x_mid, g_ffn, wu, zcv, *ffn_w, None, row0=0, batch=bp, seq=tp,
                       name="ffn_prompt")
        x, cv_s = _ffn(x_mid, g_ffn, wu, state_ffn_conv[l], *ffn_w, x, row0=n_p, batch=bs, seq=ts,
                       name="ffn_sample")

        for lst, val in zip(outs_p, (nn_p.reshape(bp, M_HEADS, M_DK), m_p.reshape(bp, M_HEADS),
                                     s_p, sh_p, cv_p)):
            lst.append(val)
        for lst, val in zip(outs_s, (nn_s.reshape(bs, M_HEADS, M_DK), m_s.reshape(bs, M_HEADS),
                                     s_s, sh_s, cv_s)):
            lst.append(val)

    g_fin = norm_final_g[None]
    y_prompt = _final_norm(x, g_fin, row0=0, n=n_p, name="final_norm_prompt").reshape(bp, tp, d)
    y_sample = _final_norm(x, g_fin, row0=n_p, n=n_s, name="final_norm_sample").reshape(bs, ts, d)
    return (y_prompt, y_sample, c_all_p, *[jnp.stack(o) for o in outs_p],
            c_all_s, *[jnp.stack(o) for o in outs_s])
```

```python
import functools

import jax
import jax.numpy as jnp
from jax import lax
from jax.experimental import pallas as pl
from jax.experimental.pallas import tpu as pltpu

F32 = jnp.float32
BF16 = jnp.bfloat16

M_HEADS = 4
M_DK = 128
M_DV = 256
M_CHUNK = 64
R_HEADS = 16
R_HD = 64
R_LORA = 256
RMS_EPS = 1e-6
GN_EPS = 64e-5

LANES = 128
SUBLANES = 8
VMEM_LIMIT_CAP = 56 * 1024 * 1024
HEADS_PER_LANE_GROUP = LANES // R_HEADS

HI = lax.Precision.HIGHEST
ANY_SPEC = pl.BlockSpec(memory_space=pl.ANY)


def _vmem_limit(block_bytes):
    return int(min(VMEM_LIMIT_CAP, max(16 * 1024 * 1024, 3 * block_bytes)))


def _nbytes(shape, dtype):
    n = 1
    for s in shape:
        n *= s
    return n * jnp.dtype(dtype).itemsize


def _pick_tile(n, candidates):
    for c in candidates:
        if n % c == 0:
            return c
    raise ValueError(f"no tile in {candidates} divides {n}")


def _sigmoid(x):
    return 1.0 / (1.0 + jnp.exp(-x))


def _softplus(x):
    return jnp.maximum(x, 0.0) + jnp.log(1.0 + jnp.exp(-jnp.abs(x)))


def _params(semantics, block_bytes):
    return pltpu.CompilerParams(dimension_semantics=semantics,
                                vmem_limit_bytes=_vmem_limit(block_bytes))


def _in_proj_kernel(x_ref, g_ref, w_ref, b_ref, wg_ref, bg_ref, o_ref, og_ref, h_ref):
    @pl.when(pl.program_id(1) == 0)
    def _():
        x = x_ref[...]
        ms = jnp.mean(x * x, axis=-1, keepdims=True)
        h = (x * lax.rsqrt(ms + RMS_EPS) * g_ref[...]).astype(BF16)
        h_ref[...] = h
        og_ref[...] = jnp.dot(h, wg_ref[...], preferred_element_type=F32) + bg_ref[...]

    acc = jnp.dot(h_ref[...], w_ref[...], preferred_element_type=F32) + b_ref[...]
    o_ref[...] = acc.astype(o_ref.dtype)


def _in_proj(x, g, w, b, wg, bg, *, tn, name, out_dtype):
    n, d = x.shape
    c = w.shape[1]
    tm = _pick_tile(n, (1024, 512, 256, 128, 64, 32, 16, 8))
    blocks = (_nbytes((tm, d), F32) + _nbytes((d, tn), BF16) + _nbytes((tm, tn), F32)
              + _nbytes((tm, d), BF16) + _nbytes((d, LANES), BF16) + _nbytes((tm, LANES), F32))
    return pl.pallas_call(
        _in_proj_kernel,
        name=name,
        grid=(n // tm, c // tn),
        in_specs=[
            pl.BlockSpec((tm, d), lambda i, j: (i, 0)),
            pl.BlockSpec((1, d), lambda i, j: (0, 0)),
            pl.BlockSpec((d, tn), lambda i, j: (0, j)),
            pl.BlockSpec((1, tn), lambda i, j: (0, j)),
            pl.BlockSpec((d, LANES), lambda i, j: (0, 0)),
            pl.BlockSpec((1, LANES), lambda i, j: (0, 0)),
        ],
        out_specs=[pl.BlockSpec((tm, tn), lambda i, j: (i, j)),
                   pl.BlockSpec((tm, LANES), lambda i, j: (i, 0))],
        out_shape=[jax.ShapeDtypeStruct((n, c), out_dtype),
                   jax.ShapeDtypeStruct((n, LANES), F32)],
        scratch_shapes=[pltpu.VMEM((tm, d), BF16)],
        compiler_params=_params(("arbitrary", "arbitrary"), blocks),
    )(x, g, w, b, wg, bg)


def _mlstm_kernel(*refs, chunk, slots, per_slot):
    tok_refs = [refs[5 * a:5 * a + 5] for a in range(slots)]
    c0_ref, n0_ref, m0_ref, gn_ref = refs[5 * slots:5 * slots + 4]
    h_ref, c1_ref, n1_ref, m1_ref, c_sc, n_sc, m_sc = refs[-7:]
    L = chunk
    nseq = slots * per_slot
    ci = pl.program_id(1)

    @pl.when(ci == 0)
    def _():
        c_sc[...] = c0_ref[0]
        n_sc[...] = n0_ref[...]
        m_sc[...] = m0_ref[...]

    lane = lax.broadcasted_iota(jnp.int32, (1, LANES), 1)
    row = lax.broadcasted_iota(jnp.int32, (L, L), 0)
    col = lax.broadcasted_iota(jnp.int32, (L, L), 1)
    causal = row >= col
    q_all = [r[0][...].astype(F32) * (M_DK ** -0.5) for r in tok_refs]
    k_all = [r[1][...].astype(F32) for r in tok_refs]
    v_all = [r[2][...].astype(F32) for r in tok_refs]
    o_all = [r[3][...].astype(F32) for r in tok_refs]

    def rows_of(s):
        return s // per_slot, slice((s % per_slot) * L, (s % per_slot + 1) * L)

    nt_dims = (((1,), (1,)), ((), ()))
    chains = [(s, h) for s in range(nseq) for h in range(M_HEADS)]

    zs, z_rows = [], []
    tri = causal.astype(F32)
    eye8 = (lax.broadcasted_iota(jnp.int32, (SUBLANES, LANES), 0)
            == lax.broadcasted_iota(jnp.int32, (SUBLANES, LANES), 1)).astype(F32)
    for s in range(nseq):
        a, rows = rows_of(s)
        gates = tok_refs[a][4][rows, :]
        b_all = jnp.dot(tri, -_softplus(-gates), precision=HI, preferred_element_type=F32)
        zs.append(jnp.where(lane < M_HEADS, gates, b_all))
    for s in range(nseq):
        z_rows.append(lax.dot_general(eye8, zs[s], nt_dims, precision=HI,
                                      preferred_element_type=F32))

    def qkv(s, h):
        a, rows = rows_of(s)
        return (q_all[a][rows, h * M_DK:(h + 1) * M_DK], k_all[a][rows, h * M_DK:(h + 1) * M_DK],
                v_all[a][rows, h * M_DV:(h + 1) * M_DV])

    raw, qcs = [], []
    for s, h in chains:
        q, k, _ = qkv(s, h)
        raw.append(lax.dot_general(q, k, nt_dims, preferred_element_type=F32))
        qcs.append(lax.dot_general(q, c_sc[s, h], nt_dims, preferred_element_type=F32))

    scs, inters, m_ts = [], [], []
    for idx, (s, h) in enumerate(chains):
        i_row = z_rows[s][h:h + 1, :]
        b_row = z_rows[s][M_HEADS + h:M_HEADS + h + 1, :]
        b_col = zs[s][:, M_HEADS + h:M_HEADS + h + 1]
        dmat = jnp.where(causal, b_col - b_row + i_row, -jnp.inf)
        m_inter = b_col + m_sc[s, h]
        m_t = jnp.maximum(m_inter, jnp.max(dmat, axis=-1, keepdims=True))
        scs.append(raw[idx] * jnp.exp(dmat - m_t))
        inters.append(jnp.exp(m_inter - m_t))
        m_ts.append(m_t)

    svs = [jnp.dot(scs[idx], qkv(s, h)[2], preferred_element_type=F32)
           for idx, (s, h) in enumerate(chains)]

    upds = []
    for idx, (s, h) in enumerate(chains):
        _, k, v = qkv(s, h)
        i_col = zs[s][:, h:h + 1]
        b_col = zs[s][:, M_HEADS + h:M_HEADS + h + 1]
        m_new = m_ts[idx][L - 1:L, :]
        b_last = b_col[L - 1:L, :]
        w_s = jnp.exp(b_last - b_col + i_col - m_new)
        decay = jnp.exp(b_last + m_sc[s, h] - m_new)
        kv = lax.dot_general(v * w_s, k, (((0,), (0,)), ((), ())), preferred_element_type=F32)
        upds.append((decay, kv, jnp.sum(w_s * k, axis=0, keepdims=True), m_new))

    for idx, (s, h) in enumerate(chains):
        q, _, _ = qkv(s, h)
        a, rows = rows_of(s)
        num = svs[idx] + inters[idx] * qcs[idx]
        den = (jnp.sum(scs[idx], axis=-1, keepdims=True)
               + inters[idx] * jnp.sum(q * n_sc[s, h], axis=-1, keepdims=True))
        hh = num / jnp.maximum(jnp.abs(den), jnp.exp(-m_ts[idx]))
        hn = (hh * lax.rsqrt(jnp.mean(hh * hh, axis=-1, keepdims=True) + RMS_EPS)
              * gn_ref[:, h * M_DV:(h + 1) * M_DV])
        h_ref[s, :, h * M_DV:(h + 1) * M_DV] = hn * _sigmoid(
            o_all[a][rows, h * M_DV:(h + 1) * M_DV])

    for (s, h), (decay, kv, kn, m_new) in zip(chains, upds):
        c_sc[s, h] = decay * c_sc[s, h] + kv
        n_sc[s, h] = decay * n_sc[s, h] + kn
        m_sc[s, h] = m_new

    @pl.when(ci == pl.num_programs(1) - 1)
    def _():
        c1_ref[0] = c_sc[...]
        n1_ref[...] = n_sc[...]
        m1_ref[...] = m_sc[...]


def _mlstm(p_mix, gates, c0_all, n0, m0, gn, c_buf, *, layer, depth, row0, batch, seq, name):
    layer_in = min(layer, c0_all.shape[0] - 1)
    L = M_CHUNK if seq % M_CHUNK == 0 else seq
    nc = seq // L
    slots, per_slot = (_pick_tile(batch, (2, 1)), 1) if nc > 1 else (1, _pick_tile(batch, (4, 2, 1)))
    nseq = slots * per_slot
    rows = per_slot * L
    rb0 = row0 // rows
    assert row0 % rows == 0
    qk_w = M_HEADS * M_DK
    v_w = M_HEADS * M_DV
    assert qk_w * 2 == v_w

    def tok(a):
        return lambda bi, c: rb0 + (bi * slots + a) * nc + c

    def tok_spec(a, width, col_blk):
        return pl.BlockSpec((rows, width), lambda bi, c: (tok(a)(bi, c), col_blk))

    state_map = lambda bi, c: (bi, 0, 0, 0)
    c_blk = (1, nseq, M_HEADS, M_DV, M_DK)
    in_specs, args = [], []
    for a in range(slots):
        in_specs += [tok_spec(a, qk_w, 0), tok_spec(a, qk_w, 1), tok_spec(a, v_w, 1),
                     tok_spec(a, v_w, 2), tok_spec(a, LANES, 0)]
        args += [p_mix, p_mix, p_mix, p_mix, gates]
    in_specs += [
        pl.BlockSpec(c_blk, lambda bi, c: (layer_in, bi, 0, 0, 0)),
        pl.BlockSpec((nseq, M_HEADS, 1, M_DK), state_map),
        pl.BlockSpec((nseq, M_HEADS, 1, 1), state_map),
        pl.BlockSpec((1, v_w), lambda bi, c: (0, 0)),
    ]
    args += [c0_all, n0.reshape(batch, M_HEADS, 1, M_DK), m0.reshape(batch, M_HEADS, 1, 1), gn]
    aliases = {}
    if c_buf is not None:
        aliases[len(args)] = 1
        in_specs.append(ANY_SPEC)
        args.append(c_buf)
    blocks = (slots * (2 * _nbytes((rows, qk_w), F32) + 3 * _nbytes((rows, v_w), F32))
              + 3 * _nbytes((nseq, M_HEADS, M_DV, M_DK), F32))
    return pl.pallas_call(
        functools.partial(_mlstm_kernel, chunk=L, slots=slots, per_slot=per_slot),
        name=name,
        grid=(batch // nseq, nc),
        in_specs=in_specs,
        out_specs=[
            pl.BlockSpec((nseq, L, v_w), lambda bi, c: (bi, c, 0)),
            pl.BlockSpec(c_blk, lambda bi, c: (layer, bi, 0, 0, 0)),
            pl.BlockSpec((nseq, M_HEADS, 1, M_DK), state_map),
            pl.BlockSpec((nseq, M_HEADS, 1, 1), state_map),
        ],
        out_shape=[
            jax.ShapeDtypeStruct((batch, seq, v_w), F32),
            jax.ShapeDtypeStruct((depth, batch, M_HEADS, M_DV, M_DK), F32),
            jax.ShapeDtypeStruct((batch, M_HEADS, 1, M_DK), F32),
            jax.ShapeDtypeStruct((batch, M_HEADS, 1, 1), F32),
        ],
        scratch_shapes=[
            pltpu.VMEM((nseq, M_HEADS, M_DV, M_DK), F32),
            pltpu.VMEM((nseq, M_HEADS, 1, M_DK), F32),
            pltpu.VMEM((nseq, M_HEADS, 1, 1), F32),
        ],
        input_output_aliases=aliases,
        compiler_params=_params(("arbitrary", "arbitrary"), blocks),
    )(*args)


def _select(pick, x):
    hi = x.astype(BF16)
    rest = x - hi.astype(F32)
    mid = rest.astype(BF16)
    lo = (rest - mid.astype(F32)).astype(BF16)
    pick = pick.astype(F32).astype(BF16)
    part = lambda p: jnp.dot(pick, p, preferred_element_type=F32)
    return (part(hi) + part(mid)) + part(lo)


def _seq_rows(x, offset, *, seq):
    rows = x.shape[0]
    i = lax.broadcasted_iota(jnp.int32, (rows // seq, rows), 0)
    r = lax.broadcasted_iota(jnp.int32, (rows // seq, rows), 1)
    return _select(r == i * seq + offset, x)


def _seq_place(per_seq, offset, *, seq):
    nseq = per_seq.shape[0]
    r = lax.broadcasted_iota(jnp.int32, (nseq * seq, nseq), 0)
    i = lax.broadcasted_iota(jnp.int32, (nseq * seq, nseq), 1)
    return _select(r == i * seq + offset, per_seq)


def _shift_rows(x, shift, *, seq, tile_has_whole_seqs, before):
    rows = x.shape[0]
    rolled = pltpu.roll(x, shift, 0)
    r = lax.broadcasted_iota(jnp.int32, (rows, 1), 0)
    if tile_has_whole_seqs:
        head = _seq_place(before[0], 0, seq=seq)
        for s in range(1, shift):
            head = head + _seq_place(before[s], s, seq=seq)
        return jnp.where((r % seq) < shift, head, rolled)
    out = rolled
    for s in range(shift):
        out = jnp.where(r == s, before[s:s + 1, :], out)
    return out


def _scan_block(seq):
    return _pick_tile(seq, (32, 16, 8))


def _hk_to_kh(a, axis):
    shape = a.shape
    a = a.reshape(shape[:axis] + (R_HEADS, R_HD) + shape[axis + 1:])
    return jnp.swapaxes(a, axis, axis + 1).reshape(shape)


def _kh_to_hk(a, axis):
    shape = a.shape
    a = a.reshape(shape[:axis] + (R_HD, R_HEADS) + shape[axis + 1:])
    return jnp.swapaxes(a, axis, axis + 1).reshape(shape)


def _sections(a, axis, fn):
    rw = R_HEADS * R_HD
    idx = lambda lo, hi: tuple(slice(lo, hi) if d == axis else slice(None) for d in range(a.ndim))
    parts = [fn(a[idx(i * rw, (i + 1) * rw)], axis) for i in range(3)]
    if a.shape[axis] > 3 * rw:
        parts.append(a[idx(3 * rw, a.shape[axis])])
    return jnp.concatenate(parts, axis=axis)


def _rwkv_prep_kernel(x_ref, gn_ref, wp_ref, bp_ref, st_ref, mu_ref, wl_ref, w0_ref, a0_ref,
                      rr_ref, rk_ref, rv_ref, dec_ref, a_ref, g_ref, sh_ref, carry_sc,
                      *, seq, whole_seqs):
    xin = x_ref[...]
    tb = xin.shape[0]
    rw = R_HEADS * R_HD
    ms = jnp.mean(xin * xin, axis=-1, keepdims=True)
    h = (xin * lax.rsqrt(ms + RMS_EPS) * gn_ref[...]).astype(BF16)
    if not whole_seqs:
        @pl.when(pl.program_id(1) == 0)
        def _():
            carry_sc[...] = st_ref[0]

    def proj(lo, hi):
        return jnp.dot(h, wp_ref[:, lo:hi], preferred_element_type=F32) + bp_ref[:, lo:hi]

    def shifted(x, lo, hi):
        if whole_seqs:
            prev = _shift_rows(x, 1, seq=seq, tile_has_whole_seqs=True, before=[st_ref[:, lo:hi]])
            sh_ref[:, lo:hi] = _seq_rows(x, seq - 1, seq=seq)
        else:
            prev = _shift_rows(x, 1, seq=seq, tile_has_whole_seqs=False,
                               before=carry_sc[:, lo:hi])
            carry_sc[:, lo:hi] = x[tb - 1:tb, :]
            sh_ref[0, :, lo:hi] = x[tb - 1:tb, :]
        return x + (prev - x) * mu_ref[:, lo:hi]

    x_lora = proj(3 * rw, 3 * rw + R_LORA)
    x_r = proj(0, rw)
    z = shifted(x_lora, 3 * rw, 3 * rw + R_LORA)
    lane = lax.broadcasted_iota(jnp.int32, (1, R_LORA), 1)
    act = jnp.where(lane < 64, jnp.tanh(z), jnp.where(lane < 128, z, _sigmoid(z))).astype(BF16)
    lo_w = jnp.dot(act, wl_ref[:, 0:rw], preferred_element_type=F32)
    x_k = proj(rw, 2 * rw)
    rr_ref[...] = shifted(x_r, 0, rw)
    lo_a = jnp.dot(act, wl_ref[:, rw:2 * rw], preferred_element_type=F32)
    x_v = proj(2 * rw, 3 * rw)
    rk_ref[...] = shifted(x_k, rw, 2 * rw)
    dec_ref[...] = jnp.exp(-jnp.exp(-_softplus(-(w0_ref[...] + lo_w)) - 0.5))
    g_ref[...] = jnp.dot(act, wl_ref[:, 2 * rw:3 * rw], preferred_element_type=F32)
    rv_ref[...] = shifted(x_v, 2 * rw, 3 * rw)
    a_ref[...] = _sigmoid(a0_ref[...] + lo_a)


def _rwkv_prep(x, gn, w_rw, b_rw, shift_state, mu, w_lora, w0, a0, *, row0, batch, seq, name):
    d = x.shape[1]
    c = w_rw.shape[1]
    rw = R_HEADS * R_HD
    n_tok = batch * seq
    whole = seq <= 64
    if whole:
        tb = _pick_tile(n_tok, (256, 128, 64, 32, 16, 8))
        assert tb % seq == 0 and row0 % tb == 0
        grid = (n_tok // tb, 1)
        rb0 = row0 // tb
        tok_map = lambda i, j: (rb0 + i, 0)
        out_map = lambda i, j: (i, 0)
        state = shift_state
        st_spec = pl.BlockSpec((tb // seq, c), out_map)
        sh_shape = jax.ShapeDtypeStruct((batch, c), F32)
    else:
        tb = _pick_tile(seq, (256, 128, 64, 32, 16, 8))
        nt = seq // tb
        grid = (batch, nt)
        rb0 = row0 // tb
        tok_map = lambda b, j: (rb0 + b * nt + j, 0)
        out_map = lambda b, j: (b * nt + j, 0)
        state = shift_state.reshape(batch, 1, c)
        st_spec = pl.BlockSpec((1, 1, c), lambda b, j: (b, 0, 0))
        sh_shape = jax.ShapeDtypeStruct((batch, 1, c), F32)
    fixed = lambda i, j: (0, 0)
    tok_out = jax.ShapeDtypeStruct((n_tok, rw), F32)
    tok_spec = pl.BlockSpec((tb, rw), out_map)
    blocks = (4 * _nbytes((tb, c), F32) + 6 * _nbytes((tb, rw), F32) + _nbytes(w_lora.shape, BF16)
              + _nbytes(w_rw.shape, BF16))
    outs = pl.pallas_call(
        functools.partial(_rwkv_prep_kernel, seq=seq, whole_seqs=whole),
        name=name,
        grid=grid,
        in_specs=[
            pl.BlockSpec((tb, d), tok_map),
            pl.BlockSpec((1, d), fixed),
            pl.BlockSpec(w_rw.shape, fixed),
            pl.BlockSpec((1, c), fixed),
            st_spec,
            pl.BlockSpec((1, c), fixed),
            pl.BlockSpec(w_lora.shape, fixed),
            pl.BlockSpec((1, rw), fixed),
            pl.BlockSpec((1, rw), fixed),
        ],
        out_specs=[tok_spec] * 6 + [st_spec],
        out_shape=[tok_out] * 6 + [sh_shape],
        scratch_shapes=[pltpu.VMEM((1, c), F32)],
        compiler_params=_params(("arbitrary", "arbitrary"), blocks),
    )(x, gn, w_rw, b_rw, state, mu, w_lora, w0, a0)
    return (*outs[:6], outs[6].reshape(batch, c))


def _rwkv_scan_kernel(rr_ref, rk_ref, dec_ref, a_ref, rv_ref, kks_ref, ka_ref, rrk_ref,
                      gng_ref, gnb_ref, s0_ref, y_ref, s1_ref,
                      s_sc, w_sc, kka_sc, km_sc, nkk_sc, wr_sc, coef_sc, *, tb):
    j = pl.program_id(1)
    nk = R_HD

    @pl.when(j == 0)
    def _():
        s_sc[...] = s0_ref[0]

    rr = rr_ref[...]
    rk = rk_ref[...]
    a = a_ref[...]
    dec = dec_ref[...]
    kk = rk * kks_ref[...]
    nrm = jnp.sqrt(jnp.sum(kk * kk, axis=0, keepdims=True))
    kk = kk * (1.0 / jnp.maximum(nrm, 1e-12))
    kmod = rk * (1.0 + (a - 1.0) * ka_ref[...])
    kka = kk * a
    g2 = dec.reshape(nk * tb, LANES)
    tpos = lax.broadcasted_iota(jnp.int32, (nk * tb, 1), 0) % tb
    shift = 1
    while shift < tb:
        g2 = g2 * jnp.where(tpos >= shift, pltpu.roll(g2, shift, 0), 1.0)
        shift *= 2
    g_prev = jnp.where(tpos >= 1, pltpu.roll(g2, 1, 0), 1.0).reshape(nk, tb, LANES)
    g_run = g2.reshape(nk, tb, LANES)
    g_inv = 1.0 / g_run
    w_sc[...] = g_run
    kka_sc[...] = kka * g_inv
    km_sc[...] = kmod * g_inv
    nkk_sc[...] = -kk * g_prev
    wr_sc[...] = rr * g_run
    coef_sc[0] = jnp.sum(kka * rr, axis=0)
    coef_sc[1] = jnp.sum(kmod * rr, axis=0)
    coef_sc[2] = jnp.sum(rr * kmod * rrk_ref[...], axis=0)

    def row(ref, t, k):
        return jnp.broadcast_to(ref[k, pl.ds(t, 1), :], (nk, LANES))

    def coef(i, t):
        return jnp.broadcast_to(coef_sc[i, pl.ds(t, 1), :], (nk, LANES))

    def matvecs(t):
        acc_sa = jnp.zeros((nk, LANES), F32)
        acc_y = jnp.zeros((nk, LANES), F32)
        for k in range(nk):
            sk = s_sc[k]
            acc_sa = acc_sa + sk * row(nkk_sc, t, k)
            acc_y = acc_y + sk * row(wr_sc, t, k)
        return acc_sa, acc_y

    def emit(t, sa, y0):
        val = rv_ref[t]
        y = y0 + sa * coef(0, t) + val * coef(1, t)
        mean = jnp.sum(y, axis=0, keepdims=True) * (1.0 / nk)
        d = y - mean
        var = jnp.sum(d * d, axis=0, keepdims=True) * (1.0 / nk)
        yn = d * lax.rsqrt(var + GN_EPS) * gng_ref[...] + gnb_ref[...]
        y_ref[t] = yn + coef(2, t) * val
        return val

    def step(t, carry):
        sa, y0 = carry
        val = emit(t, sa, y0)
        acc_sa = jnp.zeros((nk, LANES), F32)
        acc_y = jnp.zeros((nk, LANES), F32)
        for k in range(nk):
            sk = s_sc[k] + sa * row(kka_sc, t, k) + val * row(km_sc, t, k)
            s_sc[k] = sk
            acc_sa = acc_sa + sk * row(nkk_sc, t + 1, k)
            acc_y = acc_y + sk * row(wr_sc, t + 1, k)
        return acc_sa, acc_y

    sa, y0 = lax.fori_loop(0, tb - 1, step, matvecs(0))
    t_last = tb - 1
    val = emit(t_last, sa, y0)
    for k in range(nk):
        s_sc[k] = ((s_sc[k] + sa * row(kka_sc, t_last, k) + val * row(km_sc, t_last, k))
                   * row(w_sc, t_last, k))

    @pl.when(j == pl.num_programs(1) - 1)
    def _():
        s1_ref[0] = s_sc[...]


LAYOUT_TB = 128


def _lane_slabs(in_ref, p):
    nb = HEADS_PER_LANE_GROUP
    zts = [in_ref[b, :, p * LANES:(p + 1) * LANES].T for b in range(nb)]
    for i in range(SUBLANES):
        d = jnp.concatenate([zt[i * R_HEADS:(i + 1) * R_HEADS, :] for zt in zts], axis=0)
        yield SUBLANES * p + i, d.T


def _lanes_in_kernel(src_ref, dst_ref, *, rows):
    for p in range(R_HD // SUBLANES):
        for i, slab in _lane_slabs(src_ref, p):
            if rows:
                dst_ref[i] = slab
            else:
                dst_ref[:, pl.ds(i, 1), :] = slab[:, None, :]


def _lanes_in(x, *, batch, seq, rows):
    nb = HEADS_PER_LANE_GROUP
    g = batch // nb
    w = R_HEADS * R_HD
    if seq % LAYOUT_TB != 0 and (g * seq) % LAYOUT_TB == 0:
        x = x.reshape(g, nb, seq, w).transpose(1, 0, 2, 3).reshape(nb * g * seq, w)
        return _lanes_in(x, batch=nb, seq=g * seq, rows=rows)
    if seq % LAYOUT_TB != 0:
        x = x.reshape(g, nb, seq, R_HD, R_HEADS)
        if rows:
            return x.transpose(3, 0, 2, 1, 4).reshape(R_HD, g * seq, LANES)
        return x.transpose(0, 2, 3, 1, 4).reshape(g * seq, R_HD, LANES)
    nt = seq // LAYOUT_TB
    if rows:
        out_spec = pl.BlockSpec((R_HD, LAYOUT_TB, LANES), lambda gi, j: (0, gi * nt + j, 0))
        out_shape = jax.ShapeDtypeStruct((R_HD, g * seq, LANES), F32)
    else:
        out_spec = pl.BlockSpec((LAYOUT_TB, R_HD, LANES), lambda gi, j: (gi * nt + j, 0, 0))
        out_shape = jax.ShapeDtypeStruct((g * seq, R_HD, LANES), F32)
    return pl.pallas_call(
        functools.partial(_lanes_in_kernel, rows=rows),
        name="lanes_in_rows" if rows else "lanes_in_full",
        grid=(g, nt),
        in_specs=[pl.BlockSpec((nb, LAYOUT_TB, w), lambda gi, j: (gi, j, 0))],
        out_specs=out_spec,
        out_shape=out_shape,
        compiler_params=_params(("arbitrary", "arbitrary"), 2 * _nbytes((nb, LAYOUT_TB, w), F32)),
    )(x.reshape(batch, seq, w))


def _lanes_out_kernel(y_ref, o_ref):
    nb = HEADS_PER_LANE_GROUP
    for p in range(R_HD // SUBLANES):
        pieces = [[] for _ in range(nb)]
        for i in range(SUBLANES):
            st = y_ref[:, SUBLANES * p + i, :].T
            for b in range(nb):
                pieces[b].append(st[b * R_HEADS:(b + 1) * R_HEADS, :])
        for b in range(nb):
            zt = jnp.concatenate(pieces[b], axis=0)
            o_ref[b, :, p * LANES:(p + 1) * LANES] = zt.T


def _lanes_out(y, *, batch, seq):
    nb = HEADS_PER_LANE_GROUP
    g = batch // nb
    w = R_HEADS * R_HD
    if seq % LAYOUT_TB != 0 and (g * seq) % LAYOUT_TB == 0:
        tok = _lanes_out(y, batch=nb, seq=g * seq)
        return tok.reshape(nb, g, seq, w).transpose(1, 0, 2, 3).reshape(batch * seq, w)
    if seq % LAYOUT_TB != 0:
        y = y.reshape(g, seq, R_HD, nb, R_HEADS)
        return y.transpose(0, 3, 1, 2, 4).reshape(batch * seq, w)
    nt = seq // LAYOUT_TB
    out = pl.pallas_call(
        _lanes_out_kernel,
        name="lanes_out",
        grid=(g, nt),
        in_specs=[pl.BlockSpec((LAYOUT_TB, R_HD, LANES), lambda gi, j: (gi * nt + j, 0, 0))],
        out_specs=pl.BlockSpec((nb, LAYOUT_TB, w), lambda gi, j: (gi, j, 0)),
        out_shape=jax.ShapeDtypeStruct((batch, seq, w), F32),
        compiler_params=_params(("arbitrary", "arbitrary"), 2 * _nbytes((nb, LAYOUT_TB, w), F32)),
    )(y)
    return out.reshape(batch * seq, w)


def _param_lanes(p):
    p = p.reshape(R_HEADS, R_HD).T
    return jnp.tile(p, (1, HEADS_PER_LANE_GROUP))


def _rwkv_scan(rr, rk, rv, dec, a, kks, ka, rrk, gng, gnb, s0, *, batch, seq, name):
    assert batch % HEADS_PER_LANE_GROUP == 0
    g = batch // HEADS_PER_LANE_GROUP
    tb = _scan_block(seq)
    nt = seq // tb
    rows = [_lanes_in(v, batch=batch, seq=seq, rows=True) for v in (rr, rk, dec, a)]
    val = _lanes_in(rv, batch=batch, seq=seq, rows=False)
    s0l = s0.reshape(g, HEADS_PER_LANE_GROUP, R_HEADS, R_HD, R_HD).transpose(0, 4, 3, 1, 2)
    s0l = s0l.reshape(g, R_HD, R_HD, LANES)
    row_spec = pl.BlockSpec((R_HD, tb, LANES), lambda gi, j: (0, gi * nt + j, 0))
    val_spec = pl.BlockSpec((tb, R_HD, LANES), lambda gi, j: (gi * nt + j, 0, 0))
    kpar_spec = pl.BlockSpec((R_HD, 1, LANES), lambda gi, j: (0, 0, 0))
    vpar_spec = pl.BlockSpec((R_HD, LANES), lambda gi, j: (0, 0))
    st_spec = pl.BlockSpec((1, R_HD, R_HD, LANES), lambda gi, j: (gi, 0, 0, 0))
    vec_bytes = _nbytes((tb, R_HD, LANES), F32)
    blocks = 11 * vec_bytes + 3 * _nbytes((R_HD, R_HD, LANES), F32)
    kpar = lambda p: _param_lanes(p)[:, None, :]
    y, s1l = pl.pallas_call(
        functools.partial(_rwkv_scan_kernel, tb=tb),
        name=name,
        grid=(g, nt),
        in_specs=[row_spec] * 4 + [val_spec] + [kpar_spec] * 3 + [vpar_spec] * 2 + [st_spec],
        out_specs=[val_spec, st_spec],
        out_shape=[
            jax.ShapeDtypeStruct((g * seq, R_HD, LANES), F32),
            jax.ShapeDtypeStruct(s0l.shape, F32),
        ],
        scratch_shapes=[pltpu.VMEM((R_HD, R_HD, LANES), F32)]
        + [pltpu.VMEM((R_HD, tb, LANES), F32)] * 5
        + [pltpu.VMEM((3, tb, LANES), F32)],
        compiler_params=_params(("arbitrary", "arbitrary"), blocks),
    )(*rows, val, kpar(kks), kpar(ka), kpar(rrk), _param_lanes(gng), _param_lanes(gnb), s0l)
    s1 = s1l.reshape(g, R_HD, R_HD, HEADS_PER_LANE_GROUP, R_HEADS).transpose(0, 3, 4, 2, 1)
    return _lanes_out(y, batch=batch, seq=seq), s1.reshape(batch, R_HEADS, R_HD, R_HD)


def _mix_out_kernel(hm_ref, yr_ref, g_ref, gm_ref, gr_ref, x_ref, wm_ref, wr_ref, wo_ref, *rest):
    o_ref = rest[-1]
    bm = jnp.dot(hm_ref[...].astype(BF16), wm_ref[...], preferred_element_type=F32)
    yr = (yr_ref[...] * g_ref[...]).astype(BF16)
    br = jnp.dot(yr, wr_ref[...], preferred_element_type=F32)
    merged = (_sigmoid(gm_ref[...].astype(F32)) * bm
              + _sigmoid(gr_ref[...].astype(F32)) * br)
    o_ref[...] = x_ref[...] + jnp.dot(merged.astype(BF16), wo_ref[...], preferred_element_type=F32)


def _mix_out(hm, yr, g, p_mix, x, wm, wr, wo, out_buf, *, row0, name):
    n_all, d = x.shape
    n = yr.shape[0]
    tm = _pick_tile(n, (256, 128, 64, 32, 16, 8))
    assert row0 % tm == 0
    rb0 = row0 // tm
    gm_blk = (2 * M_HEADS * (M_DK + M_DV)) // d
    shared = lambda i: (rb0 + i, 0)
    own = lambda i: (i, 0)
    fixed = lambda i: (0, 0)
    aliased = out_buf is not None
    in_specs = [
        pl.BlockSpec((tm, wm.shape[0]), own),
        pl.BlockSpec((tm, wr.shape[0]), own),
        pl.BlockSpec((tm, wr.shape[0]), own),
        pl.BlockSpec((tm, d), lambda i: (rb0 + i, gm_blk)),
        pl.BlockSpec((tm, d), lambda i: (rb0 + i, gm_blk + 1)),
        pl.BlockSpec((tm, d), shared),
        pl.BlockSpec(wm.shape, fixed),
        pl.BlockSpec(wr.shape, fixed),
        pl.BlockSpec(wo.shape, fixed),
    ]
    args = [hm, yr, g, p_mix, p_mix, x, wm, wr, wo]
    if aliased:
        in_specs.append(ANY_SPEC)
        args.append(out_buf)
    blocks = 7 * _nbytes((tm, d), F32) + 3 * _nbytes((d, d), BF16)
    return pl.pallas_call(
        _mix_out_kernel,
        name=name,
        grid=(n // tm,),
        in_specs=in_specs,
        out_specs=pl.BlockSpec((tm, d), shared),
        out_shape=jax.ShapeDtypeStruct((n_all, d), F32),
        input_output_aliases={9: 0} if aliased else {},
        compiler_params=_params(("arbitrary",), blocks),
    )(*args)


FFN_CHUNK = 256


def _ffn_kernel(x_ref, gn_ref, wu_ref, cs_ref, cw_ref, cb_ref, wd_ref, *rest, seq, whole_seqs):
    o_ref, st_ref, carry_sc = rest[-3:]
    x = x_ref[...]
    tb = x.shape[0]
    dff = wd_ref.shape[0]
    ms = jnp.mean(x * x, axis=-1, keepdims=True)
    h = (x * lax.rsqrt(ms + RMS_EPS) * gn_ref[...]).astype(BF16)
    if not whole_seqs:
        @pl.when(pl.program_id(1) == 0)
        def _():
            carry_sc[...] = cs_ref[0]
    def up(c0):
        return [jnp.dot(h, wu_ref[:, lo:lo + FFN_CHUNK], preferred_element_type=F32)
                for lo in (c0, dff + c0)]

    def down(acc, act, c0):
        return acc + jnp.dot(act, wd_ref[c0:c0 + FFN_CHUNK, :], preferred_element_type=F32)

    acc = x
    starts = list(range(0, dff, FFN_CHUNK))
    u_next = up(starts[0])
    pending = None
    for ci, c0 in enumerate(starts):
        u_cur = u_next
        if ci + 1 < len(starts):
            u_next = up(starts[ci + 1])
        if pending is not None:
            acc = down(acc, *pending)
        halves = []
        for lo, u in zip((c0, dff + c0), u_cur):
            cols = slice(lo, lo + FFN_CHUNK)
            if whole_seqs:
                before = [cs_ref[0, :, cols], cs_ref[1, :, cols]]
                p1 = _shift_rows(u, 1, seq=seq, tile_has_whole_seqs=True, before=before[1:])
                p2 = _shift_rows(u, 2, seq=seq, tile_has_whole_seqs=True, before=before)
                st_ref[0, :, cols] = _seq_rows(u, seq - 2, seq=seq)
                st_ref[1, :, cols] = _seq_rows(u, seq - 1, seq=seq)
            else:
                carry = carry_sc[:, cols]
                p1 = _shift_rows(u, 1, seq=seq, tile_has_whole_seqs=False, before=carry[1:2, :])
                p2 = _shift_rows(u, 2, seq=seq, tile_has_whole_seqs=False, before=carry)
                carry_sc[:, cols] = u[tb - 2:tb, :]
                st_ref[0, :, cols] = u[tb - 2:tb, :]
            halves.append(cb_ref[:, cols] + p2 * cw_ref[0:1, cols] + p1 * cw_ref[1:2, cols]
                          + u * cw_ref[2:3, cols])
        gate, val = halves
        pending = ((gate * _sigmoid(gate) * val).astype(BF16), c0)
    o_ref[...] = down(acc, *pending)


def _ffn(x, gn, wu, conv_state, cw, cb, wd, out_buf, *, row0, batch, seq, name):
    n_all, d = x.shape
    c = wu.shape[1]
    n_tok = batch * seq
    taps = cw.shape[0]
    assert taps == 3 and seq >= taps - 1 and wd.shape[0] % FFN_CHUNK == 0
    whole = seq <= 64
    if whole:
        tb = _pick_tile(n_tok, (128, 64, 32, 16, 8))
        assert tb % seq == 0 and row0 % tb == 0
        grid = (n_tok // tb, 1)
        rb0 = row0 // tb
        tok_map = lambda i, j: (rb0 + i, 0)
        state = jnp.swapaxes(conv_state, 0, 1)
        st_shape = jax.ShapeDtypeStruct((taps - 1, batch, c), F32)
        st_spec = pl.BlockSpec((taps - 1, tb // seq, c), lambda i, j: (0, i, 0))
    else:
        tb = _pick_tile(seq, (256, 128, 64, 32, 16, 8))
        nt = seq // tb
        grid = (batch, nt)
        rb0 = row0 // tb
        tok_map = lambda b, j: (rb0 + b * nt + j, 0)
        state = conv_state
        st_shape = jax.ShapeDtypeStruct((batch, taps - 1, c), F32)
        st_spec = pl.BlockSpec((1, taps - 1, c), lambda b, j: (b, 0, 0))
    fixed = lambda i, j: (0, 0)
    aliased = out_buf is not None
    in_specs = [
        pl.BlockSpec((tb, d), tok_map),
        pl.BlockSpec((1, d), fixed),
        pl.BlockSpec(wu.shape, fixed),
        st_spec,
        pl.BlockSpec(cw.shape, fixed),
        pl.BlockSpec((1, c), fixed),
        pl.BlockSpec(wd.shape, fixed),
    ]
    args = [x, gn, wu, state, cw, cb, wd]
    if aliased:
        in_specs.append(ANY_SPEC)
        args.append(out_buf)
    blocks = _nbytes(wu.shape, BF16) + _nbytes(wd.shape, BF16) + 6 * _nbytes((tb, d), F32)
    y, st = pl.pallas_call(
        functools.partial(_ffn_kernel, seq=seq, whole_seqs=whole),
        name=name,
        grid=grid,
        in_specs=in_specs,
        out_specs=[pl.BlockSpec((tb, d), tok_map), st_spec],
        out_shape=[jax.ShapeDtypeStruct((n_all, d), F32), st_shape],
        scratch_shapes=[pltpu.VMEM((taps - 1, c), F32)],
        input_output_aliases={7: 0} if aliased else {},
        compiler_params=_params(("arbitrary", "arbitrary"), blocks),
    )(*args)
    return y, (jnp.swapaxes(st, 0, 1) if whole else st)


def _final_norm_kernel(x_ref, g_ref, o_ref):
    x = x_ref[...]
    ms = jnp.mean(x * x, axis=-1, keepdims=True)
    o_ref[...] = x * lax.rsqrt(ms + RMS_EPS) * g_ref[...]


def _final_norm(x, g, *, row0, n, name):
    d = x.shape[1]
    tm = _pick_tile(n, (1024, 512, 256, 128, 64, 32, 16, 8))
    assert row0 % tm == 0
    rb0 = row0 // tm
    return pl.pallas_call(
        _final_norm_kernel,
        name=name,
        grid=(n // tm,),
        in_specs=[pl.BlockSpec((tm, d), lambda i: (rb0 + i, 0)),
                  pl.BlockSpec((1, d), lambda i: (0, 0))],
        out_specs=pl.BlockSpec((tm, d), lambda i: (i, 0)),
        out_shape=jax.ShapeDtypeStruct((n, d), F32),
        compiler_params=_params(("arbitrary",), 2 * _nbytes((tm, d), F32)),
    )(x, g)


def kernel(x_prompt, x_sample, state_mlstm_C, state_mlstm_n, state_mlstm_m, state_rwkv_S, state_rwkv_shift, state_ffn_conv, norm_mix_g, w_in, b_in, mu_shift, m_norm_g, r_w0, r_w_up, r_a0, r_a_up, r_g_up, r_k_k, r_k_a, r_r_k, r_gn_g, r_gn_b, w_br_m, w_br_r, w_out, norm_ffn_g, w_up, conv_w, conv_b, w_down, norm_final_g):
    bp, tp, d = x_prompt.shape
    bs, ts, _ = x_sample.shape
    depth = w_in.shape[0]
    n_p = bp * tp
    n_s = bs * ts
    rw = R_HEADS * R_HD
    qkvo = 2 * M_HEADS * (M_DK + M_DV)
    n_gate = 2 * M_HEADS
    mix_cols = qkvo + n_gate + 2 * d
    rw_cols = 3 * rw + R_LORA
    dff2 = w_up.shape[2]
    assert w_in.shape[2] == mix_cols + rw_cols

    x = jnp.concatenate([x_prompt.reshape(n_p, d), x_sample.reshape(n_s, d)], axis=0)

    zc = jnp.zeros((1, bp, M_HEADS, M_DV, M_DK), F32)
    c_all_p = c_all_s = None
    zn = jnp.zeros((bp, M_HEADS, M_DK), F32)
    zm = jnp.zeros((bp, M_HEADS), F32)
    zs = jnp.zeros((bp, R_HEADS, R_HD, R_HD), F32)
    zsh = jnp.zeros((bp, rw_cols), F32)
    zcv = jnp.zeros((bp, conv_w.shape[1] - 1, dff2), F32)

    outs_p = [[] for _ in range(5)]
    outs_s = [[] for _ in range(5)]
    for l in range(depth):
        wl = w_in[l]
        bl = b_in[l]
        w_mix = jnp.concatenate([wl[:, :qkvo], wl[:, qkvo + n_gate:mix_cols]], axis=1).astype(BF16)
        b_mix = jnp.concatenate([bl[:qkvo], bl[qkvo + n_gate:mix_cols]])[None]
        w_gate = jnp.pad(wl[:, qkvo:qkvo + n_gate], ((0, 0), (0, LANES - n_gate))).astype(BF16)
        b_gate = jnp.pad(bl[qkvo:qkvo + n_gate], (0, LANES - n_gate))[None]
        w_rw = _sections(wl[:, mix_cols:], 1, _hk_to_kh).astype(BF16)
        b_rw = _sections(bl[mix_cols:], 0, _hk_to_kh)[None]
        mu_rw = _sections(mu_shift[l], 0, _hk_to_kh)[None]
        w_lora = jnp.zeros((R_LORA, 3 * rw), F32)
        w_lora = w_lora.at[0:64, 0:rw].set(r_w_up[l]).at[64:128, rw:2 * rw].set(r_a_up[l])
        w_lora = _sections(w_lora.at[128:256, 2 * rw:].set(r_g_up[l]), 1, _hk_to_kh).astype(BF16)
        w_br_r_kh = _hk_to_kh(w_br_r[l], 0).astype(BF16)
        g_mix = norm_mix_g[l][None]

        p_mix, gates = _in_proj(x, g_mix, w_mix, b_mix, w_gate, b_gate, tn=1024,
                                name="in_proj_mix", out_dtype=BF16)

        gn = m_norm_g[l].reshape(1, M_HEADS * M_DV)
        hm_p, c_all_p, nn_p, m_p = _mlstm(
            p_mix, gates, zc, zn, zm, gn, c_all_p, layer=l, depth=depth,
            row0=0, batch=bp, seq=tp, name="mlstm_prompt")
        hm_s, c_all_s, nn_s, m_s = _mlstm(
            p_mix, gates, state_mlstm_C, state_mlstm_n[l], state_mlstm_m[l], gn, c_all_s,
            layer=l, depth=depth, row0=n_p, batch=bs, seq=ts, name="mlstm_sample")

        prep_args = (mu_rw, w_lora, _hk_to_kh(r_w0[l], 0)[None], _hk_to_kh(r_a0[l], 0)[None])
        rr_p, rk_p, rv_p, wp_p, ap_p, g_p, sh_p = _rwkv_prep(
            x, g_mix, w_rw, b_rw, zsh, *prep_args, row0=0, batch=bp, seq=tp,
            name="rwkv_prep_prompt")
        rr_s, rk_s, rv_s, wp_s, ap_s, g_s, sh_s = _rwkv_prep(
            x, g_mix, w_rw, b_rw, _sections(state_rwkv_shift[l], 1, _hk_to_kh), *prep_args,
            row0=n_p, batch=bs, seq=ts, name="rwkv_prep_sample")
        sh_p = _sections(sh_p, 1, _kh_to_hk)
        sh_s = _sections(sh_s, 1, _kh_to_hk)
        scan_args = (r_k_k[l], r_k_a[l], r_r_k[l], r_gn_g[l], r_gn_b[l])
        y_p, s_p = _rwkv_scan(rr_p, rk_p, rv_p, wp_p, ap_p, *scan_args, zs,
                              batch=bp, seq=tp, name="rwkv_scan_prompt")
        y_s, s_s = _rwkv_scan(rr_s, rk_s, rv_s, wp_s, ap_s, *scan_args, state_rwkv_S[l],
                              batch=bs, seq=ts, name="rwkv_scan_sample")

        w_branches = (w_br_m[l].astype(BF16), w_br_r_kh, w_out[l].astype(BF16))
        v_w = M_HEADS * M_DV
        x_mid = _mix_out(hm_p.reshape(n_p, v_w), y_p, g_p, p_mix, x, *w_branches, None, row0=0,
                         name="mix_out_prompt")
        x_mid = _mix_out(hm_s.reshape(n_s, v_w), y_s, g_s, p_mix, x, *w_branches, x_mid, row0=n_p,
                         name="mix_out_sample")

        ffn_w = (conv_w[l], conv_b[l][None], w_down[l].astype(BF16))
        g_ffn = norm_ffn_g[l][None]
        wu = w_up[l].astype(BF16)
        x, cv_p = _ffn(x_mid, g_ffn, wu, zcv, *ffn_w, None, row0=0, batch=bp, seq=tp,
                       name="ffn_prompt")
        x, cv_s = _ffn(x_mid, g_ffn, wu, state_ffn_conv[l], *ffn_w, x, row0=n_p, batch=bs, seq=ts,
                       name="ffn_sample")

        for lst, val in zip(outs_p, (nn_p.reshape(bp, M_HEADS, M_DK), m_p.reshape(bp, M_HEADS),
                                     s_p, sh_p, cv_p)):
            lst.append(val)
        for lst, val in zip(outs_s, (nn_s.reshape(bs, M_HEADS, M_DK), m_s.reshape(bs, M_HEADS),
                                     s_s, sh_s, cv_s)):
            lst.append(val)

    g_fin = norm_final_g[None]
    y_prompt = _final_norm(x, g_fin, row0=0, n=n_p, name="final_norm_prompt").reshape(bp, tp, d)
    y_sample = _final_norm(x, g_fin, row0=n_p, n=n_s, name="final_norm_sample").reshape(bs, ts, d)
    return (y_prompt, y_sample, c_all_p, *[jnp.stack(o) for o in outs_p],
            c_all_s, *[jnp.stack(o) for o in outs_s])
```

```python
import functools

import jax
import jax.numpy as jnp
from jax import lax
from jax.experimental import pallas as pl
from jax.experimental.pallas import tpu as pltpu

F32 = jnp.float32
BF16 = jnp.bfloat16

M_HEADS = 4
M_DK = 128
M_DV = 256
M_CHUNK = 64
R_HEADS = 16
R_HD = 64
R_LORA = 256
RMS_EPS = 1e-6
GN_EPS = 64e-5

LANES = 128
SUBLANES = 8
VMEM_LIMIT_CAP = 56 * 1024 * 1024
HEADS_PER_LANE_GROUP = LANES // R_HEADS

HI = lax.Precision.HIGHEST
ANY_SPEC = pl.BlockSpec(memory_space=pl.ANY)


def _vmem_limit(block_bytes):
    return int(min(VMEM_LIMIT_CAP, max(16 * 1024 * 1024, 3 * block_bytes)))


def _nbytes(shape, dtype):
    n = 1
    for s in shape:
        n *= s
    return n * jnp.dtype(dtype).itemsize


def _pick_tile(n, candidates):
    for c in candidates:
        if n % c == 0:
            return c
    raise ValueError(f"no tile in {candidates} divides {n}")


def _sigmoid(x):
    return 1.0 / (1.0 + jnp.exp(-x))


def _softplus(x):
    return jnp.maximum(x, 0.0) + jnp.log(1.0 + jnp.exp(-jnp.abs(x)))


def _params(semantics, block_bytes):
    return pltpu.CompilerParams(dimension_semantics=semantics,
                                vmem_limit_bytes=_vmem_limit(block_bytes))


def _in_proj_kernel(x_ref, g_ref, w_ref, b_ref, wg_ref, bg_ref, o_ref, og_ref, h_ref):
    @pl.when(pl.program_id(1) == 0)
    def _():
        x = x_ref[...]
        ms = jnp.mean(x * x, axis=-1, keepdims=True)
        h = (x * lax.rsqrt(ms + RMS_EPS) * g_ref[...]).astype(BF16)
        h_ref[...] = h
        og_ref[...] = jnp.dot(h, wg_ref[...], preferred_element_type=F32) + bg_ref[...]

    acc = jnp.dot(h_ref[...], w_ref[...], preferred_element_type=F32) + b_ref[...]
    o_ref[...] = acc.astype(o_ref.dtype)


def _in_proj(x, g, w, b, wg, bg, *, tn, name, out_dtype):
    n, d = x.shape
    c = w.shape[1]
    tm = _pick_tile(n, (1024, 512, 256, 128, 64, 32, 16, 8))
    blocks = (_nbytes((tm, d), F32) + _nbytes((d, tn), BF16) + _nbytes((tm, tn), F32)
              + _nbytes((tm, d), BF16) + _nbytes((d, LANES), BF16) + _nbytes((tm, LANES), F32))
    return pl.pallas_call(
        _in_proj_kernel,
        name=name,
        grid=(n // tm, c // tn),
        in_specs=[
            pl.BlockSpec((tm, d), lambda i, j: (i, 0)),
            pl.BlockSpec((1, d), lambda i, j: (0, 0)),
            pl.BlockSpec((d, tn), lambda i, j: (0, j)),
            pl.BlockSpec((1, tn), lambda i, j: (0, j)),
            pl.BlockSpec((d, LANES), lambda i, j: (0, 0)),
            pl.BlockSpec((1, LANES), lambda i, j: (0, 0)),
        ],
        out_specs=[pl.BlockSpec((tm, tn), lambda i, j: (i, j)),
                   pl.BlockSpec((tm, LANES), lambda i, j: (i, 0))],
        out_shape=[jax.ShapeDtypeStruct((n, c), out_dtype),
                   jax.ShapeDtypeStruct((n, LANES), F32)],
        scratch_shapes=[pltpu.VMEM((tm, d), BF16)],
        compiler_params=_params(("arbitrary", "arbitrary"), blocks),
    )(x, g, w, b, wg, bg)


def _mlstm_kernel(*refs, chunk, slots, per_slot):
    tok_refs = [refs[5 * a:5 * a + 5] for a in range(slots)]
    c0_ref, n0_ref, m0_ref, gn_ref = refs[5 * slots:5 * slots + 4]
    h_ref, c1_ref, n1_ref, m1_ref, c_sc, n_sc, m_sc = refs[-7:]
    L = chunk
    nseq = slots * per_slot
    ci = pl.program_id(1)

    @pl.when(ci == 0)
    def _():
        c_sc[...] = c0_ref[0]
        n_sc[...] = n0_ref[...]
        m_sc[...] = m0_ref[...]

    lane = lax.broadcasted_iota(jnp.int32, (1, LANES), 1)
    row = lax.broadcasted_iota(jnp.int32, (L, L), 0)
    col = lax.broadcasted_iota(jnp.int32, (L, L), 1)
    causal = row >= col
    q_all = [r[0][...].astype(F32) * (M_DK ** -0.5) for r in tok_refs]
    k_all = [r[1][...].astype(F32) for r in tok_refs]
    v_all = [r[2][...].astype(F32) for r in tok_refs]
    o_all = [r[3][...].astype(F32) for r in tok_refs]

    def rows_of(s):
        return s // per_slot, slice((s % per_slot) * L, (s % per_slot + 1) * L)

    nt_dims = (((1,), (1,)), ((), ()))
    chains = [(s, h) for s in range(nseq) for h in range(M_HEADS)]

    zs, z_rows = [], []
    tri = causal.astype(F32)
    eye8 = (lax.broadcasted_iota(jnp.int32, (SUBLANES, LANES), 0)
            == lax.broadcasted_iota(jnp.int32, (SUBLANES, LANES), 1)).astype(F32)
    for s in range(nseq):
        a, rows = rows_of(s)
        gates = tok_refs[a][4][rows, :]
        b_all = jnp.dot(tri, -_softplus(-gates), precision=HI, preferred_element_type=F32)
        zs.append(jnp.where(lane < M_HEADS, gates, b_all))
    for s in range(nseq):
        z_rows.append(lax.dot_general(eye8, zs[s], nt_dims, precision=HI,
                                      preferred_element_type=F32))

    def qkv(s, h):
        a, rows = rows_of(s)
        return (q_all[a][rows, h * M_DK:(h + 1) * M_DK], k_all[a][rows, h * M_DK:(h + 1) * M_DK],
                v_all[a][rows, h * M_DV:(h + 1) * M_DV])

    raw, qcs = [], []
    for s, h in chains:
        q, k, _ = qkv(s, h)
        raw.append(lax.dot_general(q, k, nt_dims, preferred_element_type=F32))
        qcs.append(lax.dot_general(q, c_sc[s, h], nt_dims, preferred_element_type=F32))

    scs, inters, m_ts = [], [], []
    for idx, (s, h) in enumerate(chains):
        i_row = z_rows[s][h:h + 1, :]
        b_row = z_rows[s][M_HEADS + h:M_HEADS + h + 1, :]
        b_col = zs[s][:, M_HEADS + h:M_HEADS + h + 1]
        dmat = jnp.where(causal, b_col - b_row + i_row, -jnp.inf)
        m_inter = b_col + m_sc[s, h]
        m_t = jnp.maximum(m_inter, jnp.max(dmat, axis=-1, keepdims=True))
        scs.append(raw[idx] * jnp.exp(dmat - m_t))
        inters.append(jnp.exp(m_inter - m_t))
        m_ts.append(m_t)

    svs = [jnp.dot(scs[idx], qkv(s, h)[2], preferred_element_type=F32)
           for idx, (s, h) in enumerate(chains)]

    upds = []
    for idx, (s, h) in enumerate(chains):
        _, k, v = qkv(s, h)
        i_col = zs[s][:, h:h + 1]
        b_col = zs[s][:, M_HEADS + h:M_HEADS + h + 1]
        m_new = m_ts[idx][L - 1:L, :]
        b_last = b_col[L - 1:L, :]
        w_s = jnp.exp(b_last - b_col + i_col - m_new)
        decay = jnp.exp(b_last + m_sc[s, h] - m_new)
        kv = lax.dot_general(v * w_s, k, (((0,), (0,)), ((), ())), preferred_element_type=F32)
        upds.append((decay, kv, jnp.sum(w_s * k, axis=0, keepdims=True), m_new))

    for idx, (s, h) in enumerate(chains):
        q, _, _ = qkv(s, h)
        a, rows = rows_of(s)
        num = svs[idx] + inters[idx] * qcs[idx]
        den = (jnp.sum(scs[idx], axis=-1, keepdims=True)
               + inters[idx] * jnp.sum(q * n_sc[s, h], axis=-1, keepdims=True))
        hh = num / jnp.maximum(jnp.abs(den), jnp.exp(-m_ts[idx]))
        hn = (hh * lax.rsqrt(jnp.mean(hh * hh, axis=-1, keepdims=True) + RMS_EPS)
              * gn_ref[:, h * M_DV:(h + 1) * M_DV])
        h_ref[s, :, h * M_DV:(h + 1) * M_DV] = hn * _sigmoid(
            o_all[a][rows, h * M_DV:(h + 1) * M_DV])

    for (s, h), (decay, kv, kn, m_new) in zip(chains, upds):
        c_sc[s, h] = decay * c_sc[s, h] + kv
        n_sc[s, h] = decay * n_sc[s, h] + kn
        m_sc[s, h] = m_new

    @pl.when(ci == pl.num_programs(1) - 1)
    def _():
        c1_ref[0] = c_sc[...]
        n1_ref[...] = n_sc[...]
        m1_ref[...] = m_sc[...]


def _mlstm(p_mix, gates, c0_all, n0, m0, gn, c_buf, *, layer, depth, row0, batch, seq, name):
    layer_in = min(layer, c0_all.shape[0] - 1)
    L = M_CHUNK if seq % M_CHUNK == 0 else seq
    nc = seq // L
    slots, per_slot = (_pick_tile(batch, (2, 1)), 1) if nc > 1 else (1, _pick_tile(batch, (4, 2, 1)))
    nseq = slots * per_slot
    rows = per_slot * L
    rb0 = row0 // rows
    assert row0 % rows == 0
    qk_w = M_HEADS * M_DK
    v_w = M_HEADS * M_DV
    assert qk_w * 2 == v_w

    def tok(a):
        return lambda bi, c: rb0 + (bi * slots + a) * nc + c

    def tok_spec(a, width, col_blk):
        return pl.BlockSpec((rows, width), lambda bi, c: (tok(a)(bi, c), col_blk))

    state_map = lambda bi, c: (bi, 0, 0, 0)
    c_blk = (1, nseq, M_HEADS, M_DV, M_DK)
    in_specs, args = [], []
    for a in range(slots):
        in_specs += [tok_spec(a, qk_w, 0), tok_spec(a, qk_w, 1), tok_spec(a, v_w, 1),
                     tok_spec(a, v_w, 2), tok_spec(a, LANES, 0)]
        args += [p_mix, p_mix, p_mix, p_mix, gates]
    in_specs += [
        pl.BlockSpec(c_blk, lambda bi, c: (layer_in, bi, 0, 0, 0)),
        pl.BlockSpec((nseq, M_HEADS, 1, M_DK), state_map),
        pl.BlockSpec((nseq, M_HEADS, 1, 1), state_map),
        pl.BlockSpec((1, v_w), lambda bi, c: (0, 0)),
    ]
    args += [c0_all, n0.reshape(batch, M_HEADS, 1, M_DK), m0.reshape(batch, M_HEADS, 1, 1), gn]
    aliases = {}
    if c_buf is not None:
        aliases[len(args)] = 1
        in_specs.append(ANY_SPEC)
        args.append(c_buf)
    blocks = (slots * (2 * _nbytes((rows, qk_w), F32) + 3 * _nbytes((rows, v_w), F32))
              + 3 * _nbytes((nseq, M_HEADS, M_DV, M_DK), F32))
    return pl.pallas_call(
        functools.partial(_mlstm_kernel, chunk=L, slots=slots, per_slot=per_slot),
        name=name,
        grid=(batch // nseq, nc),
        in_specs=in_specs,
        out_specs=[
            pl.BlockSpec((nseq, L, v_w), lambda bi, c: (bi, c, 0)),
            pl.BlockSpec(c_blk, lambda bi, c: (layer, bi, 0, 0, 0)),
            pl.BlockSpec((nseq, M_HEADS, 1, M_DK), state_map),
            pl.BlockSpec((nseq, M_HEADS, 1, 1), state_map),
        ],
        out_shape=[
            jax.ShapeDtypeStruct((batch, seq, v_w), F32),
            jax.ShapeDtypeStruct((depth, batch, M_HEADS, M_DV, M_DK), F32),
            jax.ShapeDtypeStruct((batch, M_HEADS, 1, M_DK), F32),
            jax.ShapeDtypeStruct((batch, M_HEADS, 1, 1), F32),
        ],
        scratch_shapes=[
            pltpu.VMEM((nseq, M_HEADS, M_DV, M_DK), F32),
            pltpu.VMEM((nseq, M_HEADS, 1, M_DK), F32),
            pltpu.VMEM((nseq, M_HEADS, 1, 1), F32),
        ],
        input_output_aliases=aliases,
        compiler_params=_params(("arbitrary", "arbitrary"), blocks),
    )(*args)


def _select(pick, x):
    hi = x.astype(BF16)
    rest = x - hi.astype(F32)
    mid = rest.astype(BF16)
    lo = (rest - mid.astype(F32)).astype(BF16)
    pick = pick.astype(F32).astype(BF16)
    part = lambda p: jnp.dot(pick, p, preferred_element_type=F32)
    return (part(hi) + part(mid)) + part(lo)


def _seq_rows(x, offset, *, seq):
    rows = x.shape[0]
    i = lax.broadcasted_iota(jnp.int32, (rows // seq, rows), 0)
    r = lax.broadcasted_iota(jnp.int32, (rows // seq, rows), 1)
    return _select(r == i * seq + offset, x)


def _seq_place(per_seq, offset, *, seq):
    nseq = per_seq.shape[0]
    r = lax.broadcasted_iota(jnp.int32, (nseq * seq, nseq), 0)
    i = lax.broadcasted_iota(jnp.int32, (nseq * seq, nseq), 1)
    return _select(r == i * seq + offset, per_seq)


def _shift_rows(x, shift, *, seq, tile_has_whole_seqs, before):
    rows = x.shape[0]
    rolled = pltpu.roll(x, shift, 0)
    r = lax.broadcasted_iota(jnp.int32, (rows, 1), 0)
    if tile_has_whole_seqs:
        head = _seq_place(before[0], 0, seq=seq)
        for s in range(1, shift):
            head = head + _seq_place(before[s], s, seq=seq)
        return jnp.where((r % seq) < shift, head, rolled)
    out = rolled
    for s in range(shift):
        out = jnp.where(r == s, before[s:s + 1, :], out)
    return out


def _scan_block(seq):
    return _pick_tile(seq, (32, 16, 8))


def _hk_to_kh(a, axis):
    shape = a.shape
    a = a.reshape(shape[:axis] + (R_HEADS, R_HD) + shape[axis + 1:])
    return jnp.swapaxes(a, axis, axis + 1).reshape(shape)


def _kh_to_hk(a, axis):
    shape = a.shape
    a = a.reshape(shape[:axis] + (R_HD, R_HEADS) + shape[axis + 1:])
    return jnp.swapaxes(a, axis, axis + 1).reshape(shape)


def _sections(a, axis, fn):
    rw = R_HEADS * R_HD
    idx = lambda lo, hi: tuple(slice(lo, hi) if d == axis else slice(None) for d in range(a.ndim))
    parts = [fn(a[idx(i * rw, (i + 1) * rw)], axis) for i in range(3)]
    if a.shape[axis] > 3 * rw:
        parts.append(a[idx(3 * rw, a.shape[axis])])
    return jnp.concatenate(parts, axis=axis)


def _rwkv_prep_kernel(x_ref, gn_ref, wp_ref, bp_ref, st_ref, mu_ref, wl_ref, w0_ref, a0_ref,
                      rr_ref, rk_ref, rv_ref, dec_ref, a_ref, g_ref, sh_ref, carry_sc,
                      *, seq, whole_seqs):
    xin = x_ref[...]
    tb = xin.shape[0]
    rw = R_HEADS * R_HD
    ms = jnp.mean(xin * xin, axis=-1, keepdims=True)
    h = (xin * lax.rsqrt(ms + RMS_EPS) * gn_ref[...]).astype(BF16)
    if not whole_seqs:
        @pl.when(pl.program_id(1) == 0)
        def _():
            carry_sc[...] = st_ref[0]

    def proj(lo, hi):
        return jnp.dot(h, wp_ref[:, lo:hi], preferred_element_type=F32) + bp_ref[:, lo:hi]

    def shifted(x, lo, hi):
        if whole_seqs:
            prev = _shift_rows(x, 1, seq=seq, tile_has_whole_seqs=True, before=[st_ref[:, lo:hi]])
            sh_ref[:, lo:hi] = _seq_rows(x, seq - 1, seq=seq)
        else:
            prev = _shift_rows(x, 1, seq=seq, tile_has_whole_seqs=False,
                               before=carry_sc[:, lo:hi])
            carry_sc[:, lo:hi] = x[tb - 1:tb, :]
            sh_ref[0, :, lo:hi] = x[tb - 1:tb, :]
        return x + (prev - x) * mu_ref[:, lo:hi]

    x_lora = proj(3 * rw, 3 * rw + R_LORA)
    x_r = proj(0, rw)
    z = shifted(x_lora, 3 * rw, 3 * rw + R_LORA)
    lane = lax.broadcasted_iota(jnp.int32, (1, R_LORA), 1)
    act = jnp.where(lane < 64, jnp.tanh(z), jnp.where(lane < 128, z, _sigmoid(z))).astype(BF16)
    lo_w = jnp.dot(act, wl_ref[:, 0:rw], preferred_element_type=F32)
    x_k = proj(rw, 2 * rw)
    rr_ref[...] = shifted(x_r, 0, rw)
    lo_a = jnp.dot(act, wl_ref[:, rw:2 * rw], preferred_element_type=F32)
    x_v = proj(2 * rw, 3 * rw)
    rk_ref[...] = shifted(x_k, rw, 2 * rw)
    dec_ref[...] = jnp.exp(-jnp.exp(-_softplus(-(w0_ref[...] + lo_w)) - 0.5))
    g_ref[...] = jnp.dot(act, wl_ref[:, 2 * rw:3 * rw], preferred_element_type=F32)
    rv_ref[...] = shifted(x_v, 2 * rw, 3 * rw)
    a_ref[...] = _sigmoid(a0_ref[...] + lo_a)


def _rwkv_prep(x, gn, w_rw, b_rw, shift_state, mu, w_lora, w0, a0, *, row0, batch, seq, name):
    d = x.shape[1]
    c = w_rw.shape[1]
    rw = R_HEADS * R_HD
    n_tok = batch * seq
    whole = seq <= 64
    if whole:
        tb = _pick_tile(n_tok, (256, 128, 64, 32, 16, 8))
        assert tb % seq == 0 and row0 % tb == 0
        grid = (n_tok // tb, 1)
        rb0 = row0 // tb
        tok_map = lambda i, j: (rb0 + i, 0)
        out_map = lambda i, j: (i, 0)
        state = shift_state
        st_spec = pl.BlockSpec((tb // seq, c), out_map)
        sh_shape = jax.ShapeDtypeStruct((batch, c), F32)
    else:
        tb = _pick_tile(seq, (256, 128, 64, 32, 16, 8))
        nt = seq // tb
        grid = (batch, nt)
        rb0 = row0 // tb
        tok_map = lambda b, j: (rb0 + b * nt + j, 0)
        out_map = lambda b, j: (b * nt + j, 0)
        state = shift_state.reshape(batch, 1, c)
        st_spec = pl.BlockSpec((1, 1, c), lambda b, j: (b, 0, 0))
        sh_shape = jax.ShapeDtypeStruct((batch, 1, c), F32)
    fixed = lambda i, j: (0, 0)
    tok_out = jax.ShapeDtypeStruct((n_tok, rw), F32)
    tok_spec = pl.BlockSpec((tb, rw), out_map)
    blocks = (4 * _nbytes((tb, c), F32) + 6 * _nbytes((tb, rw), F32) + _nbytes(w_lora.shape, BF16)
              + _nbytes(w_rw.shape, BF16))
    outs = pl.pallas_call(
        functools.partial(_rwkv_prep_kernel, seq=seq, whole_seqs=whole),
        name=name,
        grid=grid,
        in_specs=[
            pl.BlockSpec((tb, d), tok_map),
            pl.BlockSpec((1, d), fixed),
            pl.BlockSpec(w_rw.shape, fixed),
            pl.BlockSpec((1, c), fixed),
            st_spec,
            pl.BlockSpec((1, c), fixed),
            pl.BlockSpec(w_lora.shape, fixed),
            pl.BlockSpec((1, rw), fixed),
            pl.BlockSpec((1, rw), fixed),
        ],
        out_specs=[tok_spec] * 6 + [st_spec],
        out_shape=[tok_out] * 6 + [sh_shape],
        scratch_shapes=[pltpu.VMEM((1, c), F32)],
        compiler_params=_params(("arbitrary", "arbitrary"), blocks),
    )(x, gn, w_rw, b_rw, state, mu, w_lora, w0, a0)
    return (*outs[:6], outs[6].reshape(batch, c))


def _rwkv_scan_kernel(*refs, tb):
    (rr_ref, rk_ref, dec_ref, a_ref, rv_ref, kks_ref, ka_ref, rrk_ref, gng_ref, gnb_ref,
     s0_ref) = refs[:11]
    y_ref, s1_ref, s_sc, w_sc, kka_sc, km_sc, nkk_sc, wr_sc, coef_sc = refs[-9:]
    j = pl.program_id(1)
    nk = R_HD

    @pl.when(j == 0)
    def _():
        s_sc[...] = s0_ref[0]

    rr = rr_ref[...]
    rk = rk_ref[...]
    a = a_ref[...]
    dec = dec_ref[...]
    kk = rk * kks_ref[...]
    nrm = jnp.sqrt(jnp.sum(kk * kk, axis=0, keepdims=True))
    kk = kk * (1.0 / jnp.maximum(nrm, 1e-12))
    kmod = rk * (1.0 + (a - 1.0) * ka_ref[...])
    kka = kk * a
    g2 = dec.reshape(nk * tb, LANES)
    tpos = lax.broadcasted_iota(jnp.int32, (nk * tb, 1), 0) % tb
    shift = 1
    while shift < tb:
        g2 = g2 * jnp.where(tpos >= shift, pltpu.roll(g2, shift, 0), 1.0)
        shift *= 2
    g_prev = jnp.where(tpos >= 1, pltpu.roll(g2, 1, 0), 1.0).reshape(nk, tb, LANES)
    g_run = g2.reshape(nk, tb, LANES)
    g_inv = 1.0 / g_run
    w_sc[...] = g_run
    kka_sc[...] = kka * g_inv
    km_sc[...] = kmod * g_inv
    nkk_sc[...] = -kk * g_prev
    wr_sc[...] = rr * g_run
    coef_sc[0] = jnp.sum(kka * rr, axis=0)
    coef_sc[1] = jnp.sum(kmod * rr, axis=0)
    coef_sc[2] = jnp.sum(rr * kmod * rrk_ref[...], axis=0)

    def row(ref, t, k):
        return jnp.broadcast_to(ref[k, pl.ds(t, 1), :], (nk, LANES))

    def coef(i, t):
        return jnp.broadcast_to(coef_sc[i, pl.ds(t, 1), :], (nk, LANES))

    def matvecs(t):
        acc_sa = jnp.zeros((nk, LANES), F32)
        acc_y = jnp.zeros((nk, LANES), F32)
        for k in range(nk):
            sk = s_sc[k]
            acc_sa = acc_sa + sk * row(nkk_sc, t, k)
            acc_y = acc_y + sk * row(wr_sc, t, k)
        return acc_sa, acc_y

    def emit(t, sa, y0):
        val = rv_ref[t]
        y = y0 + sa * coef(0, t) + val * coef(1, t)
        mean = jnp.sum(y, axis=0, keepdims=True) * (1.0 / nk)
        d = y - mean
        var = jnp.sum(d * d, axis=0, keepdims=True) * (1.0 / nk)
        yn = d * lax.rsqrt(var + GN_EPS) * gng_ref[...] + gnb_ref[...]
        y_ref[t] = yn + coef(2, t) * val
        return val

    def step(t, carry):
        sa, y0 = carry
        val = emit(t, sa, y0)
        acc_sa = jnp.zeros((nk, LANES), F32)
        acc_y = jnp.zeros((nk, LANES), F32)
        for k in range(nk):
            sk = s_sc[k] + sa * row(kka_sc, t, k) + val * row(km_sc, t, k)
            s_sc[k] = sk
            acc_sa = acc_sa + sk * row(nkk_sc, t + 1, k)
            acc_y = acc_y + sk * row(wr_sc, t + 1, k)
        return acc_sa, acc_y

    sa, y0 = lax.fori_loop(0, tb - 1, step, matvecs(0))
    t_last = tb - 1
    val = emit(t_last, sa, y0)
    for k in range(nk):
        s_sc[k] = ((s_sc[k] + sa * row(kka_sc, t_last, k) + val * row(km_sc, t_last, k))
                   * row(w_sc, t_last, k))

    @pl.when(j == pl.num_programs(1) - 1)
    def _():
        s1_ref[0, 0] = s_sc[...]


LAYOUT_TB = 128


def _lane_slabs(in_ref, p):
    nb = HEADS_PER_LANE_GROUP
    zts = [in_ref[b, :, p * LANES:(p + 1) * LANES].T for b in range(nb)]
    for i in range(SUBLANES):
        d = jnp.concatenate([zt[i * R_HEADS:(i + 1) * R_HEADS, :] for zt in zts], axis=0)
        yield SUBLANES * p + i, d.T


def _lanes_in_kernel(src_ref, dst_ref, *, rows):
    for p in range(R_HD // SUBLANES):
        for i, slab in _lane_slabs(src_ref, p):
            if rows:
                dst_ref[i] = slab
            else:
                dst_ref[:, pl.ds(i, 1), :] = slab[:, None, :]


def _lanes_in(x, *, batch, seq, rows):
    nb = HEADS_PER_LANE_GROUP
    g = batch // nb
    w = R_HEADS * R_HD
    if seq % LAYOUT_TB != 0 and (g * seq) % LAYOUT_TB == 0:
        x = x.reshape(g, nb, seq, w).transpose(1, 0, 2, 3).reshape(nb * g * seq, w)
        return _lanes_in(x, batch=nb, seq=g * seq, rows=rows)
    if seq % LAYOUT_TB != 0:
        x = x.reshape(g, nb, seq, R_HD, R_HEADS)
        if rows:
            return x.transpose(3, 0, 2, 1, 4).reshape(R_HD, g * seq, LANES)
        return x.transpose(0, 2, 3, 1, 4).reshape(g * seq, R_HD, LANES)
    nt = seq // LAYOUT_TB
    if rows:
        out_spec = pl.BlockSpec((R_HD, LAYOUT_TB, LANES), lambda gi, j: (0, gi * nt + j, 0))
        out_shape = jax.ShapeDtypeStruct((R_HD, g * seq, LANES), F32)
    else:
        out_spec = pl.BlockSpec((LAYOUT_TB, R_HD, LANES), lambda gi, j: (gi * nt + j, 0, 0))
        out_shape = jax.ShapeDtypeStruct((g * seq, R_HD, LANES), F32)
    return pl.pallas_call(
        functools.partial(_lanes_in_kernel, rows=rows),
        name="lanes_in_rows" if rows else "lanes_in_full",
        grid=(g, nt),
        in_specs=[pl.BlockSpec((nb, LAYOUT_TB, w), lambda gi, j: (gi, j, 0))],
        out_specs=out_spec,
        out_shape=out_shape,
        compiler_params=_params(("arbitrary", "arbitrary"), 2 * _nbytes((nb, LAYOUT_TB, w), F32)),
    )(x.reshape(batch, seq, w))


def _lanes_out_kernel(y_ref, o_ref):
    nb = HEADS_PER_LANE_GROUP
    for p in range(R_HD // SUBLANES):
        pieces = [[] for _ in range(nb)]
        for i in range(SUBLANES):
            st = y_ref[:, SUBLANES * p + i, :].T
            for b in range(nb):
                pieces[b].append(st[b * R_HEADS:(b + 1) * R_HEADS, :])
        for b in range(nb):
            zt = jnp.concatenate(pieces[b], axis=0)
            o_ref[b, :, p * LANES:(p + 1) * LANES] = zt.T


def _lanes_out(y, *, batch, seq):
    nb = HEADS_PER_LANE_GROUP
    g = batch // nb
    w = R_HEADS * R_HD
    if seq % LAYOUT_TB != 0 and (g * seq) % LAYOUT_TB == 0:
        tok = _lanes_out(y, batch=nb, seq=g * seq)
        return tok.reshape(nb, g, seq, w).transpose(1, 0, 2, 3).reshape(batch * seq, w)
    if seq % LAYOUT_TB != 0:
        y = y.reshape(g, seq, R_HD, nb, R_HEADS)
        return y.transpose(0, 3, 1, 2, 4).reshape(batch * seq, w)
    nt = seq // LAYOUT_TB
    out = pl.pallas_call(
        _lanes_out_kernel,
        name="lanes_out",
        grid=(g, nt),
        in_specs=[pl.BlockSpec((LAYOUT_TB, R_HD, LANES), lambda gi, j: (gi * nt + j, 0, 0))],
        out_specs=pl.BlockSpec((nb, LAYOUT_TB, w), lambda gi, j: (gi, j, 0)),
        out_shape=jax.ShapeDtypeStruct((batch, seq, w), F32),
        compiler_params=_params(("arbitrary", "arbitrary"), 2 * _nbytes((nb, LAYOUT_TB, w), F32)),
    )(y)
    return out.reshape(batch * seq, w)


def _param_lanes(p):
    p = p.reshape(R_HEADS, R_HD).T
    return jnp.tile(p, (1, HEADS_PER_LANE_GROUP))


def _rwkv_scan(rr, rk, rv, dec, a, kks, ka, rrk, gng, gnb, s0, s_buf, *, layer, depth, batch, seq,
               name):
    assert batch % HEADS_PER_LANE_GROUP == 0
    g = batch // HEADS_PER_LANE_GROUP
    tb = _scan_block(seq)
    nt = seq // tb
    rows = [_lanes_in(v, batch=batch, seq=seq, rows=True) for v in (rr, rk, dec, a)]
    val = _lanes_in(rv, batch=batch, seq=seq, rows=False)
    s0l = s0.reshape(g, HEADS_PER_LANE_GROUP, R_HEADS, R_HD, R_HD).transpose(0, 4, 3, 1, 2)
    s0l = s0l.reshape(g, R_HD, R_HD, LANES)
    row_spec = pl.BlockSpec((R_HD, tb, LANES), lambda gi, j: (0, gi * nt + j, 0))
    val_spec = pl.BlockSpec((tb, R_HD, LANES), lambda gi, j: (gi * nt + j, 0, 0))
    kpar_spec = pl.BlockSpec((R_HD, 1, LANES), lambda gi, j: (0, 0, 0))
    vpar_spec = pl.BlockSpec((R_HD, LANES), lambda gi, j: (0, 0))
    st_spec = pl.BlockSpec((1, R_HD, R_HD, LANES), lambda gi, j: (gi, 0, 0, 0))
    vec_bytes = _nbytes((tb, R_HD, LANES), F32)
    blocks = 11 * vec_bytes + 3 * _nbytes((R_HD, R_HD, LANES), F32)
    kpar = lambda p: _param_lanes(p)[:, None, :]
    in_specs = [row_spec] * 4 + [val_spec] + [kpar_spec] * 3 + [vpar_spec] * 2 + [st_spec]
    args = [*rows, val, kpar(kks), kpar(ka), kpar(rrk), _param_lanes(gng), _param_lanes(gnb), s0l]
    aliases = {}
    if s_buf is not None:
        aliases[len(args)] = 1
        in_specs.append(ANY_SPEC)
        args.append(s_buf)
    y, s_all = pl.pallas_call(
        functools.partial(_rwkv_scan_kernel, tb=tb),
        name=name,
        grid=(g, nt),
        in_specs=in_specs,
        out_specs=[val_spec,
                   pl.BlockSpec((1, 1, R_HD, R_HD, LANES), lambda gi, j: (layer, gi, 0, 0, 0))],
        out_shape=[
            jax.ShapeDtypeStruct((g * seq, R_HD, LANES), F32),
            jax.ShapeDtypeStruct((depth,) + s0l.shape, F32),
        ],
        scratch_shapes=[pltpu.VMEM((R_HD, R_HD, LANES), F32)]
        + [pltpu.VMEM((R_HD, tb, LANES), F32)] * 5
        + [pltpu.VMEM((3, tb, LANES), F32)],
        input_output_aliases=aliases,
        compiler_params=_params(("arbitrary", "arbitrary"), blocks),
    )(*args)
    return _lanes_out(y, batch=batch, seq=seq), s_all


def _state_from_lanes(s_all, *, batch):
    depth, g = s_all.shape[:2]
    s = s_all.reshape(depth, g, R_HD, R_HD, HEADS_PER_LANE_GROUP, R_HEADS)
    return s.transpose(0, 1, 4, 5, 3, 2).reshape(depth, batch, R_HEADS, R_HD, R_HD)


def _mix_out_kernel(hm_ref, yr_ref, g_ref, gm_ref, gr_ref, x_ref, wm_ref, wr_ref, wo_ref, *rest):
    o_ref = rest[-1]
    bm = jnp.dot(hm_ref[...].astype(BF16), wm_ref[...], preferred_element_type=F32)
    yr = (yr_ref[...] * g_ref[...]).astype(BF16)
    br = jnp.dot(yr, wr_ref[...], preferred_element_type=F32)
    merged = (_sigmoid(gm_ref[...].astype(F32)) * bm
              + _sigmoid(gr_ref[...].astype(F32)) * br)
    o_ref[...] = x_ref[...] + jnp.dot(merged.astype(BF16), wo_ref[...], preferred_element_type=F32)


def _mix_out(hm, yr, g, p_mix, x, wm, wr, wo, out_buf, *, row0, name):
    n_all, d = x.shape
    n = yr.shape[0]
    tm = _pick_tile(n, (256, 128, 64, 32, 16, 8))
    assert row0 % tm == 0
    rb0 = row0 // tm
    gm_blk = (2 * M_HEADS * (M_DK + M_DV)) // d
    shared = lambda i: (rb0 + i, 0)
    own = lambda i: (i, 0)
    fixed = lambda i: (0, 0)
    aliased = out_buf is not None
    in_specs = [
        pl.BlockSpec((tm, wm.shape[0]), own),
        pl.BlockSpec((tm, wr.shape[0]), own),
        pl.BlockSpec((tm, wr.shape[0]), own),
        pl.BlockSpec((tm, d), lambda i: (rb0 + i, gm_blk)),
        pl.BlockSpec((tm, d), lambda i: (rb0 + i, gm_blk + 1)),
        pl.BlockSpec((tm, d), shared),
        pl.BlockSpec(wm.shape, fixed),
        pl.BlockSpec(wr.shape, fixed),
        pl.BlockSpec(wo.shape, fixed),
    ]
    args = [hm, yr, g, p_mix, p_mix, x, wm, wr, wo]
    if aliased:
        in_specs.append(ANY_SPEC)
        args.append(out_buf)
    blocks = 7 * _nbytes((tm, d), F32) + 3 * _nbytes((d, d), BF16)
    return pl.pallas_call(
        _mix_out_kernel,
        name=name,
        grid=(n // tm,),
        in_specs=in_specs,
        out_specs=pl.BlockSpec((tm, d), shared),
        out_shape=jax.ShapeDtypeStruct((n_all, d), F32),
        input_output_aliases={9: 0} if aliased else {},
        compiler_params=_params(("arbitrary",), blocks),
    )(*args)


FFN_CHUNK = 256


def _ffn_kernel(x_ref, gn_ref, wu_ref, cs_ref, cw_ref, cb_ref, wd_ref, *rest, seq, whole_seqs):
    o_ref, st_ref, carry_sc = rest[-3:]
    x = x_ref[...]
    tb = x.shape[0]
    dff = wd_ref.shape[0]
    ms = jnp.mean(x * x, axis=-1, keepdims=True)
    h = (x * lax.rsqrt(ms + RMS_EPS) * gn_ref[...]).astype(BF16)
    if not whole_seqs:
        @pl.when(pl.program_id(1) == 0)
        def _():
            carry_sc[...] = cs_ref[0]
    def up(c0):
        return [jnp.dot(h, wu_ref[:, lo:lo + FFN_CHUNK], preferred_element_type=F32)
                for lo in (c0, dff + c0)]

    def down(acc, act, c0):
        return acc + jnp.dot(act, wd_ref[c0:c0 + FFN_CHUNK, :], preferred_element_type=F32)

    acc = x
    starts = list(range(0, dff, FFN_CHUNK))
    u_next = up(starts[0])
    pending = None
    for ci, c0 in enumerate(starts):
        u_cur = u_next
        if ci + 1 < len(starts):
            u_next = up(starts[ci + 1])
        if pending is not None:
            acc = down(acc, *pending)
        halves = []
        for lo, u in zip((c0, dff + c0), u_cur):
            cols = slice(lo, lo + FFN_CHUNK)
            if whole_seqs:
                before = [cs_ref[0, :, cols], cs_ref[1, :, cols]]
                p1 = _shift_rows(u, 1, seq=seq, tile_has_whole_seqs=True, before=before[1:])
                p2 = _shift_rows(u, 2, seq=seq, tile_has_whole_seqs=True, before=before)
                st_ref[0, :, cols] = _seq_rows(u, seq - 2, seq=seq)
                st_ref[1, :, cols] = _seq_rows(u, seq - 1, seq=seq)
            else:
                carry = carry_sc[:, cols]
                p1 = _shift_rows(u, 1, seq=seq, tile_has_whole_seqs=False, before=carry[1:2, :])
                p2 = _shift_rows(u, 2, seq=seq, tile_has_whole_seqs=False, before=carry)
                carry_sc[:, cols] = u[tb - 2:tb, :]
                st_ref[0, :, cols] = u[tb - 2:tb, :]
            halves.append(cb_ref[:, cols] + p2 * cw_ref[0:1, cols] + p1 * cw_ref[1:2, cols]
                          + u * cw_ref[2:3, cols])
        gate, val = halves
        pending = ((gate * _sigmoid(gate) * val).astype(BF16), c0)
    o_ref[...] = down(acc, *pending)


def _ffn(x, gn, wu, conv_state, cw, cb, wd, out_buf, *, row0, batch, seq, name):
    n_all, d = x.shape
    c = wu.shape[1]
    n_tok = batch * seq
    taps = cw.shape[0]
    assert taps == 3 and seq >= taps - 1 and wd.shape[0] % FFN_CHUNK == 0
    whole = seq <= 64
    if whole:
        tb = _pick_tile(n_tok, (128, 64, 32, 16, 8))
        assert tb % seq == 0 and row0 % tb == 0
        grid = (n_tok // tb, 1)
        rb0 = row0 // tb
        tok_map = lambda i, j: (rb0 + i, 0)
        state = jnp.swapaxes(conv_state, 0, 1)
        st_shape = jax.ShapeDtypeStruct((taps - 1, batch, c), F32)
        st_spec = pl.BlockSpec((taps - 1, tb // seq, c), lambda i, j: (0, i, 0))
    else:
        tb = _pick_tile(seq, (256, 128, 64, 32, 16, 8))
        nt = seq // tb
        grid = (batch, nt)
        rb0 = row0 // tb
        tok_map = lambda b, j: (rb0 + b * nt + j, 0)
        state = conv_state
        st_shape = jax.ShapeDtypeStruct((batch, taps - 1, c), F32)
        st_spec = pl.BlockSpec((1, taps - 1, c), lambda b, j: (b, 0, 0))
    fixed = lambda i, j: (0, 0)
    aliased = out_buf is not None
    in_specs = [
        pl.BlockSpec((tb, d), tok_map),
        pl.BlockSpec((1, d), fixed),
        pl.BlockSpec(wu.shape, fixed),
        st_spec,
        pl.BlockSpec(cw.shape, fixed),
        pl.BlockSpec((1, c), fixed),
        pl.BlockSpec(wd.shape, fixed),
    ]
    args = [x, gn, wu, state, cw, cb, wd]
    if aliased:
        in_specs.append(ANY_SPEC)
        args.append(out_buf)
    blocks = _nbytes(wu.shape, BF16) + _nbytes(wd.shape, BF16) + 6 * _nbytes((tb, d), F32)
    y, st = pl.pallas_call(
        functools.partial(_ffn_kernel, seq=seq, whole_seqs=whole),
        name=name,
        grid=grid,
        in_specs=in_specs,
        out_specs=[pl.BlockSpec((tb, d), tok_map), st_spec],
        out_shape=[jax.ShapeDtypeStruct((n_all, d), F32), st_shape],
        scratch_shapes=[pltpu.VMEM((taps - 1, c), F32)],
        input_output_aliases={7: 0} if aliased else {},
        compiler_params=_params(("arbitrary", "arbitrary"), blocks),
    )(*args)
    return y, (jnp.swapaxes(st, 0, 1) if whole else st)


def _final_norm_kernel(x_ref, g_ref, o_ref):
    x = x_ref[...]
    ms = jnp.mean(x * x, axis=-1, keepdims=True)
    o_ref[...] = x * lax.rsqrt(ms + RMS_EPS) * g_ref[...]


def _final_norm(x, g, *, row0, n, name):
    d = x.shape[1]
    tm = _pick_tile(n, (1024, 512, 256, 128, 64, 32, 16, 8))
    assert row0 % tm == 0
    rb0 = row0 // tm
    return pl.pallas_call(
        _final_norm_kernel,
        name=name,
        grid=(n // tm,),
        in_specs=[pl.BlockSpec((tm, d), lambda i: (rb0 + i, 0)),
                  pl.BlockSpec((1, d), lambda i: (0, 0))],
        out_specs=pl.BlockSpec((tm, d), lambda i: (i, 0)),
        out_shape=jax.ShapeDtypeStruct((n, d), F32),
        compiler_params=_params(("arbitrary",), 2 * _nbytes((tm, d), F32)),
    )(x, g)


def kernel(x_prompt, x_sample, state_mlstm_C, state_mlstm_n, state_mlstm_m, state_rwkv_S, state_rwkv_shift, state_ffn_conv, norm_mix_g, w_in, b_in, mu_shift, m_norm_g, r_w0, r_w_up, r_a0, r_a_up, r_g_up, r_k_k, r_k_a, r_r_k, r_gn_g, r_gn_b, w_br_m, w_br_r, w_out, norm_ffn_g, w_up, conv_w, conv_b, w_down, norm_final_g):
    bp, tp, d = x_prompt.shape
    bs, ts, _ = x_sample.shape
    depth = w_in.shape[0]
    n_p = bp * tp
    n_s = bs * ts
    rw = R_HEADS * R_HD
    qkvo = 2 * M_HEADS * (M_DK + M_DV)
    n_gate = 2 * M_HEADS
    mix_cols = qkvo + n_gate + 2 * d
    rw_cols = 3 * rw + R_LORA
    dff2 = w_up.shape[2]
    assert w_in.shape[2] == mix_cols + rw_cols

    x = jnp.concatenate([x_prompt.reshape(n_p, d), x_sample.reshape(n_s, d)], axis=0)

    zc = jnp.zeros((1, bp, M_HEADS, M_DV, M_DK), F32)
    c_all_p = c_all_s = None
    s_all_p = s_all_s = None
    zn = jnp.zeros((bp, M_HEADS, M_DK), F32)
    zm = jnp.zeros((bp, M_HEADS), F32)
    zs = jnp.zeros((bp, R_HEADS, R_HD, R_HD), F32)
    zsh = jnp.zeros((bp, rw_cols), F32)
    zcv = jnp.zeros((bp, conv_w.shape[1] - 1, dff2), F32)

    outs_p = [[] for _ in range(4)]
    outs_s = [[] for _ in range(4)]
    for l in range(depth):
        wl = w_in[l]
        bl = b_in[l]
        w_mix = jnp.concatenate([wl[:, :qkvo], wl[:, qkvo + n_gate:mix_cols]], axis=1).astype(BF16)
        b_mix = jnp.concatenate([bl[:qkvo], bl[qkvo + n_gate:mix_cols]])[None]
        w_gate = jnp.pad(wl[:, qkvo:qkvo + n_gate], ((0, 0), (0, LANES - n_gate))).astype(BF16)
        b_gate = jnp.pad(bl[qkvo:qkvo + n_gate], (0, LANES - n_gate))[None]
        w_rw = _sections(wl[:, mix_cols:], 1, _hk_to_kh).astype(BF16)
        b_rw = _sections(bl[mix_cols:], 0, _hk_to_kh)[None]
        mu_rw = _sections(mu_shift[l], 0, _hk_to_kh)[None]
        w_lora = jnp.zeros((R_LORA, 3 * rw), F32)
        w_lora = w_lora.at[0:64, 0:rw].set(r_w_up[l]).at[64:128, rw:2 * rw].set(r_a_up[l])
        w_lora = _sections(w_lora.at[128:256, 2 * rw:].set(r_g_up[l]), 1, _hk_to_kh).astype(BF16)
        w_br_r_kh = _hk_to_kh(w_br_r[l], 0).astype(BF16)
        g_mix = norm_mix_g[l][None]

        p_mix, gates = _in_proj(x, g_mix, w_mix, b_mix, w_gate, b_gate, tn=1024,
                                name="in_proj_mix", out_dtype=BF16)

        gn = m_norm_g[l].reshape(1, M_HEADS * M_DV)
        hm_p, c_all_p, nn_p, m_p = _mlstm(
            p_mix, gates, zc, zn, zm, gn, c_all_p, layer=l, depth=depth,
            row0=0, batch=bp, seq=tp, name="mlstm_prompt")
        hm_s, c_all_s, nn_s, m_s = _mlstm(
            p_mix, gates, state_mlstm_C, state_mlstm_n[l], state_mlstm_m[l], gn, c_all_s,
            layer=l, depth=depth, row0=n_p, batch=bs, seq=ts, name="mlstm_sample")

        prep_args = (mu_rw, w_lora, _hk_to_kh(r_w0[l], 0)[None], _hk_to_kh(r_a0[l], 0)[None])
        rr_p, rk_p, rv_p, wp_p, ap_p, g_p, sh_p = _rwkv_prep(
            x, g_mix, w_rw, b_rw, zsh, *prep_args, row0=0, batch=bp, seq=tp,
            name="rwkv_prep_prompt")
        rr_s, rk_s, rv_s, wp_s, ap_s, g_s, sh_s = _rwkv_prep(
            x, g_mix, w_rw, b_rw, _sections(state_rwkv_shift[l], 1, _hk_to_kh), *prep_args,
            row0=n_p, batch=bs, seq=ts, name="rwkv_prep_sample")
        sh_p = _sections(sh_p, 1, _kh_to_hk)
        sh_s = _sections(sh_s, 1, _kh_to_hk)
        scan_args = (r_k_k[l], r_k_a[l], r_r_k[l], r_gn_g[l], r_gn_b[l])
        y_p, s_all_p = _rwkv_scan(rr_p, rk_p, rv_p, wp_p, ap_p, *scan_args, zs, s_all_p, layer=l,
                                  depth=depth, batch=bp, seq=tp, name="rwkv_scan_prompt")
        y_s, s_all_s = _rwkv_scan(rr_s, rk_s, rv_s, wp_s, ap_s, *scan_args, state_rwkv_S[l],
                                  s_all_s, layer=l, depth=depth, batch=bs, seq=ts,
                                  name="rwkv_scan_sample")

        w_branches = (w_br_m[l].astype(BF16), w_br_r_kh, w_out[l].astype(BF16))
        v_w = M_HEADS * M_DV
        x_mid = _mix_out(hm_p.reshape(n_p, v_w), y_p, g_p, p_mix, x, *w_branches, None, row0=0,
                         name="mix_out_prompt")
        x_mid = _mix_out(hm_s.reshape(n_s, v_w), y_s, g_s, p_mix, x, *w_branches, x_mid, row0=n_p,
                         name="mix_out_sample")

        ffn_w = (conv_w[l], conv_b[l][None], w_down[l].astype(BF16))
        g_ffn = norm_ffn_g[l][None]
        wu = w_up[l].astype(BF16)
        x, cv_p = _ffn(x_mid, g_ffn, wu, zcv, *ffn_w, None, row0=0, batch=bp, seq=tp,
                       name="ffn_prompt")
        x, cv_s = _ffn(x_mid, g_ffn, wu, state_ffn_conv[l], *ffn_w, x, row0=n_p, batch=bs, seq=ts,
                       name="ffn_sample")

        for lst, val in zip(outs_p, (nn_p.reshape(bp, M_HEADS, M_DK), m_p.reshape(bp, M_HEADS),
                                     sh_p, cv_p)):
            lst.append(val)
        for lst, val in zip(outs_s, (nn_s.reshape(bs, M_HEADS, M_DK), m_s.reshape(bs, M_HEADS),
                                     sh_s, cv_s)):
            lst.append(val)

    g_fin = norm_final_g[None]
    y_prompt = _final_norm(x, g_fin, row0=0, n=n_p, name="final_norm_prompt").reshape(bp, tp, d)
    y_sample = _final_norm(x, g_fin, row0=n_p, n=n_s, name="final_norm_sample").reshape(bs, ts, d)
    n_p_all, m_p_all, sh_p_all, cv_p_all = [jnp.stack(o) for o in outs_p]
    n_s_all, m_s_all, sh_s_all, cv_s_all = [jnp.stack(o) for o in outs_s]
    return (y_prompt, y_sample,
            c_all_p, n_p_all, m_p_all, _state_from_lanes(s_all_p, batch=bp), sh_p_all, cv_p_all,
            c_all_s, n_s_all, m_s_all, _state_from_lanes(s_all_s, batch=bs), sh_s_all, cv_s_all)
```
